```python
import math
import jax, jax.numpy as jnp
from jax import lax
import numpy as np

D_MODEL = 2048
BATCH = 4
SEQ = 2048
DEPTH = 4
DEC_BATCH = 32
DEC_SEQ = 8
PAST_LEN = 16384
PAGE_SIZE = 128

N_MIXERS = 4
N_A = len(range(0, DEPTH, N_MIXERS))
N_B = len(range(1, DEPTH, N_MIXERS))
N_C = len(range(2, DEPTH, N_MIXERS))
N_D = len(range(3, DEPTH, N_MIXERS))
N_DENSE = len(range(0, DEPTH, 2))
N_MOE = len(range(1, DEPTH, 2))

CONV_W = 3
HEAD_DIM = 64
N_ATTN_HEADS = 24
ATTN_WIDTH = N_ATTN_HEADS * HEAD_DIM
DIL_CFG = ((128, 1), (512, 4), (2048, 16))
N_DIL = len(DIL_CFG)
B_HG = N_ATTN_HEADS // N_DIL
C_WINDOW = 128
C_KV_HEADS = 4
C_GROUP = N_ATTN_HEADS // C_KV_HEADS
ATTN_BLOCK = 128
CHUNK = 128
SG_GROUPS = 8
SG_WIDTH = D_MODEL
REL_BUCKETS = 32
REL_MAX_DIST = 2048
D_FF = 5632
N_EXPERTS = 8
TOP_K = 2
D_FF_EXPERT = 7168
MOE_BLOCK = 128
RMS_EPS = 1e-6
LN_EPS = 1e-5

kernel_name = "hybrid_conv_dilated_swa_gmlp_decoder_step"


def _rel_bucket(dist):
    n = np.maximum(np.asarray(dist), 0)
    exact = REL_BUCKETS // 2
    ratio = np.log(np.maximum(n, 1).astype(np.float32) / exact) / math.log(REL_MAX_DIST / exact)
    large = np.minimum(exact + (ratio * (REL_BUCKETS - exact)).astype(np.int32), REL_BUCKETS - 1)
    return np.where(n < exact, n, large).astype(np.int32)


def _band_distances():
    return np.arange(ATTN_BLOCK)[:, None] + ATTN_BLOCK - np.arange(2 * ATTN_BLOCK)[None, :]


def _rms_norm(x, g):
    xf = x.astype(jnp.float32)
    y = xf * lax.rsqrt(jnp.mean(xf * xf, axis=-1, keepdims=True) + RMS_EPS)
    return (y * g.astype(jnp.float32)).astype(x.dtype)


def _modulation(cond, w, b):
    mod = (cond @ w + b)[:, None, :]
    shift, scale, gate = jnp.split(mod, 3, axis=-1)
    return shift, scale, gate


def _softmax_parts(logits, sink=None):
    m = jnp.max(logits, axis=-1)
    if sink is not None:
        m = jnp.maximum(m, sink)
    p = jnp.exp(logits - m[..., None])
    l = jnp.sum(p, axis=-1)
    if sink is not None:
        l = l + jnp.exp(sink - m)
    return p, l, m


def _banded_attention(q, k, v, bias, max_dist, sink=None, with_lse=False):
    n, L, hk, g, dh = q.shape
    blk = ATTN_BLOCK
    nb = -(-L // blk)
    extra = nb * blk - L
    qf = jnp.pad(q.astype(jnp.float32), ((0, 0), (0, extra), (0, 0), (0, 0), (0, 0))).reshape(n, nb, blk, hk, g, dh)

    def key_blocks(a):
        ap = jnp.pad(a.astype(jnp.float32), ((0, 0), (blk, extra), (0, 0), (0, 0)))
        prev = ap[:, :nb * blk].reshape(n, nb, blk, hk, dh)
        cur = ap[:, blk:].reshape(n, nb, blk, hk, dh)
        return jnp.concatenate([prev, cur], axis=2)

    kb, vb = key_blocks(k), key_blocks(v)
    logits = jnp.einsum('nbqhgd,nbkhd->nbhgqk', qf, kb) * (dh ** -0.5) + bias.astype(jnp.float32)
    dist = _band_distances()
    key_pos = np.arange(nb)[:, None, None] * blk + np.arange(2 * blk)[None, None, :] - blk
    valid = (dist >= 0) & (dist <= max_dist) & (key_pos >= 0)
    logits = jnp.where(valid[:, None, None], logits, -jnp.inf)
    p, l, m = _softmax_parts(logits, sink)
    o = jnp.einsum('nbhgqk,nbkhd->nbqhgd', p, vb) / jnp.moveaxis(l, -1, 2)[..., None]
    o = o.reshape(n, nb * blk, hk, g, dh)[:, :L]
    if not with_lse:
        return o
    lse = jnp.moveaxis(m + jnp.log(l), -1, 2).reshape(n, nb * blk, hk, g)[:, :L]
    return o, lse


def _to_residues(a, dil):
    n, t = a.shape[:2]
    rest = a.shape[2:]
    return a.reshape((n, t // dil, dil) + rest).swapaxes(1, 2).reshape((n * dil, t // dil) + rest)


def _from_residues(a, dil, n):
    sub = a.shape[1]
    rest = a.shape[2:]
    return a.reshape((n, dil, sub) + rest).swapaxes(1, 2).reshape((n, sub * dil) + rest)


def _merge_dilations(outs, lses):
    alpha = jax.nn.softmax(jnp.stack(lses, axis=0), axis=0)
    merged = jnp.concatenate([outs[g] * alpha[g][..., None] for g in range(N_DIL)], axis=2)
    return merged.reshape(merged.shape[:2] + (ATTN_WIDTH,))


def _dilated_attn_prompt(h, w_qkv, w_o, rel_bias):
    n, t, _ = h.shape
    qkv = (h @ w_qkv).reshape(n, t, 3, N_DIL, B_HG, HEAD_DIM)
    outs, lses, rows = [], [], []
    for gi, (win, dil) in enumerate(DIL_CFG):
        q, k, v = qkv[:, :, 0, gi], qkv[:, :, 1, gi], qkv[:, :, 2, gi]
        bias = rel_bias[_rel_bucket(_band_distances() * dil)][:, :, gi * B_HG:(gi + 1) * B_HG]
        bias = jnp.transpose(bias, (2, 0, 1))[:, None]
        o, lse = _banded_attention(_to_residues(q, dil)[:, :, :, None], _to_residues(k, dil),
                                   _to_residues(v, dil), bias, win // dil, with_lse=True)
        outs.append(_from_residues(o[:, :, :, 0], dil, n))
        lses.append(_from_residues(lse[:, :, :, 0], dil, n))
        keep = min(win, t)
        rows.append(jnp.stack([k[:, t - keep:], v[:, t - keep:]], axis=2))
    y = _merge_dilations(outs, lses).astype(h.dtype) @ w_o
    return y, tuple(rows)


def _dilated_attn_sample(h, caches, w_qkv, w_o, rel_bias):
    n, t, _ = h.shape
    qkv = (h @ w_qkv).reshape(n, t, 3, N_DIL, B_HG, HEAD_DIM)
    outs, lses, rows = [], [], []
    for gi, (win, dil) in enumerate(DIL_CFG):
        q, k, v = qkv[:, :, 0, gi], qkv[:, :, 1, gi], qkv[:, :, 2, gi]
        cache = caches[gi]
        lb = cache.shape[1]
        k_all = jnp.concatenate([cache[:, :, 0], k], axis=1).astype(jnp.float32)
        v_all = jnp.concatenate([cache[:, :, 1], v], axis=1).astype(jnp.float32)
        n_keys = win // dil + 1
        idx = lb + np.arange(t)[:, None] - dil * np.arange(n_keys)[None, :]
        valid = idx >= 0
        idx = np.maximum(idx, 0)
        kg, vg = k_all[:, idx], v_all[:, idx]
        bias = rel_bias[_rel_bucket(dil * np.arange(n_keys))][:, gi * B_HG:(gi + 1) * B_HG].T
        logits = jnp.einsum('nthd,ntkhd->nhtk', q.astype(jnp.float32), kg) * (HEAD_DIM ** -0.5)
        logits = jnp.where(valid, logits + bias.astype(jnp.float32)[None, :, None, :], -jnp.inf)
        p, l, m = _softmax_parts(logits)
        outs.append(jnp.einsum('nhtk,ntkhd->nthd', p, vg) / jnp.swapaxes(l, 1, 2)[..., None])
        lses.append(jnp.swapaxes(m + jnp.log(l), 1, 2))
        rows.append(jnp.stack([k, v], axis=2))
    y = _merge_dilations(outs, lses).astype(h.dtype) @ w_o
    return y, tuple(rows)


def _swa_sink_prompt(h, w_qkv, sink, w_o, rel_bias):
    n, t, _ = h.shape
    qkv = h @ w_qkv
    q = qkv[..., :ATTN_WIDTH].reshape(n, t, C_KV_HEADS, C_GROUP, HEAD_DIM)
    kv = qkv[..., ATTN_WIDTH:].reshape(n, t, 2, C_KV_HEADS, HEAD_DIM)
    bias = rel_bias[_rel_bucket(_band_distances())]
    bias = jnp.transpose(bias, (2, 0, 1)).reshape(C_KV_HEADS, C_GROUP, ATTN_BLOCK, 2 * ATTN_BLOCK)
    s = sink.astype(jnp.float32).reshape(C_KV_HEADS, C_GROUP, 1)
    o = _banded_attention(q, kv[:, :, 0], kv[:, :, 1], bias, C_WINDOW - 1, sink=s)
    y = o.reshape(n, t, ATTN_WIDTH).astype(h.dtype) @ w_o
    keep = min(C_WINDOW, t)
    return y, kv[:, t - keep:]


def _swa_sink_sample(h, cache, w_qkv, sink, w_o, rel_bias):
    n, t, _ = h.shape
    qkv = h @ w_qkv
    q = qkv[..., :ATTN_WIDTH].reshape(n, t, C_KV_HEADS, C_GROUP, HEAD_DIM).astype(jnp.float32)
    kv = qkv[..., ATTN_WIDTH:].reshape(n, t, 2, C_KV_HEADS, HEAD_DIM)
    lc = cache.shape[1]
    kv_all = jnp.concatenate([cache, kv], axis=1).astype(jnp.float32)
    dist = (lc + np.arange(t))[:, None] - np.arange(lc + t)[None, :]
    valid = (dist >= 0) & (dist < C_WINDOW)
    bias = jnp.transpose(rel_bias[_rel_bucket(dist)], (2, 0, 1)).reshape(C_KV_HEADS, C_GROUP, t, lc + t)
    logits = jnp.einsum('nthgd,nkhd->nhgtk', q, kv_all[:, :, 0]) * (HEAD_DIM ** -0.5)
    logits = jnp.where(valid, logits + bias.astype(jnp.float32), -jnp.inf)
    p, l, m = _softmax_parts(logits, sink.astype(jnp.float32).reshape(C_KV_HEADS, C_GROUP, 1))
    o = jnp.einsum('nhgtk,nkhd->nthgd', p, kv_all[:, :, 1]) / jnp.transpose(l, (0, 3, 1, 2))[..., None]
    y = o.reshape(n, t, ATTN_WIDTH).astype(h.dtype) @ w_o
    return y, kv


def _short_conv(h, buf, w_in, conv_w, w_out):
    gate_b, gate_c, z = jnp.split(h @ w_in, 3, axis=-1)
    u = gate_c * z
    t = u.shape[1]
    u_all = jnp.concatenate([buf.astype(u.dtype), u], axis=1)
    conv = conv_w[0] * u_all[:, 0:t]
    for j in range(1, CONV_W):
        conv = conv + conv_w[j] * u_all[:, j:j + t]
    return (gate_b * conv) @ w_out, u_all[:, -(CONV_W - 1):]


def _chunk_gating_mlp(h, w_uv, w_s, b_s, w_out):
    n, t, _ = h.shape
    u, v = jnp.split(jax.nn.gelu(h @ w_uv), 2, axis=-1)
    vf = v.astype(jnp.float32)
    mu = jnp.mean(vf, axis=-1, keepdims=True)
    var = jnp.mean(jnp.square(vf - mu), axis=-1, keepdims=True)
    vn = (vf - mu) * lax.rsqrt(var + LN_EPS)
    nc = -(-t // CHUNK)
    vc = jnp.pad(vn, ((0, 0), (0, nc * CHUNK - t), (0, 0))).reshape(n, nc, CHUNK, SG_GROUPS, SG_WIDTH // SG_GROUPS)
    w_causal = jnp.where(np.tril(np.ones((CHUNK, CHUNK), dtype=bool)), w_s.astype(jnp.float32), 0.0)
    s = jnp.einsum('gts,ncsgd->nctgd', w_causal, vc) + b_s.astype(jnp.float32).T[:, :, None]
    s = s.reshape(n, nc * CHUNK, SG_WIDTH)[:, :t]
    return (u * s.astype(u.dtype)) @ w_out, vn.astype(h.dtype)


def _swiglu(x, w_gate, w_up, w_down):
    return (jax.nn.silu(x @ w_gate) * (x @ w_up)) @ w_down


def _moe_swiglu(h, w_router, b_router, w_gate, w_up, w_down):
    n, t, d = h.shape
    tokens = n * t
    xf = h.reshape(tokens, d)
    logits = xf.astype(jnp.float32) @ w_router.astype(jnp.float32) + b_router.astype(jnp.float32)
    top_val, top_idx = lax.top_k(logits, TOP_K)
    gates = jax.nn.softmax(top_val, axis=-1).astype(h.dtype)
    n_assign = tokens * TOP_K
    e_flat = top_idx.reshape(-1)
    tok_flat = jnp.repeat(jnp.arange(tokens, dtype=jnp.int32), TOP_K)
    order = jnp.argsort(e_flat)
    e_sorted, tok_sorted, g_sorted = e_flat[order], tok_flat[order], gates.reshape(-1)[order]
    counts = jnp.bincount(e_flat, length=N_EXPERTS)
    padded = (counts + MOE_BLOCK - 1) // MOE_BLOCK * MOE_BLOCK
    pad_end = jnp.cumsum(padded)
    pad_start = pad_end - padded
    start = jnp.cumsum(counts) - counts
    dest = pad_start[e_sorted] + jnp.arange(n_assign, dtype=jnp.int32) - start[e_sorted]
    n_blocks = -(-(n_assign + N_EXPERTS * (MOE_BLOCK - 1)) // MOE_BLOCK)
    buf = jnp.zeros((n_blocks * MOE_BLOCK, d), h.dtype).at[dest].set(xf[tok_sorted])
    block_start = jnp.arange(n_blocks, dtype=jnp.int32) * MOE_BLOCK
    block_expert = jnp.minimum(jnp.searchsorted(pad_end, block_start, side='right'), N_EXPERTS - 1)

    def expert_block(args):
        xb, e = args
        return _swiglu(xb, w_gate[e], w_up[e], w_down[e])

    yb = lax.map(expert_block, (buf.reshape(n_blocks, MOE_BLOCK, d), block_expert))
    y = yb.reshape(n_blocks * MOE_BLOCK, d)[dest] * g_sorted[:, None]
    return jax.ops.segment_sum(y, tok_sorted, num_segments=tokens).reshape(n, t, d)


def setup_inputs(seed: int = 0) -> dict:
    key = jax.random.key(seed)
    keys = iter(jax.random.split(key, 40))

    def nrm(shape, std=1.0):
        return jax.random.normal(next(keys), shape, jnp.float32) * std

    d_in = D_MODEL ** -0.5
    lb = [min(win, PAST_LEN) for win, _ in DIL_CFG]
    return {
        "x_prompt": nrm((BATCH, SEQ, D_MODEL)),
        "x_sample": nrm((DEC_BATCH, DEC_SEQ, D_MODEL)),
        "c_prompt": nrm((BATCH, D_MODEL)),
        "c_sample": nrm((DEC_BATCH, D_MODEL)),
        "cache_a_conv": nrm((N_A, DEC_BATCH, CONV_W - 1, D_MODEL)),
        "cache_b_kv0": nrm((N_B, DEC_BATCH, lb[0], 2, B_HG, HEAD_DIM)),
        "cache_b_kv1": nrm((N_B, DEC_BATCH, lb[1], 2, B_HG, HEAD_DIM)),
        "cache_b_kv2": nrm((N_B, DEC_BATCH, lb[2], 2, B_HG, HEAD_DIM)),
        "cache_c_kv": nrm((N_C, DEC_BATCH, min(C_WINDOW, PAST_LEN), 2, C_KV_HEADS, HEAD_DIM)),
        "norm_g": 1.0 + nrm((DEPTH, 4, D_MODEL), 0.05),
        "ada_w": nrm((DEPTH, 2, D_MODEL, 3 * D_MODEL), 0.5 * d_in),
        "ada_b": nrm((DEPTH, 2, 3 * D_MODEL), 0.02),
        "rel_bias": nrm((REL_BUCKETS, N_ATTN_HEADS), 0.5),
        "a_w_in": nrm((N_A, D_MODEL, 3 * D_MODEL), d_in),
        "a_conv_w": nrm((N_A, CONV_W, D_MODEL), CONV_W ** -0.5),
        "a_w_out": nrm((N_A, D_MODEL, D_MODEL), d_in),
        "b_w_qkv": nrm((N_B, D_MODEL, 3 * ATTN_WIDTH), d_in),
        "b_w_o": nrm((N_B, ATTN_WIDTH, D_MODEL), ATTN_WIDTH ** -0.5),
        "c_w_qkv": nrm((N_C, D_MODEL, ATTN_WIDTH + 2 * C_KV_HEADS * HEAD_DIM), d_in),
        "c_sink": nrm((N_C, N_ATTN_HEADS), 1.0),
        "c_w_o": nrm((N_C, ATTN_WIDTH, D_MODEL), ATTN_WIDTH ** -0.5),
        "d_w_uv": nrm((N_D, D_MODEL, 2 * SG_WIDTH), d_in),
        "d_w_s": nrm((N_D, SG_GROUPS, CHUNK, CHUNK), CHUNK ** -0.5),
        "d_b_s": 1.0 + nrm((N_D, SG_GROUPS, CHUNK), 0.1),
        "d_w_out": nrm((N_D, SG_WIDTH, D_MODEL), SG_WIDTH ** -0.5),
        "f_w_gate": nrm((N_DENSE, D_MODEL, D_FF), d_in),
        "f_w_up": nrm((N_DENSE, D_MODEL, D_FF), d_in),
        "f_w_down": nrm((N_DENSE, D_FF, D_MODEL), D_FF ** -0.5),
        "m_w_router": nrm((N_MOE, D_MODEL, N_EXPERTS), d_in),
        "m_b_router": nrm((N_MOE, N_EXPERTS), 0.01),
        "m_w_gate": nrm((N_MOE, N_EXPERTS, D_MODEL, D_FF_EXPERT), d_in),
        "m_w_up": nrm((N_MOE, N_EXPERTS, D_MODEL, D_FF_EXPERT), d_in),
        "m_w_down": nrm((N_MOE, N_EXPERTS, D_FF_EXPERT, D_MODEL), D_FF_EXPERT ** -0.5),
    }


def reference(x_prompt, x_sample, c_prompt, c_sample, cache_a_conv, cache_b_kv0, cache_b_kv1, cache_b_kv2,
              cache_c_kv, norm_g, ada_w, ada_b, rel_bias, a_w_in, a_conv_w, a_w_out, b_w_qkv, b_w_o,
              c_w_qkv, c_sink, c_w_o, d_w_uv, d_w_s, d_b_s, d_w_out, f_w_gate, f_w_up, f_w_down,
              m_w_router, m_b_router, m_w_gate, m_w_up, m_w_down):

    def trunk(x, c, sample):
        n = x.shape[0]
        cond = jax.nn.silu(c)
        new_a, new_b, new_c, new_d = [], [], [], []
        for i in range(DEPTH):
            kind, li = i % N_MIXERS, i // N_MIXERS
            shift, scale, gate = _modulation(cond, ada_w[i, 0], ada_b[i, 0])
            h = _rms_norm(x, norm_g[i, 0]) * (1 + scale) + shift
            if kind == 0:
                buf = cache_a_conv[li] if sample else jnp.zeros((n, CONV_W - 1, D_MODEL), h.dtype)
                out, st = _short_conv(h, buf, a_w_in[li], a_conv_w[li], a_w_out[li])
                new_a.append(st)
            elif kind == 1:
                if sample:
                    out, st = _dilated_attn_sample(h, (cache_b_kv0[li], cache_b_kv1[li], cache_b_kv2[li]),
                                                   b_w_qkv[li], b_w_o[li], rel_bias)
                else:
                    out, st = _dilated_attn_prompt(h, b_w_qkv[li], b_w_o[li], rel_bias)
                new_b.append(st)
            elif kind == 2:
                if sample:
                    out, st = _swa_sink_sample(h, cache_c_kv[li], c_w_qkv[li], c_sink[li], c_w_o[li], rel_bias)
                else:
                    out, st = _swa_sink_prompt(h, c_w_qkv[li], c_sink[li], c_w_o[li], rel_bias)
                new_c.append(st)
            else:
                out, st = _chunk_gating_mlp(h, d_w_uv[li], d_w_s[li], d_b_s[li], d_w_out[li])
                if sample:
                    new_d.append(st)
            x = x + gate * _rms_norm(out, norm_g[i, 1])
            shift, scale, gate = _modulation(cond, ada_w[i, 1], ada_b[i, 1])
            h = _rms_norm(x, norm_g[i, 2]) * (1 + scale) + shift
            fi = i // 2
            if i % 2 == 0:
                out = _swiglu(h, f_w_gate[fi], f_w_up[fi], f_w_down[fi])
            else:
                out = _moe_swiglu(h, m_w_router[fi], m_b_router[fi], m_w_gate[fi], m_w_up[fi], m_w_down[fi])
            x = x + gate * _rms_norm(out, norm_g[i, 3])
        return x, new_a, new_b, new_c, new_d

    y_prompt, pa, pb, pc, _ = trunk(x_prompt, c_prompt, False)
    y_sample, sa, sb, sc, sd = trunk(x_sample, c_sample, True)
    a_p, a_s = jnp.stack(pa), jnp.stack(sa)
    b0_p, b0_s = jnp.stack([r[0] for r in pb]), jnp.stack([r[0] for r in sb])
    b1_p, b1_s = jnp.stack([r[1] for r in pb]), jnp.stack([r[1] for r in sb])
    b2_p, b2_s = jnp.stack([r[2] for r in pb]), jnp.stack([r[2] for r in sb])
    c_p, c_s = jnp.stack(pc), jnp.stack(sc)
    d_s = jnp.stack(sd)
    return (y_prompt, y_sample, a_p, a_s, b0_p, b0_s, b1_p, b1_s, b2_p, b2_s, c_p, c_s, d_s)
```

```python
import functools
import math

import numpy as np
import jax
import jax.numpy as jnp
from jax import lax
from jax.experimental import pallas as pl
from jax.experimental.pallas import tpu as pltpu

D = 2048
N_P, T_P = 4, 2048
N_S, T_S = 32, 8
ROWS_P = N_P * T_P
ROWS_S = N_S * T_S
ROWS = ROWS_P + ROWS_S
DEPTH = 4
HEAD_DIM = 64
N_HEADS = 24
ATTN_W = N_HEADS * HEAD_DIM
DIL_CFG = ((128, 1), (512, 4), (2048, 16))
B_HG = 8
GROUP_W = B_HG * HEAD_DIM
C_WINDOW = 128
C_KV_HEADS = 4
C_GROUP = 6
BLK = 128
SG_GROUPS = 8
SG_GW = D // SG_GROUPS
REL_BUCKETS = 32
REL_MAX_DIST = 2048
N_EXPERTS = 8
RMS_EPS = 1e-6
LN_EPS = 1e-5
NEG = -1e30

LANE = 128
VMEM_LIMIT_CAP = 56 * 1024 * 1024
VMEM_LIMIT_FLOOR = 32 * 1024 * 1024

ROW_BLK = 256
N_PROMPT_BLKS = ROWS_P // ROW_BLK
BLKS_PER_SEQ = T_P // ROW_BLK
MOE_TM = 256
GATHER_CHUNK = 512

BF16 = jnp.bfloat16
F32 = jnp.float32


def _vmem_limit(nbytes):
    return int(min(max(2 * nbytes + (8 << 20), VMEM_LIMIT_FLOOR), VMEM_LIMIT_CAP))


def _params(sem, nbytes):
    return pltpu.CompilerParams(dimension_semantics=sem, vmem_limit_bytes=_vmem_limit(nbytes))


def _weight_changed(be_ref):
    b = pl.program_id(1)
    prev = be_ref[jnp.maximum(b - 1, 0)]
    return jnp.logical_or(b == 0, be_ref[b] != prev)


def _gmm_kernel(be_ref, x_ref, w_ref, o_ref, wc_ref, *, act):
    @pl.when(_weight_changed(be_ref))
    def _():
        wc_ref[...] = w_ref[...].astype(BF16)

    acc = jnp.dot(x_ref[...].astype(BF16), wc_ref[...], preferred_element_type=F32)
    if act == "gelu":
        acc = jax.nn.gelu(acc)
    o_ref[...] = acc.astype(o_ref.dtype)


def _gmm_gated_kernel(be_ref, x_ref, wg_ref, wu_ref, o_ref, wgc_ref, wuc_ref):
    @pl.when(_weight_changed(be_ref))
    def _():
        wgc_ref[...] = wg_ref[...].astype(BF16)
        wuc_ref[...] = wu_ref[...].astype(BF16)

    x = x_ref[...].astype(BF16)
    g = jnp.dot(x, wgc_ref[...], preferred_element_type=F32)
    u = jnp.dot(x, wuc_ref[...], preferred_element_type=F32)
    o_ref[...] = (jax.nn.silu(g) * u).astype(o_ref.dtype)


def _gmm(x, w, block_expert, *, tm, tn, out_dtype, act=None, name):
    m, k = x.shape
    e, k2, n = w.shape
    assert k == k2 and m % tm == 0 and n % tn == 0
    nbytes = (2 * tm * k * x.dtype.itemsize + 2 * k * tn * 4 + k * tn * 2
              + 2 * tm * tn * jnp.dtype(out_dtype).itemsize)
    return pl.pallas_call(
        functools.partial(_gmm_kernel, act=act),
        out_shape=jax.ShapeDtypeStruct((m, n), out_dtype),
        grid_spec=pltpu.PrefetchScalarGridSpec(
            num_scalar_prefetch=1,
            grid=(n // tn, m // tm),
            in_specs=[
                pl.BlockSpec((tm, k), lambda j, b, be: (b, 0)),
                pl.BlockSpec((None, k, tn), lambda j, b, be: (be[b], 0, j)),
            ],
            out_specs=pl.BlockSpec((tm, tn), lambda j, b, be: (b, j)),
            scratch_shapes=[pltpu.VMEM((k, tn), BF16)],
        ),
        compiler_params=_params(("arbitrary", "arbitrary"), nbytes),
        name=name,
    )(block_expert, x, w)


def _gmm_gated(x, wg, wu, block_expert, *, tm, tn, name):
    m, k = x.shape
    e, k2, n = wg.shape
    assert k == k2 and m % tm == 0 and n % tn == 0 and wu.shape == wg.shape
    nbytes = 2 * tm * k * x.dtype.itemsize + 4 * k * tn * 4 + 2 * k * tn * 2 + 2 * tm * tn * 2
    return pl.pallas_call(
        _gmm_gated_kernel,
        out_shape=jax.ShapeDtypeStruct((m, n), BF16),
        grid_spec=pltpu.PrefetchScalarGridSpec(
            num_scalar_prefetch=1,
            grid=(n // tn, m // tm),
            in_specs=[
                pl.BlockSpec((tm, k), lambda j, b, be: (b, 0)),
                pl.BlockSpec((None, k, tn), lambda j, b, be: (be[b], 0, j)),
                pl.BlockSpec((None, k, tn), lambda j, b, be: (be[b], 0, j)),
            ],
            out_specs=pl.BlockSpec((tm, tn), lambda j, b, be: (b, j)),
            scratch_shapes=[pltpu.VMEM((k, tn), BF16), pltpu.VMEM((k, tn), BF16)],
        ),
        compiler_params=_params(("arbitrary", "arbitrary"), nbytes),
        name=name,
    )(block_expert, x, wg, wu)


def _dense(x, w, *, tn, out_dtype, act=None, name, tm=ROW_BLK):
    be = jnp.zeros((x.shape[0] // tm,), jnp.int32)
    return _gmm(x, w[None], be, tm=tm, tn=tn, out_dtype=out_dtype, act=act, name=name)


def _rms(x, g):
    return x * lax.rsqrt(jnp.mean(x * x, axis=-1, keepdims=True) + RMS_EPS) * g


def _route(h, wr_ref, br_ref):
    logits = jnp.dot(h, wr_ref[...], preferred_element_type=F32, precision=lax.Precision.HIGHEST) + br_ref[...]
    lane = lax.broadcasted_iota(jnp.int32, logits.shape, 1)
    logits = jnp.where(lane < N_EXPERTS, logits, -jnp.inf)
    m1 = jnp.max(logits, axis=-1, keepdims=True)
    i1 = jnp.min(jnp.where(logits == m1, lane, LANE), axis=-1, keepdims=True)
    rest = jnp.where(lane == i1, -jnp.inf, logits)
    m2 = jnp.max(rest, axis=-1, keepdims=True)
    i2 = jnp.min(jnp.where(rest == m2, lane, LANE), axis=-1, keepdims=True)
    e2 = jnp.exp(m2 - m1)
    g1 = 1.0 / (1.0 + e2)
    g2 = e2 / (1.0 + e2)
    out = jnp.where(lane == 0, i1.astype(F32),
                    jnp.where(lane == 1, i2.astype(F32),
                              jnp.where(lane == 2, g1, jnp.where(lane == 3, g2, 0.0))))
    return out


def _norm_mod_kernel(x_ref, g_ref, shp_ref, scp_ref, shs_ref, scs_ref, *rest, route):
    if route:
        wr_ref, br_ref, h_ref, r_ref = rest
    else:
        (h_ref,) = rest
    i = pl.program_id(0)
    y = _rms(x_ref[...], g_ref[...])

    def emit(shift, scale):
        h = y * (1.0 + scale) + shift
        h_ref[...] = h.astype(h_ref.dtype)
        if route:
            r_ref[...] = _route(h, wr_ref, br_ref)

    @pl.when(i < N_PROMPT_BLKS)
    def _():
        emit(shp_ref[...], scp_ref[...])

    @pl.when(i == N_PROMPT_BLKS)
    def _():
        emit(shs_ref[...], scs_ref[...])


def _mod_specs(cols):
    specs = []
    for c in cols:
        specs.append(pl.BlockSpec((None, 1, D), lambda i, c=c: (jnp.minimum(i // BLKS_PER_SEQ, N_P - 1), 0, c)))
    for c in cols:
        specs.append(pl.BlockSpec((ROW_BLK, D), lambda i, c=c: (0, c)))
    return specs


def _norm_mod(x, g, modp, mods, router=None, *, name):
    route = router is not None
    row_spec = pl.BlockSpec((ROW_BLK, D), lambda i: (i, 0))
    in_specs = [row_spec, pl.BlockSpec((1, D), lambda i: (0, 0))] + _mod_specs((0, 1))
    args = [x, g.reshape(1, D), modp, modp, mods, mods]
    out_shape = [jax.ShapeDtypeStruct((ROWS, D), BF16)]
    out_specs = [row_spec]
    if route:
        wr, br = router
        in_specs += [pl.BlockSpec((D, LANE), lambda i: (0, 0)), pl.BlockSpec((1, LANE), lambda i: (0, 0))]
        args += [wr, br]
        out_shape.append(jax.ShapeDtypeStruct((ROWS, LANE), F32))
        out_specs.append(pl.BlockSpec((ROW_BLK, LANE), lambda i: (i, 0)))
    nbytes = ROW_BLK * D * (4 + 2 + 8 + 8) * 2 + D * LANE * 4 * 2
    res = pl.pallas_call(
        functools.partial(_norm_mod_kernel, route=route),
        out_shape=out_shape,
        grid=(ROWS // ROW_BLK,),
        in_specs=in_specs,
        out_specs=out_specs,
        compiler_params=_params(("parallel",), nbytes),
        name=name,
    )(*args)
    return res if route else res[0]


def _resid_kernel(x_ref, *rest, moe):
    if moe:
        ya_ref, yb_ref, r_ref, g_ref, gp_ref, gs_ref, o_ref = rest
        r = r_ref[...]
        out = r[:, 2:3] * ya_ref[...] + r[:, 3:4] * yb_ref[...]
    else:
        y_ref, g_ref, gp_ref, gs_ref, o_ref = rest
        out = y_ref[...]
    i = pl.program_id(0)
    upd = _rms(out, g_ref[...])

    @pl.when(i < N_PROMPT_BLKS)
    def _():
        o_ref[...] = x_ref[...] + gp_ref[...] * upd

    @pl.when(i == N_PROMPT_BLKS)
    def _():
        o_ref[...] = x_ref[...] + gs_ref[...] * upd


def _resid(x, y, g, modp, mods, route=None, *, name):
    moe = route is not None
    row_spec = pl.BlockSpec((ROW_BLK, D), lambda i: (i, 0))
    nblk = ROWS // ROW_BLK
    in_specs = [row_spec]
    args = [x]
    if moe:
        in_specs += [row_spec, pl.BlockSpec((ROW_BLK, D), lambda i: (i + nblk, 0)),
                     pl.BlockSpec((ROW_BLK, LANE), lambda i: (i, 0))]
        args += [y, y, route]
    else:
        in_specs += [row_spec]
        args += [y]
    in_specs += [pl.BlockSpec((1, D), lambda i: (0, 0))] + _mod_specs((2,))
    args += [g.reshape(1, D), modp, mods]
    nbytes = ROW_BLK * D * 4 * 6 * 2
    return pl.pallas_call(
        functools.partial(_resid_kernel, moe=moe),
        out_shape=jax.ShapeDtypeStruct((ROWS, D), F32),
        grid=(nblk,),
        in_specs=in_specs,
        out_specs=row_spec,
        compiler_params=_params(("parallel",), nbytes),
        name=name,
    )(*args)


def _conv_kernel(gb_ref, gc_ref, z_ref, buf_ref, cw_ref, y_ref, st_ref, tail_ref, *, rows):
    j = pl.program_id(1)
    u = gc_ref[...] * z_ref[...]

    @pl.when(j == 0)
    def _():
        tail_ref[6:8, :] = buf_ref[...]

    prev2 = tail_ref[6:7, :]
    prev1 = tail_ref[7:8, :]
    r = lax.broadcasted_iota(jnp.int32, u.shape, 0)
    um1 = jnp.where(r == 0, prev1, pltpu.roll(u, 1, axis=0))
    um2 = jnp.where(r == 0, prev2, jnp.where(r == 1, prev1, pltpu.roll(u, 2, axis=0)))
    conv = cw_ref[0:1, :] * um2 + cw_ref[1:2, :] * um1 + cw_ref[2:3, :] * u
    y_ref[...] = (gb_ref[...] * conv).astype(y_ref.dtype)
    tail_ref[...] = u[rows - 8:rows, :]
    st_ref[...] = u[rows - 2:rows, :]


def _short_conv(bcz, buf, conv_w, *, row0, n_seq, seq_len, rows, name):
    per_seq = seq_len // rows
    blk0 = row0 // rows

    def col(c):
        return pl.BlockSpec((rows, D), lambda n, j, c=c: (blk0 + n * per_seq + j, c))

    nbytes = rows * D * 4 * 5 * 2
    return pl.pallas_call(
        functools.partial(_conv_kernel, rows=rows),
        out_shape=[jax.ShapeDtypeStruct((n_seq * seq_len, D), BF16),
                   jax.ShapeDtypeStruct((n_seq, 2, D), F32)],
        grid=(n_seq, per_seq),
        in_specs=[col(0), col(1), col(2),
                  pl.BlockSpec((None, 2, D), lambda n, j: (n, 0, 0)),
                  pl.BlockSpec((3, D), lambda n, j: (0, 0))],
        out_specs=[pl.BlockSpec((rows, D), lambda n, j: (n * per_seq + j, 0)),
                   pl.BlockSpec((None, 2, D), lambda n, j: (n, 0, 0))],
        scratch_shapes=[pltpu.VMEM((8, D), F32)],
        compiler_params=_params(("arbitrary", "arbitrary"), nbytes),
        name=name,
    )(bcz, bcz, bcz, buf, conv_w)


def _rel_bucket(dist):
    n = np.maximum(np.asarray(dist), 0)
    exact = REL_BUCKETS // 2
    ratio = np.log(np.maximum(n, 1).astype(np.float32) / exact) / math.log(REL_MAX_DIST / exact)
    large = np.minimum(exact + (ratio * (REL_BUCKETS - exact)).astype(np.int32), REL_BUCKETS - 1)
    return np.where(n < exact, n, large).astype(np.int32)


def _band_dist():
    return np.arange(BLK)[:, None] + BLK - np.arange(2 * BLK)[None, :]


def _softmax_pv(s, v, sink=None):
    m = jnp.max(s, axis=-1, keepdims=True)
    if sink is not None:
        m = jnp.maximum(m, sink)
    p = jnp.exp(s - m)
    l = jnp.sum(p, axis=-1, keepdims=True)
    if sink is not None:
        l = l + jnp.exp(sink - m)
    o = jnp.dot(p.astype(BF16), v, preferred_element_type=F32) / l
    return o, m + jnp.log(l)


def _nt_dot(a, b):
    return lax.dot_general(a, b, (((1,), (1,)), ((), ())), preferred_element_type=F32)


def _banded_kernel(*refs, n_heads, group, with_lse, has_sink):
    refs = list(refs)
    sink_ref = refs.pop(0) if has_sink else None
    q_ref, kp_ref, kc_ref, vp_ref, vc_ref, bias_ref = refs[:6]
    o_ref = refs[6]
    lse_ref = refs[7] if with_lse else None
    qb = pl.program_id(1)
    q = (q_ref[...] * (HEAD_DIM ** -0.5)).astype(BF16)
    k = jnp.concatenate([kp_ref[...], kc_ref[...]], axis=0).astype(BF16)
    v = jnp.concatenate([vp_ref[...], vc_ref[...]], axis=0).astype(BF16)
    col = lax.broadcasted_iota(jnp.int32, (BLK, 2 * BLK), 1)
    no_prev = jnp.logical_and(qb == 0, col < BLK)
    for h in range(n_heads):
        hk = h // group
        qs = slice(h * HEAD_DIM, (h + 1) * HEAD_DIM)
        ks = slice(hk * HEAD_DIM, (hk + 1) * HEAD_DIM)
        s = _nt_dot(q[:, qs], k[:, ks]) + bias_ref[h]
        s = jnp.where(no_prev, NEG, s)
        sink = sink_ref[h] if has_sink else None
        o, lse = _softmax_pv(s, v[:, ks], sink)
        o_ref[:, qs] = o
        if with_lse:
            lse_ref[:, qs] = jnp.broadcast_to(lse, (BLK, HEAD_DIM))


def _banded_attention(q, k, v, bias, sink=None, *, group, with_lse, name):
    s_, l_, qw = q.shape
    kw = k.shape[-1]
    n_heads = qw // HEAD_DIM
    has_sink = sink is not None
    cur = lambda s, b: (s, b, 0)
    prev = lambda s, b: (s, jnp.maximum(b - 1, 0), 0)
    in_specs = [pl.BlockSpec((None, BLK, qw), cur),
                pl.BlockSpec((None, BLK, kw), prev), pl.BlockSpec((None, BLK, kw), cur),
                pl.BlockSpec((None, BLK, kw), prev), pl.BlockSpec((None, BLK, kw), cur),
                pl.BlockSpec((n_heads, BLK, 2 * BLK), lambda s, b: (0, 0, 0))]
    args = [q, k, k, v, v, bias]
    if has_sink:
        in_specs = [pl.BlockSpec(memory_space=pltpu.SMEM)] + in_specs
        args = [sink] + args
    out_shape = [jax.ShapeDtypeStruct((s_, l_, qw), F32)]
    out_specs = [pl.BlockSpec((None, BLK, qw), cur)]
    if with_lse:
        out_shape.append(jax.ShapeDtypeStruct((s_, l_, qw), F32))
        out_specs.append(pl.BlockSpec((None, BLK, qw), cur))
    nbytes = (BLK * qw * 4 * 3 + BLK * kw * 4 * 4) * 2 + n_heads * BLK * 2 * BLK * 4 * 2
    res = pl.pallas_call(
        functools.partial(_banded_kernel, n_heads=n_heads, group=group, with_lse=with_lse, has_sink=has_sink),
        out_shape=out_shape,
        grid=(s_, l_ // BLK),
        in_specs=in_specs,
        out_specs=out_specs,
        compiler_params=_params(("parallel", "parallel"), nbytes),
        name=name,
    )(*args)
    return res if with_lse else res[0]


def _dil_sample_kernel(q_ref, kn_ref, vn_ref, cache_ref, bc_ref, bn_ref, o_ref, lse_ref):
    q = q_ref[...] * (HEAD_DIM ** -0.5)
    qt = jnp.concatenate([q] * B_HG, axis=0)
    row_h = lax.broadcasted_iota(jnp.int32, qt.shape, 0) // T_S
    col_h = lax.broadcasted_iota(jnp.int32, qt.shape, 1) // HEAD_DIM
    qbd = jnp.where(row_h == col_h, qt, 0.0).astype(BF16)
    pad = jnp.zeros((BLK - T_S, GROUP_W), F32)
    kn = jnp.concatenate([kn_ref[...], pad], axis=0).astype(BF16)
    vn = jnp.concatenate([vn_ref[...], pad], axis=0).astype(BF16)
    kc = cache_ref[:, 0:GROUP_W].astype(BF16)
    vc = cache_ref[:, GROUP_W:2 * GROUP_W].astype(BF16)
    sc = _nt_dot(qbd, kc) + bc_ref[...]
    sn = _nt_dot(qbd, kn) + bn_ref[...]
    m = jnp.maximum(jnp.max(sc, axis=-1, keepdims=True), jnp.max(sn, axis=-1, keepdims=True))
    pc = jnp.exp(sc - m)
    pn = jnp.exp(sn - m)
    l = jnp.sum(pc, axis=-1, keepdims=True) + jnp.sum(pn, axis=-1, keepdims=True)
    big = (jnp.dot(pc.astype(BF16), vc, preferred_element_type=F32)
           + jnp.dot(pn.astype(BF16), vn, preferred_element_type=F32)) / l
    lse = m + jnp.log(l)
    for h in range(B_HG):
        rs = slice(h * T_S, (h + 1) * T_S)
        cs = slice(h * HEAD_DIM, (h + 1) * HEAD_DIM)
        o_ref[:, cs] = big[rs, cs]
        lse_ref[:, cs] = jnp.broadcast_to(lse[rs], (T_S, HEAD_DIM))


def _dil_sample_attention(q, kn, vn, cache, bias_c, bias_n, *, name):
    lb = cache.shape[1]
    seq = lambda n: (n, 0, 0)
    const = lambda n: (0, 0)
    nbytes = lb * 2 * GROUP_W * 4 * 2 + 64 * lb * 4 * 6
    return pl.pallas_call(
        _dil_sample_kernel,
        out_shape=[jax.ShapeDtypeStruct((N_S, T_S, GROUP_W), F32)] * 2,
        grid=(N_S,),
        in_specs=[pl.BlockSpec((None, T_S, GROUP_W), seq)] * 3
        + [pl.BlockSpec((None, lb, 2 * GROUP_W), seq),
           pl.BlockSpec((B_HG * T_S, lb), const), pl.BlockSpec((B_HG * T_S, BLK), const)],
        out_specs=[pl.BlockSpec((None, T_S, GROUP_W), seq)] * 2,
        compiler_params=_params(("parallel",), nbytes),
        name=name,
    )(q, kn, vn, cache, bias_c, bias_n)


def _merge_kernel(o0, o1, o2, l0, l1, l2, out_ref):
    ls = [l0[...], l1[...], l2[...]]
    m = jnp.maximum(jnp.maximum(ls[0], ls[1]), ls[2])
    es = [jnp.exp(l - m) for l in ls]
    tot = es[0] + es[1] + es[2]
    for g, o in enumerate((o0, o1, o2)):
        out_ref[:, g * GROUP_W:(g + 1) * GROUP_W] = (o[...] * (es[g] / tot)).astype(out_ref.dtype)


def _merge_dilations(outs, lses, *, name):
    spec = pl.BlockSpec((ROW_BLK, GROUP_W), lambda i: (i, 0))
    return pl.pallas_call(
        _merge_kernel,
        out_shape=jax.ShapeDtypeStruct((ROWS, ATTN_W), BF16),
        grid=(ROWS // ROW_BLK,),
        in_specs=[spec] * 6,
        out_specs=pl.BlockSpec((ROW_BLK, ATTN_W), lambda i: (i, 0)),
        compiler_params=_params(("parallel",), ROW_BLK * GROUP_W * 4 * 8 * 2),
        name=name,
    )(*outs, *lses)


def _swa_sample_kernel(q_ref, kvn_ref, cache_ref, bc_ref, bn_ref, sink_ref, o_ref):
    kw = C_KV_HEADS * HEAD_DIM
    q = q_ref[...] * (HEAD_DIM ** -0.5)
    pad = jnp.zeros((BLK - T_S, 2 * kw), F32)
    kvn = jnp.concatenate([kvn_ref[...], pad], axis=0).astype(BF16)
    kvc = cache_ref[...].astype(BF16)
    for hk in range(C_KV_HEADS):
        heads = range(hk * C_GROUP, (hk + 1) * C_GROUP)
        qh = jnp.concatenate([q[:, h * HEAD_DIM:(h + 1) * HEAD_DIM] for h in heads], axis=0).astype(BF16)
        ks = slice(hk * HEAD_DIM, (hk + 1) * HEAD_DIM)
        vs = slice(kw + hk * HEAD_DIM, kw + (hk + 1) * HEAD_DIM)
        sc = _nt_dot(qh, kvc[:, ks]) + bc_ref[hk]
        sn = _nt_dot(qh, kvn[:, ks]) + bn_ref[hk]
        sink = sink_ref[hk]
        m = jnp.maximum(jnp.maximum(jnp.max(sc, axis=-1, keepdims=True),
                                    jnp.max(sn, axis=-1, keepdims=True)), sink)
        pc = jnp.exp(sc - m)
        pn = jnp.exp(sn - m)
        l = jnp.sum(pc, axis=-1, keepdims=True) + jnp.sum(pn, axis=-1, keepdims=True) + jnp.exp(sink - m)
        o = (jnp.dot(pc.astype(BF16), kvc[:, vs], preferred_element_type=F32)
             + jnp.dot(pn.astype(BF16), kvn[:, vs], preferred_element_type=F32)) / l
        for gi, h in enumerate(heads):
            o_ref[:, h * HEAD_DIM:(h + 1) * HEAD_DIM] = o[gi * T_S:(gi + 1) * T_S]


def _swa_sample_attention(q, kvn, cache, bias_c, bias_n, sink_tab, *, name):
    kw2 = 2 * C_KV_HEADS * HEAD_DIM
    rows = C_GROUP * T_S
    seq = lambda n: (n, 0, 0)
    const = lambda n: (0, 0, 0)
    return pl.pallas_call(
        _swa_sample_kernel,
        out_shape=jax.ShapeDtypeStruct((N_S, T_S, ATTN_W), F32),
        grid=(N_S,),
        in_specs=[pl.BlockSpec((None, T_S, ATTN_W), seq), pl.BlockSpec((None, T_S, kw2), seq),
                  pl.BlockSpec((None, C_WINDOW, kw2), seq),
                  pl.BlockSpec((C_KV_HEADS, rows, BLK), const), pl.BlockSpec((C_KV_HEADS, rows, BLK), const),
                  pl.BlockSpec((C_KV_HEADS, rows, 1), const)],
        out_specs=pl.BlockSpec((None, T_S, ATTN_W), seq),
        compiler_params=_params(("parallel",), 1 << 20),
        name=name,
    )(q, kvn, cache, bias_c, bias_n, sink_tab)


def _sgu_kernel(u_ref, v_ref, w_ref, b_ref, y_ref, vn_ref, *, rows):
    v = v_ref[...]
    mu = jnp.mean(v, axis=-1, keepdims=True)
    var = jnp.mean(jnp.square(v - mu), axis=-1, keepdims=True)
    vn = (v - mu) * lax.rsqrt(var + LN_EPS)
    vn_ref[...] = vn
    if rows < BLK:
        vn = jnp.concatenate([vn, jnp.zeros((BLK - rows, D), F32)], axis=0)
    vb = vn.astype(BF16)
    causal = (lax.broadcasted_iota(jnp.int32, (BLK, BLK), 0) >= lax.broadcasted_iota(jnp.int32, (BLK, BLK), 1))
    for g in range(SG_GROUPS):
        cs = slice(g * SG_GW, (g + 1) * SG_GW)
        wg = jnp.where(causal, w_ref[g], 0.0).astype(BF16)
        s = jnp.dot(wg, vb[:, cs], preferred_element_type=F32) + b_ref[g]
        y_ref[:, cs] = (u_ref[:, cs] * s[:rows]).astype(y_ref.dtype)


def _spatial_gate(uv, w_s, b_s, *, row0, n_rows, rows, name):
    blk0 = row0 // rows
    nbytes = rows * D * 4 * 4 * 2 + SG_GROUPS * BLK * BLK * 4 * 2
    return pl.pallas_call(
        functools.partial(_sgu_kernel, rows=rows),
        out_shape=[jax.ShapeDtypeStruct((n_rows, D), BF16), jax.ShapeDtypeStruct((n_rows, D), F32)],
        grid=(n_rows // rows,),
        in_specs=[pl.BlockSpec((rows, D), lambda i: (blk0 + i, 0)),
                  pl.BlockSpec((rows, D), lambda i: (blk0 + i, 1)),
                  pl.BlockSpec((SG_GROUPS, BLK, BLK), lambda i: (0, 0, 0)),
                  pl.BlockSpec((SG_GROUPS, BLK, 1), lambda i: (0, 0, 0))],
        out_specs=[pl.BlockSpec((rows, D), lambda i: (i, 0))] * 2,
        compiler_params=_params(("parallel",), nbytes),
        name=name,
    )(uv, uv, w_s, b_s.reshape(SG_GROUPS, BLK, 1))


def _gather_kernel(idx_ref, src_ref, dst_ref, sem):
    base = pl.program_id(0) * GATHER_CHUNK

    def row_copy(r):
        return pltpu.make_async_copy(src_ref.at[idx_ref[base + r]], dst_ref.at[base + r], sem)

    def start(r, carry):
        row_copy(r).start()
        return carry

    def wait(r, carry):
        row_copy(r).wait()
        return carry

    lax.fori_loop(0, GATHER_CHUNK, start, 0)
    lax.fori_loop(0, GATHER_CHUNK, wait, 0)


def _gather_rows(src, idx, *, name):
    n = src.shape[0]
    r = idx.shape[0]
    assert r % GATHER_CHUNK == 0
    src3 = src.reshape(n, D // LANE, LANE)
    out = pl.pallas_call(
        _gather_kernel,
        out_shape=jax.ShapeDtypeStruct((r, D // LANE, LANE), src.dtype),
        grid_spec=pltpu.PrefetchScalarGridSpec(
            num_scalar_prefetch=1,
            grid=(r // GATHER_CHUNK,),
            in_specs=[pl.BlockSpec(memory_space=pl.ANY)],
            out_specs=pl.BlockSpec(memory_space=pl.ANY),
            scratch_shapes=[pltpu.SemaphoreType.DMA],
        ),
        compiler_params=pltpu.CompilerParams(dimension_semantics=("arbitrary",)),
        name=name,
    )(idx, src3)
    return out.reshape(r, D)


N_ASSIGN = 2 * ROWS
_MOE_ROWS_MIN = N_ASSIGN + N_EXPERTS * (MOE_TM - 1)
_MOE_UNIT = MOE_TM * GATHER_CHUNK // math.gcd(MOE_TM, GATHER_CHUNK)
MOE_ROWS = -(-_MOE_ROWS_MIN // _MOE_UNIT) * _MOE_UNIT


def _moe_plan(route):
    e_flat = route[:, 0:2].astype(jnp.int32).reshape(-1)
    onehot = (e_flat[:, None] == jnp.arange(N_EXPERTS, dtype=jnp.int32)[None, :]).astype(jnp.int32)
    csum = jnp.cumsum(onehot, axis=0)
    rank = jnp.sum(csum * onehot, axis=1) - 1
    counts = csum[-1]
    padded = (counts + MOE_TM - 1) // MOE_TM * MOE_TM
    pad_end = jnp.cumsum(padded)
    pad_start = pad_end - padded
    dest = pad_start[e_flat] + rank
    token = jnp.arange(N_ASSIGN, dtype=jnp.int32) // 2
    src_row = jnp.zeros((MOE_ROWS,), jnp.int32).at[dest].set(token)
    block_start = jnp.arange(MOE_ROWS // MOE_TM, dtype=jnp.int32) * MOE_TM
    block_expert = jnp.minimum(jnp.searchsorted(pad_end, block_start, side="right"), N_EXPERTS - 1).astype(jnp.int32)
    slots = jnp.concatenate([dest[0::2], dest[1::2]])
    return slots, src_row, block_expert


def _moe(h, route, w_gate, w_up, w_down, *, name):
    slots, src_row, block_expert = _moe_plan(route)
    xs = _gather_rows(h, src_row, name=name + "_dispatch")
    act = _gmm_gated(xs, w_gate, w_up, block_expert, tm=MOE_TM, tn=512, name=name + "_up")
    ys = _gmm(act, w_down, block_expert, tm=MOE_TM, tn=256, out_dtype=F32, name=name + "_down")
    pad = (-slots.shape[0]) % GATHER_CHUNK
    slots = jnp.concatenate([slots, jnp.zeros((pad,), jnp.int32)])
    return _gather_rows(ys, slots, name=name + "_combine")[:2 * ROWS]


def _band_bias(rel_bias, heads, dil, max_dist):
    dist = _band_dist()
    valid = (dist >= 0) & (dist <= max_dist)
    tab = jnp.transpose(rel_bias[_rel_bucket(dist * dil)][:, :, heads], (2, 0, 1))
    return jnp.where(valid[None], tab, NEG).astype(F32)


def _dil_sample_bias(rel_bias, gi, win, dil):
    t = np.arange(T_S)
    heads = slice(gi * B_HG, (gi + 1) * B_HG)

    def table(key_idx, n_valid):
        dist = (win + t)[:, None] - key_idx[None, :]
        valid = (dist >= 0) & (dist % dil == 0) & (dist <= win) & (np.arange(len(key_idx)) < n_valid)[None, :]
        tab = rel_bias[_rel_bucket(dist)][:, :, heads]
        tab = jnp.where(valid[:, :, None], tab, NEG)
        return jnp.transpose(tab, (2, 0, 1)).reshape(B_HG * T_S, len(key_idx)).astype(F32)

    return table(np.arange(win), win), table(win + np.arange(BLK), T_S)


def _swa_sample_bias(rel_bias):
    t = np.arange(T_S)

    def table(key_idx, n_valid):
        dist = (C_WINDOW + t)[:, None] - key_idx[None, :]
        valid = (dist >= 0) & (dist < C_WINDOW) & (np.arange(len(key_idx)) < n_valid)[None, :]
        tab = jnp.where(valid[:, :, None], rel_bias[_rel_bucket(dist)], NEG)
        tab = jnp.transpose(tab, (2, 0, 1))
        return tab.reshape(C_KV_HEADS, C_GROUP * T_S, len(key_idx)).astype(F32)

    return table(np.arange(C_WINDOW), C_WINDOW), table(C_WINDOW + np.arange(BLK), T_S)


def _to_residues(a, dil):
    n, t, w = a.shape
    return a.reshape(n, t // dil, dil, w).swapaxes(1, 2).reshape(n * dil, t // dil, w)


def _from_residues(a, dil, n):
    _, sub, w = a.shape
    return a.reshape(n, dil, sub, w).swapaxes(1, 2).reshape(n * sub * dil, w)


def _mixer_conv(h, cache, w_in, conv_w, w_out):
    bcz = _dense(h, w_in, tn=512, out_dtype=F32, name="a_in")
    yp, st_p = _short_conv(bcz, jnp.zeros((N_P, 2, D), F32), conv_w, row0=0, n_seq=N_P, seq_len=T_P,
                           rows=ROW_BLK, name="a_conv_prompt")
    ys, st_s = _short_conv(bcz, cache, conv_w, row0=ROWS_P, n_seq=N_S, seq_len=T_S, rows=T_S,
                           name="a_conv_sample")
    out = _dense(jnp.concatenate([yp, ys], axis=0), w_out, tn=512, out_dtype=F32, name="a_out")
    return out, (st_p, st_s)


def _mixer_dilated(h, caches, w_qkv, w_o, rel_bias):
    qkv = _dense(h, w_qkv, tn=512, out_dtype=F32, name="b_qkv")
    qp = qkv[:ROWS_P].reshape(N_P, T_P, 3, 3, GROUP_W)
    qs = qkv[ROWS_P:].reshape(N_S, T_S, 3, 3, GROUP_W)
    outs, lses, rows_p, rows_s = [], [], [], []
    for gi, (win, dil) in enumerate(DIL_CFG):
        q, k, v = (qp[:, :, s, gi] for s in range(3))
        bias = _band_bias(rel_bias, slice(gi * B_HG, (gi + 1) * B_HG), dil, win // dil)
        o, lse = _banded_attention(_to_residues(q, dil), _to_residues(k, dil), _to_residues(v, dil), bias,
                                   group=1, with_lse=True, name=f"b_attn_prompt{gi}")
        keep = min(win, T_P)
        rows_p.append(jnp.stack([k[:, T_P - keep:], v[:, T_P - keep:]], axis=2)
                      .reshape(1, N_P, keep, 2, B_HG, HEAD_DIM))
        q, k, v = (qs[:, :, s, gi] for s in range(3))
        bias_c, bias_n = _dil_sample_bias(rel_bias, gi, win, dil)
        cache = caches[gi].reshape(N_S, win, 2 * GROUP_W)
        o_s, lse_s = _dil_sample_attention(q, k, v, cache, bias_c, bias_n, name=f"b_attn_sample{gi}")
        rows_s.append(jnp.stack([k, v], axis=2).reshape(1, N_S, T_S, 2, B_HG, HEAD_DIM))
        outs.append(jnp.concatenate([_from_residues(o, dil, N_P), o_s.reshape(ROWS_S, GROUP_W)], axis=0))
        lses.append(jnp.concatenate([_from_residues(lse, dil, N_P), lse_s.reshape(ROWS_S, GROUP_W)], axis=0))
    merged = _merge_dilations(outs, lses, name="b_merge")
    out = _dense(merged, w_o, tn=512, out_dtype=F32, name="b_out")
    return out, (rows_p, rows_s)


def _mixer_swa(h, cache, w_qkv, sink, w_o, rel_bias):
    qkv = _dense(h, w_qkv, tn=512, out_dtype=F32, name="c_qkv")
    kw = C_KV_HEADS * HEAD_DIM
    qp = qkv[:ROWS_P].reshape(N_P, T_P, ATTN_W + 2 * kw)
    bias = _band_bias(rel_bias, slice(0, N_HEADS), 1, C_WINDOW - 1)
    o_p = _banded_attention(qp[..., :ATTN_W], qp[..., ATTN_W:ATTN_W + kw], qp[..., ATTN_W + kw:], bias,
                            sink.astype(F32), group=C_GROUP, with_lse=False, name="c_attn_prompt")
    kv_p = qp[:, T_P - C_WINDOW:, ATTN_W:].reshape(1, N_P, C_WINDOW, 2, C_KV_HEADS, HEAD_DIM)
    qs = qkv[ROWS_P:].reshape(N_S, T_S, ATTN_W + 2 * kw)
    bias_c, bias_n = _swa_sample_bias(rel_bias)
    sink_tab = jnp.repeat(sink.astype(F32).reshape(C_KV_HEADS, C_GROUP), T_S, axis=1)[:, :, None]
    o_s = _swa_sample_attention(qs[..., :ATTN_W], qs[..., ATTN_W:], cache.reshape(N_S, C_WINDOW, 2 * kw),
                                bias_c, bias_n, sink_tab, name="c_attn_sample")
    kv_s = qs[..., ATTN_W:].reshape(1, N_S, T_S, 2, C_KV_HEADS, HEAD_DIM)
    o = jnp.concatenate([o_p.reshape(ROWS_P, ATTN_W), o_s.reshape(ROWS_S, ATTN_W)], axis=0).astype(BF16)
    out = _dense(o, w_o, tn=512, out_dtype=F32, name="c_out")
    return out, (kv_p, kv_s)


def _mixer_sgu(h, w_uv, w_s, b_s, w_out):
    uv = _dense(h, w_uv, tn=512, out_dtype=F32, act="gelu", name="d_uv")
    yp, _ = _spatial_gate(uv, w_s, b_s, row0=0, n_rows=ROWS_P, rows=BLK, name="d_gate_prompt")
    ys, vn_s = _spatial_gate(uv, w_s, b_s, row0=ROWS_P, n_rows=ROWS_S, rows=T_S, name="d_gate_sample")
    out = _dense(jnp.concatenate([yp, ys], axis=0), w_out, tn=512, out_dtype=F32, name="d_out")
    return out, vn_s.reshape(1, N_S, T_S, D)


def _modulations(c_prompt, c_sample, ada_w, ada_b):
    n_sub = 2 * DEPTH
    pad_rows = 48
    cond = jax.nn.silu(jnp.concatenate([c_prompt, c_sample], axis=0))
    cond = jnp.concatenate([cond, jnp.zeros((pad_rows - N_P - N_S, D), F32)], axis=0).astype(BF16)
    x = jnp.tile(cond, (n_sub, 1))
    mod = _gmm(x, ada_w.reshape(n_sub, D, 3 * D), jnp.arange(n_sub, dtype=jnp.int32),
               tm=pad_rows, tn=512, out_dtype=F32, name="ada")
    mod = mod.reshape(n_sub, pad_rows, 3 * D) + ada_b.reshape(n_sub, 1, 3 * D)
    res = []
    for s in range(n_sub):
        modp = mod[s, :N_P].reshape(N_P, 1, 3 * D)
        mods = jnp.repeat(mod[s, N_P:N_P + N_S], T_S, axis=0)
        res.append((modp, mods))
    return res


def kernel(x_prompt, x_sample, c_prompt, c_sample, cache_a_conv, cache_b_kv0, cache_b_kv1, cache_b_kv2, cache_c_kv, norm_g, ada_w, ada_b, rel_bias, a_w_in, a_conv_w, a_w_out, b_w_qkv, b_w_o, c_w_qkv, c_sink, c_w_o, d_w_uv, d_w_s, d_b_s, d_w_out, f_w_gate, f_w_up, f_w_down, m_w_router, m_b_router, m_w_gate, m_w_up, m_w_down):
    mods = _modulations(c_prompt, c_sample, ada_w, ada_b)
    x = jnp.concatenate([x_prompt.reshape(ROWS_P, D), x_sample.reshape(ROWS_S, D)], axis=0)
    state = {}
    for i in range(DEPTH):
        kind = i % 4
        modp, modsr = mods[2 * i]
        h = _norm_mod(x, norm_g[i, 0], modp, modsr, name=f"norm_mix{i}")
        if kind == 0:
            out, state["a"] = _mixer_conv(h, cache_a_conv[0], a_w_in[0], a_conv_w[0], a_w_out[0])
        elif kind == 1:
            out, state["b"] = _mixer_dilated(h, (cache_b_kv0[0], cache_b_kv1[0], cache_b_kv2[0]),
                                             b_w_qkv[0], b_w_o[0], rel_bias)
        elif kind == 2:
            out, state["c"] = _mixer_swa(h, cache_c_kv[0], c_w_qkv[0], c_sink[0], c_w_o[0], rel_bias)
        else:
            out, state["d"] = _mixer_sgu(h, d_w_uv[0], d_w_s[0], d_b_s[0], d_w_out[0])
        x = _resid(x, out, norm_g[i, 1], modp, modsr, name=f"resid_mix{i}")

        modp, modsr = mods[2 * i + 1]
        fi = i // 2
        if i % 2 == 0:
            h = _norm_mod(x, norm_g[i, 2], modp, modsr, name=f"norm_ffn{i}")
            act = _gmm_gated(h, f_w_gate[fi][None], f_w_up[fi][None], jnp.zeros((ROWS // ROW_BLK,), jnp.int32),
                             tm=ROW_BLK, tn=512, name=f"ffn_up{i}")
            out = _dense(act, f_w_down[fi], tn=256, out_dtype=F32, name=f"ffn_down{i}")
            x = _resid(x, out, norm_g[i, 3], modp, modsr, name=f"resid_ffn{i}")
        else:
            wr = jnp.zeros((D, LANE), F32).at[:, :N_EXPERTS].set(m_w_router[fi])
            br = jnp.zeros((1, LANE), F32).at[0, :N_EXPERTS].set(m_b_router[fi])
            h, route = _norm_mod(x, norm_g[i, 2], modp, modsr, (wr, br), name=f"norm_moe{i}")
            y2 = _moe(h, route, m_w_gate[fi], m_w_up[fi], m_w_down[fi], name=f"moe{i}")
            x = _resid(x, y2, norm_g[i, 3], modp, modsr, route, name=f"resid_moe{i}")

    y_prompt = x[:ROWS_P].reshape(N_P, T_P, D)
    y_sample = x[ROWS_P:].reshape(N_S, T_S, D)
    (a_p, a_s) = state["a"]
    rows_p, rows_s = state["b"]
    c_p, c_s = state["c"]
    return (y_prompt, y_sample, a_p[None], a_s[None],
            rows_p[0], rows_s[0], rows_p[1], rows_s[1], rows_p[2], rows_s[2],
            c_p, c_s, state["d"])
```

```python
import functools
import math

import numpy as np
import jax
import jax.numpy as jnp
from jax import lax
from jax.experimental import pallas as pl
from jax.experimental.pallas import tpu as pltpu

D = 2048
N_P, T_P = 4, 2048
N_S, T_S = 32, 8
ROWS_P = N_P * T_P
ROWS_S = N_S * T_S
ROWS = ROWS_P + ROWS_S
DEPTH = 4
HEAD_DIM = 64
N_HEADS = 24
ATTN_W = N_HEADS * HEAD_DIM
DIL_CFG = ((128, 1), (512, 4), (2048, 16))
N_DIL = len(DIL_CFG)
B_HG = 8
GROUP_W = B_HG * HEAD_DIM
C_WINDOW = 128
C_KV_HEADS = 4
C_GROUP = 6
C_KW = C_KV_HEADS * HEAD_DIM
BLK = 128
SG_GROUPS = 8
SG_GW = D // SG_GROUPS
REL_BUCKETS = 32
REL_MAX_DIST = 2048
N_EXPERTS = 8
RMS_EPS = 1e-6
LN_EPS = 1e-5
NEG = -1e30

LANE = 128
VMEM_LIMIT_CAP = 56 * 1024 * 1024
VMEM_LIMIT_FLOOR = 32 * 1024 * 1024
VMEM_MARGIN = 8 * 1024 * 1024

ROW_BLK = 256
N_ROW_BLKS = ROWS // ROW_BLK
N_PROMPT_BLKS = ROWS_P // ROW_BLK
BLKS_PER_SEQ = T_P // ROW_BLK
SAMPLE_BLK8 = ROWS_P // T_S
DENSE_TM = 768
MOE_TM = 256
GATHER_CHUNK = 512
GATHER_UNROLL = 8

BF16 = jnp.bfloat16
F32 = jnp.float32


def _params(sem, nbytes):
    limit = int(min(max(nbytes + VMEM_MARGIN, VMEM_LIMIT_FLOOR), VMEM_LIMIT_CAP))
    return pltpu.CompilerParams(dimension_semantics=sem, vmem_limit_bytes=limit)


def _weight_changed(be_ref):
    b = pl.program_id(1)
    prev = be_ref[jnp.maximum(b - 1, 0)]
    return jnp.logical_or(b == 0, be_ref[b] != prev)


def _gmm_kernel(be_ref, x_ref, w_ref, o_ref, wc_ref, *, act):
    @pl.when(_weight_changed(be_ref))
    def _():
        wc_ref[...] = w_ref[...].astype(BF16)

    acc = jnp.dot(x_ref[...].astype(BF16), wc_ref[...], preferred_element_type=F32)
    if act == "gelu":
        acc = jax.nn.gelu(acc)
    o_ref[...] = acc.astype(o_ref.dtype)


def _gmm_gated_kernel(be_ref, x_ref, wg_ref, wu_ref, o_ref, wgc_ref, wuc_ref):
    @pl.when(_weight_changed(be_ref))
    def _():
        wgc_ref[...] = wg_ref[...].astype(BF16)
        wuc_ref[...] = wu_ref[...].astype(BF16)

    x = x_ref[...].astype(BF16)
    g = jnp.dot(x, wgc_ref[...], preferred_element_type=F32)
    u = jnp.dot(x, wuc_ref[...], preferred_element_type=F32)
    o_ref[...] = (jax.nn.silu(g) * u).astype(o_ref.dtype)


def _gmm(x, w, block_expert, *, tm, tn, out_dtype, act=None, name):
    m, k = x.shape
    e, k2, n = w.shape
    assert k == k2 and m % tm == 0 and n % tn == 0
    nbytes = (2 * tm * k * x.dtype.itemsize + 2 * k * tn * 4 + k * tn * 2
              + 2 * tm * tn * jnp.dtype(out_dtype).itemsize)
    return pl.pallas_call(
        functools.partial(_gmm_kernel, act=act),
        out_shape=jax.ShapeDtypeStruct((m, n), out_dtype),
        grid_spec=pltpu.PrefetchScalarGridSpec(
            num_scalar_prefetch=1,
            grid=(n // tn, m // tm),
            in_specs=[
                pl.BlockSpec((tm, k), lambda j, b, be: (b, 0)),
                pl.BlockSpec((None, k, tn), lambda j, b, be: (be[b], 0, j)),
            ],
            out_specs=pl.BlockSpec((tm, tn), lambda j, b, be: (b, j)),
            scratch_shapes=[pltpu.VMEM((k, tn), BF16)],
        ),
        compiler_params=_params(("arbitrary", "arbitrary"), nbytes),
        name=name,
    )(block_expert, x, w)


def _gmm_gated(x, wg, wu, block_expert, *, tm, tn, name):
    m, k = x.shape
    e, k2, n = wg.shape
    assert k == k2 and m % tm == 0 and n % tn == 0 and wu.shape == wg.shape
    nbytes = 2 * tm * k * x.dtype.itemsize + 4 * k * tn * 4 + 2 * k * tn * 2 + 2 * tm * tn * 2
    return pl.pallas_call(
        _gmm_gated_kernel,
        out_shape=jax.ShapeDtypeStruct((m, n), BF16),
        grid_spec=pltpu.PrefetchScalarGridSpec(
            num_scalar_prefetch=1,
            grid=(n // tn, m // tm),
            in_specs=[
                pl.BlockSpec((tm, k), lambda j, b, be: (b, 0)),
                pl.BlockSpec((None, k, tn), lambda j, b, be: (be[b], 0, j)),
                pl.BlockSpec((None, k, tn), lambda j, b, be: (be[b], 0, j)),
            ],
            out_specs=pl.BlockSpec((tm, tn), lambda j, b, be: (b, j)),
            scratch_shapes=[pltpu.VMEM((k, tn), BF16), pltpu.VMEM((k, tn), BF16)],
        ),
        compiler_params=_params(("arbitrary", "arbitrary"), nbytes),
        name=name,
    )(block_expert, x, wg, wu)


def _single(m, tm, which=0):
    return jnp.full((m // tm,), which, jnp.int32)


def _dense(x, w, *, tn, out_dtype, act=None, name, tm=DENSE_TM, which=0):
    return _gmm(x, w, _single(x.shape[0], tm, which), tm=tm, tn=tn, out_dtype=out_dtype, act=act, name=name)


def _rms(x, g):
    return x * lax.rsqrt(jnp.mean(x * x, axis=-1, keepdims=True) + RMS_EPS) * g


def _route(h, wr_ref, br_ref):
    logits = jnp.dot(h, wr_ref[...], preferred_element_type=F32, precision=lax.Precision.HIGHEST) + br_ref[...]
    lane = lax.broadcasted_iota(jnp.int32, logits.shape, 1)
    logits = jnp.where(lane < N_EXPERTS, logits, -jnp.inf)
    m1 = jnp.max(logits, axis=-1, keepdims=True)
    i1 = jnp.min(jnp.where(logits == m1, lane, LANE), axis=-1, keepdims=True)
    rest = jnp.where(lane == i1, -jnp.inf, logits)
    m2 = jnp.max(rest, axis=-1, keepdims=True)
    i2 = jnp.min(jnp.where(rest == m2, lane, LANE), axis=-1, keepdims=True)
    e2 = jnp.exp(m2 - m1)
    g1 = 1.0 / (1.0 + e2)
    g2 = e2 / (1.0 + e2)
    out = jnp.where(lane == 0, i1.astype(F32),
                    jnp.where(lane == 1, i2.astype(F32),
                              jnp.where(lane == 2, g1, jnp.where(lane == 3, g2, 0.0))))
    return out


def _norm_mod_kernel(x_ref, g_ref, shp_ref, scp_ref, shs_ref, scs_ref, *rest, route):
    if route:
        wr_ref, br_ref, h_ref, r_ref = rest
    else:
        (h_ref,) = rest
    i = pl.program_id(0)
    y = _rms(x_ref[...], g_ref[...])

    def emit(shift, scale):
        h = y * (1.0 + scale) + shift
        h_ref[...] = h.astype(h_ref.dtype)
        if route:
            r_ref[...] = _route(h, wr_ref, br_ref)

    @pl.when(i < N_PROMPT_BLKS)
    def _():
        emit(shp_ref[...], scp_ref[...])

    @pl.when(i == N_PROMPT_BLKS)
    def _():
        emit(shs_ref[...], scs_ref[...])


def _mod_specs(cols):
    specs = []
    for c in cols:
        specs.append(pl.BlockSpec((None, 1, D), lambda i, c=c: (jnp.minimum(i // BLKS_PER_SEQ, N_P - 1), 0, c)))
    for c in cols:
        specs.append(pl.BlockSpec((ROW_BLK, D), lambda i, c=c: (0, c)))
    return specs


def _norm_mod(x, g, modp, mods, router=None, *, name):
    route = router is not None
    row_spec = pl.BlockSpec((ROW_BLK, D), lambda i: (i, 0))
    in_specs = [row_spec, pl.BlockSpec((1, D), lambda i: (0, 0))] + _mod_specs((0, 1))
    args = [x, g.reshape(1, D), modp, modp, mods, mods]
    out_shape = [jax.ShapeDtypeStruct((ROWS, D), BF16)]
    out_specs = [row_spec]
    if route:
        wr, br = router
        in_specs += [pl.BlockSpec((D, LANE), lambda i: (0, 0)), pl.BlockSpec((1, LANE), lambda i: (0, 0))]
        args += [wr, br]
        out_shape.append(jax.ShapeDtypeStruct((ROWS, LANE), F32))
        out_specs.append(pl.BlockSpec((ROW_BLK, LANE), lambda i: (i, 0)))
    nbytes = ROW_BLK * D * (4 + 2 + 8 + 8) * 2 + D * LANE * 4 * 2
    res = pl.pallas_call(
        functools.partial(_norm_mod_kernel, route=route),
        out_shape=out_shape,
        grid=(N_ROW_BLKS,),
        in_specs=in_specs,
        out_specs=out_specs,
        compiler_params=_params(("parallel",), nbytes),
        name=name,
    )(*args)
    return res if route else res[0]


def _resid_kernel(x_ref, *rest, moe):
    if moe:
        ya_ref, yb_ref, r_ref, g_ref, gp_ref, gs_ref, o_ref = rest
        r = r_ref[...]
        out = r[:, 2:3] * ya_ref[...] + r[:, 3:4] * yb_ref[...]
    else:
        y_ref, g_ref, gp_ref, gs_ref, o_ref = rest
        out = y_ref[...]
    i = pl.program_id(0)
    upd = _rms(out, g_ref[...])

    @pl.when(i < N_PROMPT_BLKS)
    def _():
        o_ref[...] = x_ref[...] + gp_ref[...] * upd

    @pl.when(i == N_PROMPT_BLKS)
    def _():
        o_ref[...] = x_ref[...] + gs_ref[...] * upd


def _resid(x, y, g, modp, mods, route=None, *, name):
    moe = route is not None
    row_spec = pl.BlockSpec((ROW_BLK, D), lambda i: (i, 0))
    in_specs = [row_spec]
    args = [x]
    if moe:
        in_specs += [row_spec, pl.BlockSpec((ROW_BLK, D), lambda i: (i + N_ROW_BLKS, 0)),
                     pl.BlockSpec((ROW_BLK, LANE), lambda i: (i, 0))]
        args += [y, y, route]
    else:
        in_specs += [row_spec]
        args += [y]
    in_specs += [pl.BlockSpec((1, D), lambda i: (0, 0))] + _mod_specs((2,))
    args += [g.reshape(1, D), modp, mods]
    nbytes = ROW_BLK * D * 4 * 6 * 2
    return pl.pallas_call(
        functools.partial(_resid_kernel, moe=moe),
        out_shape=jax.ShapeDtypeStruct((ROWS, D), F32),
        grid=(N_ROW_BLKS,),
        in_specs=in_specs,
        out_specs=row_spec,
        compiler_params=_params(("parallel",), nbytes),
        name=name,
    )(*args)


def _conv_kernel(gb_ref, gc_ref, z_ref, cw_ref, b1_ref, b2_ref, y_ref, stp_ref, us_ref, tail_ref):
    i = pl.program_id(0)
    u = gc_ref[...] * z_ref[...]
    r = lax.broadcasted_iota(jnp.int32, u.shape, 0)
    roll1 = pltpu.roll(u, 1, axis=0)
    roll2 = pltpu.roll(u, 2, axis=0)

    def emit(um1, um2):
        conv = cw_ref[0:1, :] * um2 + cw_ref[1:2, :] * um1 + cw_ref[2:3, :] * u
        y_ref[...] = (gb_ref[...] * conv).astype(y_ref.dtype)

    @pl.when(i < N_PROMPT_BLKS)
    def _():
        @pl.when(i % BLKS_PER_SEQ == 0)
        def _():
            tail_ref[...] = jnp.zeros_like(tail_ref)

        prev2 = tail_ref[6:7, :]
        prev1 = tail_ref[7:8, :]
        emit(jnp.where(r == 0, prev1, roll1),
             jnp.where(r == 0, prev2, jnp.where(r == 1, prev1, roll2)))
        tail_ref[...] = u[ROW_BLK - 8:ROW_BLK, :]
        stp_ref[...] = u[ROW_BLK - 2:ROW_BLK, :]

    @pl.when(i == N_PROMPT_BLKS)
    def _():
        t = r % T_S
        emit(jnp.where(t == 0, b1_ref[...], roll1), jnp.where(t < 2, b2_ref[...], roll2))
        us_ref[...] = u


def _short_conv(bcz, cache, conv_w, *, name):
    zeros = jnp.zeros((N_S, T_S, D), F32)
    b1 = zeros.at[:, 0].set(cache[:, 1]).reshape(ROWS_S, D)
    b2 = zeros.at[:, 0].set(cache[:, 0]).at[:, 1].set(cache[:, 1]).reshape(ROWS_S, D)

    def col(c):
        return pl.BlockSpec((ROW_BLK, D), lambda i, c=c: (i, c))

    const = pl.BlockSpec((ROW_BLK, D), lambda i: (0, 0))
    y, st_p, u_s = pl.pallas_call(
        _conv_kernel,
        out_shape=[jax.ShapeDtypeStruct((ROWS, D), BF16), jax.ShapeDtypeStruct((N_P, 2, D), F32),
                   jax.ShapeDtypeStruct((ROWS_S, D), F32)],
        grid=(N_ROW_BLKS,),
        in_specs=[col(0), col(1), col(2), pl.BlockSpec((3, D), lambda i: (0, 0)), const, const],
        out_specs=[pl.BlockSpec((ROW_BLK, D), lambda i: (i, 0)),
                   pl.BlockSpec((None, 2, D), lambda i: (jnp.minimum(i // BLKS_PER_SEQ, N_P - 1), 0, 0)),
                   const],
        scratch_shapes=[pltpu.VMEM((8, D), F32)],
        compiler_params=_params(("arbitrary",), ROW_BLK * D * 4 * 7 * 2),
        name=name,
    )(bcz, bcz, bcz, conv_w, b1, b2)
    return y, st_p, u_s.reshape(N_S, T_S, D)[:, T_S - 2:]


def _rel_bucket(dist):
    n = np.maximum(np.asarray(dist), 0)
    exact = REL_BUCKETS // 2
    ratio = np.log(np.maximum(n, 1).astype(np.float32) / exact) / math.log(REL_MAX_DIST / exact)
    large = np.minimum(exact + (ratio * (REL_BUCKETS - exact)).astype(np.int32), REL_BUCKETS - 1)
    return np.where(n < exact, n, large).astype(np.int32)


def _toeplitz(a, n_rows, n_cols):
    h, p = a.shape
    assert n_cols < p
    t = jnp.tile(a, (1, n_rows))[:, :n_rows * (p - 1)].reshape(h, n_rows, p - 1)
    return t[:, :, :n_cols]


def _bias_line(rel_bias, heads, dist, valid):
    tab = rel_bias[_rel_bucket(dist)][:, heads]
    return jnp.where(valid[:, None], tab, NEG).T.astype(F32)


def _band_bias(rel_bias, heads, dil, max_dist):
    p = 3 * BLK
    m = np.arange(p)
    dist = np.where(m < 2 * BLK, BLK - m, BLK + p - m)
    valid = (dist >= 0) & (dist <= max_dist)
    return _toeplitz(_bias_line(rel_bias, heads, dist * dil, valid), BLK, 2 * BLK)


def _sample_bias(rel_bias, heads, win, dil, max_dist):
    lk = win + BLK
    p = lk + T_S
    m = np.arange(p)
    dist = np.where(m < lk, win - m, win + p - m)
    valid = (dist >= 0) & (dist % dil == 0) & (dist <= max_dist)
    return _toeplitz(_bias_line(rel_bias, heads, dist, valid), T_S, lk)


def _softmax_pv(s, v, sink=None):
    m = jnp.max(s, axis=-1, keepdims=True)
    if sink is not None:
        m = jnp.maximum(m, sink)
    p = jnp.exp(s - m)
    l = jnp.sum(p, axis=-1, keepdims=True)
    if sink is not None:
        l = l + jnp.exp(sink - m)
    o = jnp.dot(p.astype(BF16), v, preferred_element_type=F32) / l
    return o, m + jnp.log(l)


def _nt_dot(a, b):
    return lax.dot_general(a, b, (((1,), (1,)), ((), ())), preferred_element_type=F32)


def _banded_kernel(*refs, n_seq, **kw):
    out_refs = refs[-2:] if kw["with_lse"] else refs[-1:]

    @pl.when(pl.program_id(0) < n_seq)
    def _():
        _banded_body(*refs, **kw)

    @pl.when(pl.program_id(0) >= n_seq)
    def _():
        for ref in out_refs:
            ref[...] = jnp.zeros_like(ref)


def _banded_body(*refs, n_heads, group, with_lse, has_sink):
    refs = list(refs)
    sink_ref = refs.pop(0) if has_sink else None
    q_ref, kp_ref, kc_ref, vp_ref, vc_ref, bias_ref = refs[:6]
    o_ref = refs[6]
    lse_ref = refs[7] if with_lse else None
    qb = pl.program_id(1)
    q = (q_ref[...] * (HEAD_DIM ** -0.5)).astype(BF16)
    k = jnp.concatenate([kp_ref[...], kc_ref[...]], axis=0).astype(BF16)
    v = jnp.concatenate([vp_ref[...], vc_ref[...]], axis=0).astype(BF16)
    col = lax.broadcasted_iota(jnp.int32, (BLK, 2 * BLK), 1)
    no_prev = jnp.logical_and(qb == 0, col < BLK)
    for h in range(n_heads):
        hk = h // group
        qs = slice(h * HEAD_DIM, (h + 1) * HEAD_DIM)
        ks = slice(hk * HEAD_DIM, (hk + 1) * HEAD_DIM)
        s = _nt_dot(q[:, qs], k[:, ks]) + bias_ref[h]
        s = jnp.where(no_prev, NEG, s)
        sink = sink_ref[h] if has_sink else None
        o, lse = _softmax_pv(s, v[:, ks], sink)
        o_ref[:, qs] = o.astype(o_ref.dtype)
        if with_lse:
            lse_ref[:, qs] = jnp.broadcast_to(lse, (BLK, HEAD_DIM))


def _banded_attention(src, bias, sink=None, *, dil, n_seq, n_blk, row_blk, q_col, k_col, v_col, qw, kw,
                      out_shape, out_col, group, with_lse, name):
    n_heads = qw // HEAD_DIM
    has_sink = sink is not None
    last = n_seq - 1
    cur = lambda col: (lambda s, b: (row_blk(jnp.minimum(s, last), b), col(jnp.minimum(s, last))))
    prev = lambda col: (lambda s, b: (row_blk(jnp.minimum(s, last), jnp.maximum(b - 1, 0)),
                                      col(jnp.minimum(s, last))))
    fill_row0 = ROWS_P // dil // BLK
    fill_blks = -(-(ROWS_S // dil) // BLK)
    assert n_blk >= fill_blks

    def out_map(s, b):
        fill = s >= n_seq
        return (jnp.where(fill, fill_row0 + jnp.minimum(b, fill_blks - 1), row_blk(jnp.minimum(s, last), b)),
                jnp.where(fill, s - n_seq, out_col(jnp.minimum(s, last))))

    in_specs = [pl.BlockSpec((BLK, qw), cur(q_col)),
                pl.BlockSpec((BLK, kw), prev(k_col)), pl.BlockSpec((BLK, kw), cur(k_col)),
                pl.BlockSpec((BLK, kw), prev(v_col)), pl.BlockSpec((BLK, kw), cur(v_col)),
                pl.BlockSpec((n_heads, BLK, 2 * BLK), lambda s, b: (0, 0, 0))]
    args = [src, src, src, src, src, bias]
    if has_sink:
        in_specs = [pl.BlockSpec(memory_space=pltpu.SMEM)] + in_specs
        args = [sink] + args
    out_shapes = [out_shape]
    out_specs = [pl.BlockSpec((BLK, qw), out_map)]
    if with_lse:
        out_shapes.append(jax.ShapeDtypeStruct(out_shape.shape, F32))
        out_specs.append(pl.BlockSpec((BLK, qw), out_map))
    nbytes = (BLK * qw * 4 * 3 + BLK * kw * 4 * 4) * 2 + n_heads * BLK * 2 * BLK * 4 * 2
    res = pl.pallas_call(
        functools.partial(_banded_kernel, n_seq=n_seq, n_heads=n_heads, group=group, with_lse=with_lse,
                          has_sink=has_sink),
        out_shape=out_shapes,
        grid=(n_seq + dil, n_blk),
        in_specs=in_specs,
        out_specs=out_specs,
        compiler_params=_params(("arbitrary", "arbitrary"), nbytes),
        name=name,
    )(*args)
    return res if with_lse else res[0]


def _dil_sample_kernel(q_ref, kn_ref, vn_ref, cache_ref, bc_ref, bn_ref, o_in, lse_in, o_ref, lse_ref):
    del o_in, lse_in
    q = q_ref[...] * (HEAD_DIM ** -0.5)
    qt = jnp.concatenate([q] * B_HG, axis=0)
    row_h = lax.broadcasted_iota(jnp.int32, qt.shape, 0) // T_S
    col_h = lax.broadcasted_iota(jnp.int32, qt.shape, 1) // HEAD_DIM
    qbd = jnp.where(row_h == col_h, qt, 0.0).astype(BF16)
    pad = jnp.zeros((BLK - T_S, GROUP_W), F32)
    kn = jnp.concatenate([kn_ref[...], pad], axis=0).astype(BF16)
    vn = jnp.concatenate([vn_ref[...], pad], axis=0).astype(BF16)
    kc = cache_ref[:, 0:GROUP_W].astype(BF16)
    vc = cache_ref[:, GROUP_W:2 * GROUP_W].astype(BF16)
    sc = _nt_dot(qbd, kc) + bc_ref[...]
    sn = _nt_dot(qbd, kn) + bn_ref[...]
    m = jnp.maximum(jnp.max(sc, axis=-1, keepdims=True), jnp.max(sn, axis=-1, keepdims=True))
    pc = jnp.exp(sc - m)
    pn = jnp.exp(sn - m)
    l = jnp.sum(pc, axis=-1, keepdims=True) + jnp.sum(pn, axis=-1, keepdims=True)
    big = (jnp.dot(pc.astype(BF16), vc, preferred_element_type=F32)
           + jnp.dot(pn.astype(BF16), vn, preferred_element_type=F32)) / l
    lse = m + jnp.log(l)
    for h in range(B_HG):
        rs = slice(h * T_S, (h + 1) * T_S)
        cs = slice(h * HEAD_DIM, (h + 1) * HEAD_DIM)
        o_ref[:, cs] = big[rs, cs]
        lse_ref[:, cs] = jnp.broadcast_to(lse[rs], (T_S, HEAD_DIM))


def _dil_sample_attention(qkv, gi, cache, bias_c, bias_n, o, lse, *, name):
    lb = cache.shape[1]
    row = lambda c: pl.BlockSpec((T_S, GROUP_W), lambda n, c=c: (SAMPLE_BLK8 + n, c))
    const = lambda n: (0, 0)
    anyspec = pl.BlockSpec(memory_space=pl.ANY)
    nbytes = lb * 2 * GROUP_W * 4 * 2 + 64 * lb * 4 * 6
    return pl.pallas_call(
        _dil_sample_kernel,
        out_shape=[jax.ShapeDtypeStruct((ROWS, GROUP_W), F32)] * 2,
        grid=(N_S,),
        in_specs=[row(gi), row(N_DIL + gi), row(2 * N_DIL + gi),
                  pl.BlockSpec((None, lb, 2 * GROUP_W), lambda n: (n, 0, 0)),
                  pl.BlockSpec((B_HG * T_S, lb), const), pl.BlockSpec((B_HG * T_S, BLK), const),
                  anyspec, anyspec],
        out_specs=[row(0)] * 2,
        input_output_aliases={6: 0, 7: 1},
        compiler_params=_params(("parallel",), nbytes),
        name=name,
    )(qkv, qkv, qkv, cache, bias_c, bias_n, o, lse)


def _merge_kernel(o0, o1, o2, l0, l1, l2, out_ref):
    ls = [l0[...], l1[...], l2[...]]
    m = jnp.maximum(jnp.maximum(ls[0], ls[1]), ls[2])
    es = [jnp.exp(l - m) for l in ls]
    tot = es[0] + es[1] + es[2]
    for g, o in enumerate((o0, o1, o2)):
        out_ref[:, g * GROUP_W:(g + 1) * GROUP_W] = (o[...] * (es[g] / tot)).astype(out_ref.dtype)


def _merge_dilations(outs, lses, *, name):
    spec = pl.BlockSpec((ROW_BLK, GROUP_W), lambda i: (i, 0))
    return pl.pallas_call(
        _merge_kernel,
        out_shape=jax.ShapeDtypeStruct((ROWS, ATTN_W), BF16),
        grid=(N_ROW_BLKS,),
        in_specs=[spec] * 6,
        out_specs=pl.BlockSpec((ROW_BLK, ATTN_W), lambda i: (i, 0)),
        compiler_params=_params(("parallel",), ROW_BLK * GROUP_W * 4 * 8 * 2),
        name=name,
    )(*outs, *lses)


def _swa_sample_kernel(q_ref, kvn_ref, cache_ref, bc_ref, bn_ref, sink_ref, o_in, o_ref):
    del o_in
    q = q_ref[...] * (HEAD_DIM ** -0.5)
    pad = jnp.zeros((BLK - T_S, 2 * C_KW), F32)
    kvn = jnp.concatenate([kvn_ref[...], pad], axis=0).astype(BF16)
    kvc = cache_ref[...].astype(BF16)
    for hk in range(C_KV_HEADS):
        heads = range(hk * C_GROUP, (hk + 1) * C_GROUP)
        qh = jnp.concatenate([q[:, h * HEAD_DIM:(h + 1) * HEAD_DIM] for h in heads], axis=0).astype(BF16)
        ks = slice(hk * HEAD_DIM, (hk + 1) * HEAD_DIM)
        vs = slice(C_KW + hk * HEAD_DIM, C_KW + (hk + 1) * HEAD_DIM)
        sc = _nt_dot(qh, kvc[:, ks]) + bc_ref[hk]
        sn = _nt_dot(qh, kvn[:, ks]) + bn_ref[hk]
        sink = sink_ref[hk]
        m = jnp.maximum(jnp.maximum(jnp.max(sc, axis=-1, keepdims=True),
                                    jnp.max(sn, axis=-1, keepdims=True)), sink)
        pc = jnp.exp(sc - m)
        pn = jnp.exp(sn - m)
        l = jnp.sum(pc, axis=-1, keepdims=True) + jnp.sum(pn, axis=-1, keepdims=True) + jnp.exp(sink - m)
        o = (jnp.dot(pc.astype(BF16), kvc[:, vs], preferred_element_type=F32)
             + jnp.dot(pn.astype(BF16), kvn[:, vs], preferred_element_type=F32)) / l
        for gi, h in enumerate(heads):
            o_ref[:, h * HEAD_DIM:(h + 1) * HEAD_DIM] = o[gi * T_S:(gi + 1) * T_S].astype(o_ref.dtype)


def _swa_sample_attention(qkv, cache, bias_c, bias_n, sink_tab, o, *, name):
    rows = C_GROUP * T_S
    const = lambda n: (0, 0, 0)
    return pl.pallas_call(
        _swa_sample_kernel,
        out_shape=jax.ShapeDtypeStruct((ROWS, ATTN_W), o.dtype),
        grid=(N_S,),
        in_specs=[pl.BlockSpec((T_S, ATTN_W), lambda n: (SAMPLE_BLK8 + n, 0)),
                  pl.BlockSpec((T_S, 2 * C_KW), lambda n: (SAMPLE_BLK8 + n, ATTN_W // (2 * C_KW))),
                  pl.BlockSpec((None, C_WINDOW, 2 * C_KW), lambda n: (n, 0, 0)),
                  pl.BlockSpec((C_KV_HEADS, rows, BLK), const), pl.BlockSpec((C_KV_HEADS, rows, BLK), const),
                  pl.BlockSpec((C_KV_HEADS, rows, 1), const),
                  pl.BlockSpec(memory_space=pl.ANY)],
        out_specs=pl.BlockSpec((T_S, ATTN_W), lambda n: (SAMPLE_BLK8 + n, 0)),
        input_output_aliases={6: 0},
        compiler_params=_params(("parallel",), 1 << 20),
        name=name,
    )(qkv, qkv, cache, bias_c, bias_n, sink_tab, o)


def _sgu_kernel(u_ref, v_ref, w_ref, b_ref, ws_ref, bs_ref, y_ref, vn_ref):
    i = pl.program_id(0)
    v = v_ref[...]
    mu = jnp.mean(v, axis=-1, keepdims=True)
    var = jnp.mean(jnp.square(v - mu), axis=-1, keepdims=True)
    vn = (v - mu) * lax.rsqrt(var + LN_EPS)
    vb = vn.astype(BF16)

    @pl.when(i < N_PROMPT_BLKS)
    def _():
        causal = (lax.broadcasted_iota(jnp.int32, (BLK, BLK), 0) >= lax.broadcasted_iota(jnp.int32, (BLK, BLK), 1))
        for g in range(SG_GROUPS):
            cs = slice(g * SG_GW, (g + 1) * SG_GW)
            wg = jnp.where(causal, w_ref[g], 0.0).astype(BF16)
            for c in range(ROW_BLK // BLK):
                rs = slice(c * BLK, (c + 1) * BLK)
                s = jnp.dot(wg, vb[rs, cs], preferred_element_type=F32) + b_ref[g]
                y_ref[rs, cs] = (u_ref[rs, cs] * s).astype(y_ref.dtype)

    @pl.when(i == N_PROMPT_BLKS)
    def _():
        vn_ref[...] = vn
        for g in range(SG_GROUPS):
            cs = slice(g * SG_GW, (g + 1) * SG_GW)
            s = jnp.dot(ws_ref[g].astype(BF16), vb[:, cs], preferred_element_type=F32) + bs_ref[g]
            y_ref[:, cs] = (u_ref[:, cs] * s).astype(y_ref.dtype)


def _spatial_gate(uv, w_s, b_s, *, name):
    r = np.arange(ROWS_S)
    same_seq = (r[:, None] // T_S) == (r[None, :] // T_S)
    causal = (r[:, None] % T_S) >= (r[None, :] % T_S)
    w8 = jnp.tile(w_s[:, :T_S, :T_S], (1, N_S, N_S))
    ws = jnp.where((same_seq & causal)[None], w8, 0.0)
    bs = jnp.tile(b_s[:, :T_S], (1, N_S)).reshape(SG_GROUPS, ROWS_S, 1)
    const3 = lambda i: (0, 0, 0)
    nbytes = ROW_BLK * D * 4 * 4 * 2 + SG_GROUPS * (BLK * BLK + ROW_BLK * ROW_BLK) * 4 * 2
    return pl.pallas_call(
        _sgu_kernel,
        out_shape=[jax.ShapeDtypeStruct((ROWS, D), BF16), jax.ShapeDtypeStruct((ROWS_S, D), F32)],
        grid=(N_ROW_BLKS,),
        in_specs=[pl.BlockSpec((ROW_BLK, D), lambda i: (i, 0)), pl.BlockSpec((ROW_BLK, D), lambda i: (i, 1)),
                  pl.BlockSpec((SG_GROUPS, BLK, BLK), const3), pl.BlockSpec((SG_GROUPS, BLK, 1), const3),
                  pl.BlockSpec((SG_GROUPS, ROWS_S, ROWS_S), const3), pl.BlockSpec((SG_GROUPS, ROWS_S, 1), const3)],
        out_specs=[pl.BlockSpec((ROW_BLK, D), lambda i: (i, 0)), pl.BlockSpec((ROWS_S, D), lambda i: (0, 0))],
        compiler_params=_params(("arbitrary",), nbytes),
        name=name,
    )(uv, uv, w_s, b_s.reshape(SG_GROUPS, BLK, 1), ws, bs)


def _gather_kernel(idx_ref, src_ref, o_ref, sem):
    base = pl.program_id(0) * GATHER_CHUNK

    def row_copy(r):
        return pltpu.make_async_copy(src_ref.at[idx_ref[base + r]], o_ref.at[r], sem)

    def start(g, carry):
        for u in range(GATHER_UNROLL):
            row_copy(g * GATHER_UNROLL + u).start()
        return carry

    def wait(g, carry):
        for u in range(GATHER_UNROLL):
            row_copy(g * GATHER_UNROLL + u).wait()
        return carry

    lax.fori_loop(0, GATHER_CHUNK // GATHER_UNROLL, start, 0)
    lax.fori_loop(0, GATHER_CHUNK // GATHER_UNROLL, wait, 0)


def _gather_rows(src, idx, *, name):
    n = src.shape[0]
    r = idx.shape[0]
    assert r % GATHER_CHUNK == 0
    slab = (D // LANE, LANE)
    out = pl.pallas_call(
        _gather_kernel,
        out_shape=jax.ShapeDtypeStruct((r,) + slab, src.dtype),
        grid_spec=pltpu.PrefetchScalarGridSpec(
            num_scalar_prefetch=1,
            grid=(r // GATHER_CHUNK,),
            in_specs=[pl.BlockSpec(memory_space=pl.ANY)],
            out_specs=pl.BlockSpec((GATHER_CHUNK,) + slab, lambda c, idx: (c, 0, 0)),
            scratch_shapes=[pltpu.SemaphoreType.DMA],
        ),
        compiler_params=_params(("arbitrary",), 2 * GATHER_CHUNK * D * src.dtype.itemsize),
        name=name,
    )(idx, src.reshape((n,) + slab))
    return out.reshape(r, D)


N_ASSIGN = 2 * ROWS
_MOE_ROWS_MIN = N_ASSIGN + N_EXPERTS * (MOE_TM - 1)
_MOE_UNIT = MOE_TM * GATHER_CHUNK // math.gcd(MOE_TM, GATHER_CHUNK)
MOE_ROWS = -(-_MOE_ROWS_MIN // _MOE_UNIT) * _MOE_UNIT
assert N_ASSIGN % GATHER_CHUNK == 0


def _moe_plan(route):
    e_flat = route[:, 0:2].astype(jnp.int32).reshape(-1)
    onehot = (e_flat[:, None] == jnp.arange(N_EXPERTS, dtype=jnp.int32)[None, :]).astype(jnp.int32)
    csum = jnp.cumsum(onehot, axis=0)
    rank = jnp.sum(csum * onehot, axis=1) - 1
    counts = csum[-1]
    padded = (counts + MOE_TM - 1) // MOE_TM * MOE_TM
    pad_end = jnp.cumsum(padded)
    pad_start = pad_end - padded
    dest = pad_start[e_flat] + rank
    token = jnp.arange(N_ASSIGN, dtype=jnp.int32) // 2
    src_row = jnp.zeros((MOE_ROWS,), jnp.int32).at[dest].set(token)
    block_start = jnp.arange(MOE_ROWS // MOE_TM, dtype=jnp.int32) * MOE_TM
    block_expert = jnp.minimum(jnp.searchsorted(pad_end, block_start, side="right"), N_EXPERTS - 1).astype(jnp.int32)
    slots = jnp.concatenate([dest[0::2], dest[1::2]])
    return slots, src_row, block_expert


def _moe(h, route, w_gate, w_up, w_down, layer, *, name):
    slots, src_row, block_expert = _moe_plan(route)
    block_expert = block_expert + layer * N_EXPERTS
    xs = _gather_rows(h, src_row, name=name + "_dispatch")
    act = _gmm_gated(xs, w_gate, w_up, block_expert, tm=MOE_TM, tn=1024, name=name + "_up")
    ys = _gmm(act, w_down, block_expert, tm=MOE_TM, tn=512, out_dtype=F32, name=name + "_down")
    return _gather_rows(ys, slots, name=name + "_combine")


def _mixer_conv(h, cache, w_in, conv_w, w_out):
    bcz = _dense(h, w_in, tn=512, out_dtype=F32, name="a_in")
    y, st_p, st_s = _short_conv(bcz, cache, conv_w, name="a_conv")
    out = _dense(y, w_out, tn=512, out_dtype=F32, name="a_out")
    return out, (st_p, st_s)


def _mixer_dilated(h, caches, w_qkv, w_o, rel_bias):
    qkv = _dense(h, w_qkv, tn=512, out_dtype=F32, name="b_qkv")
    n_col = 3 * N_DIL
    qp = qkv[:ROWS_P].reshape(N_P, T_P, 3, N_DIL, GROUP_W)
    qs = qkv[ROWS_P:].reshape(N_S, T_S, 3, N_DIL, GROUP_W)
    outs, lses, rows_p, rows_s = [], [], [], []
    for gi, (win, dil) in enumerate(DIL_CFG):
        heads = slice(gi * B_HG, (gi + 1) * B_HG)
        nb = T_P // dil // BLK
        src = qkv.reshape(ROWS // dil, dil * n_col * GROUP_W)
        o, lse = _banded_attention(
            src, _band_bias(rel_bias, heads, dil, win // dil),
            dil=dil, n_seq=N_P * dil, n_blk=nb,
            row_blk=lambda s, b, dil=dil, nb=nb: (s // dil) * nb + b,
            q_col=lambda s, dil=dil, gi=gi: (s % dil) * n_col + gi,
            k_col=lambda s, dil=dil, gi=gi: (s % dil) * n_col + N_DIL + gi,
            v_col=lambda s, dil=dil, gi=gi: (s % dil) * n_col + 2 * N_DIL + gi,
            qw=GROUP_W, kw=GROUP_W,
            out_shape=jax.ShapeDtypeStruct((ROWS // dil, dil * GROUP_W), F32),
            out_col=lambda s, dil=dil: s % dil,
            group=1, with_lse=True, name=f"b_attn_prompt{gi}")
        bias = _sample_bias(rel_bias, heads, win, dil, win).reshape(B_HG * T_S, win + BLK)
        cache = caches[gi].reshape(N_S, win, 2 * GROUP_W)
        o, lse = _dil_sample_attention(qkv, gi, cache, bias[:, :win], bias[:, win:],
                                       o.reshape(ROWS, GROUP_W), lse.reshape(ROWS, GROUP_W),
                                       name=f"b_attn_sample{gi}")
        outs.append(o)
        lses.append(lse)
        keep = min(win, T_P)
        rows_p.append(jnp.stack([qp[:, T_P - keep:, 1, gi], qp[:, T_P - keep:, 2, gi]], axis=2)
                      .reshape(1, N_P, keep, 2, B_HG, HEAD_DIM))
        rows_s.append(jnp.stack([qs[:, :, 1, gi], qs[:, :, 2, gi]], axis=2)
                      .reshape(1, N_S, T_S, 2, B_HG, HEAD_DIM))
    merged = _merge_dilations(outs, lses, name="b_merge")
    out = _dense(merged, w_o, tn=512, out_dtype=F32, name="b_out")
    return out, (rows_p, rows_s)


def _mixer_swa(h, cache, w_qkv, sink, w_o, rel_bias):
    qkv = _dense(h, w_qkv, tn=512, out_dtype=F32, name="c_qkv")
    heads = slice(0, N_HEADS)
    nb = T_P // BLK
    o = _banded_attention(
        qkv, _band_bias(rel_bias, heads, 1, C_WINDOW - 1), sink.astype(F32),
        dil=1, n_seq=N_P, n_blk=nb, row_blk=lambda s, b: s * nb + b,
        q_col=lambda s: 0, k_col=lambda s: ATTN_W // C_KW, v_col=lambda s: ATTN_W // C_KW + 1,
        qw=ATTN_W, kw=C_KW, out_shape=jax.ShapeDtypeStruct((ROWS, ATTN_W), BF16), out_col=lambda s: 0,
        group=C_GROUP, with_lse=False, name="c_attn_prompt")
    bias = _sample_bias(rel_bias, heads, C_WINDOW, 1, C_WINDOW - 1)
    bias = bias.reshape(C_KV_HEADS, C_GROUP * T_S, 2 * BLK)
    sink_tab = jnp.repeat(sink.astype(F32).reshape(C_KV_HEADS, C_GROUP), T_S, axis=1)[:, :, None]
    o = _swa_sample_attention(qkv, cache.reshape(N_S, C_WINDOW, 2 * C_KW), bias[:, :, :BLK], bias[:, :, BLK:],
                              sink_tab, o, name="c_attn_sample")
    kv_p = qkv[:ROWS_P].reshape(N_P, T_P, -1)[:, T_P - C_WINDOW:, ATTN_W:]
    kv_p = kv_p.reshape(1, N_P, C_WINDOW, 2, C_KV_HEADS, HEAD_DIM)
    kv_s = qkv[ROWS_P:, ATTN_W:].reshape(1, N_S, T_S, 2, C_KV_HEADS, HEAD_DIM)
    out = _dense(o, w_o, tn=512, out_dtype=F32, name="c_out")
    return out, (kv_p, kv_s)


def _mixer_sgu(h, w_uv, w_s, b_s, w_out):
    uv = _dense(h, w_uv, tn=512, out_dtype=F32, act="gelu", name="d_uv")
    y, vn_s = _spatial_gate(uv, w_s, b_s, name="d_gate")
    out = _dense(y, w_out, tn=512, out_dtype=F32, name="d_out")
    return out, vn_s.reshape(1, N_S, T_S, D)


def _modulations(c_prompt, c_sample, ada_w, ada_b):
    n_sub = 2 * DEPTH
    pad_rows = 48
    cond = jax.nn.silu(jnp.concatenate([c_prompt, c_sample], axis=0))
    cond = jnp.concatenate([cond, jnp.zeros((pad_rows - N_P - N_S, D), F32)], axis=0).astype(BF16)
    x = jnp.tile(cond, (n_sub, 1))
    mod = _gmm(x, ada_w.reshape(n_sub, D, 3 * D), jnp.arange(n_sub, dtype=jnp.int32),
               tm=pad_rows, tn=512, out_dtype=F32, name="ada")
    mod = mod.reshape(n_sub, pad_rows, 3 * D) + ada_b.reshape(n_sub, 1, 3 * D)
    res = []
    for s in range(n_sub):
        modp = mod[s, :N_P].reshape(N_P, 1, 3 * D)
        mods = jnp.repeat(mod[s, N_P:N_P + N_S], T_S, axis=0)
        res.append((modp, mods))
    return res


def kernel(x_prompt, x_sample, c_prompt, c_sample, cache_a_conv, cache_b_kv0, cache_b_kv1, cache_b_kv2, cache_c_kv, norm_g, ada_w, ada_b, rel_bias, a_w_in, a_conv_w, a_w_out, b_w_qkv, b_w_o, c_w_qkv, c_sink, c_w_o, d_w_uv, d_w_s, d_b_s, d_w_out, f_w_gate, f_w_up, f_w_down, m_w_router, m_b_router, m_w_gate, m_w_up, m_w_down):
    mods = _modulations(c_prompt, c_sample, ada_w, ada_b)
    x = jnp.concatenate([x_prompt.reshape(ROWS_P, D), x_sample.reshape(ROWS_S, D)], axis=0)
    n_moe = m_w_gate.shape[0]
    mw_gate = m_w_gate.reshape((n_moe * N_EXPERTS,) + m_w_gate.shape[2:])
    mw_up = m_w_up.reshape((n_moe * N_EXPERTS,) + m_w_up.shape[2:])
    mw_down = m_w_down.reshape((n_moe * N_EXPERTS,) + m_w_down.shape[2:])
    state = {}
    for i in range(DEPTH):
        kind = i % 4
        modp, modsr = mods[2 * i]
        h = _norm_mod(x, norm_g[i, 0], modp, modsr, name=f"norm_mix{i}")
        if kind == 0:
            out, state["a"] = _mixer_conv(h, cache_a_conv[0], a_w_in, a_conv_w[0], a_w_out)
        elif kind == 1:
            out, state["b"] = _mixer_dilated(h, (cache_b_kv0[0], cache_b_kv1[0], cache_b_kv2[0]),
                                             b_w_qkv, b_w_o, rel_bias)
        elif kind == 2:
            out, state["c"] = _mixer_swa(h, cache_c_kv[0], c_w_qkv, c_sink[0], c_w_o, rel_bias)
        else:
            out, state["d"] = _mixer_sgu(h, d_w_uv, d_w_s[0], d_b_s[0], d_w_out)
        x = _resid(x, out, norm_g[i, 1], modp, modsr, name=f"resid_mix{i}")

        modp, modsr = mods[2 * i + 1]
        fi = i // 2
        if i % 2 == 0:
            h = _norm_mod(x, norm_g[i, 2], modp, modsr, name=f"norm_ffn{i}")
            act = _gmm_gated(h, f_w_gate, f_w_up, _single(ROWS, DENSE_TM, fi), tm=DENSE_TM, tn=512,
                             name=f"ffn_up{i}")
            out = _dense(act, f_w_down, tn=512, out_dtype=F32, name=f"ffn_down{i}", tm=ROW_BLK, which=fi)
            x = _resid(x, out, norm_g[i, 3], modp, modsr, name=f"resid_ffn{i}")
        else:
            wr = jnp.zeros((D, LANE), F32).at[:, :N_EXPERTS].set(m_w_router[fi])
            br = jnp.zeros((1, LANE), F32).at[0, :N_EXPERTS].set(m_b_router[fi])
            h, route = _norm_mod(x, norm_g[i, 2], modp, modsr, (wr, br), name=f"norm_moe{i}")
            y2 = _moe(h, route, mw_gate, mw_up, mw_down, fi, name=f"moe{i}")
            x = _resid(x, y2, norm_g[i, 3], modp, modsr, route, name=f"resid_moe{i}")

    y_prompt = x[:ROWS_P].reshape(N_P, T_P, D)
    y_sample = x[ROWS_P:].reshape(N_S, T_S, D)
    (a_p, a_s) = state["a"]
    rows_p, rows_s = state["b"]
    c_p, c_s = state["c"]
    return (y_prompt, y_sample, a_p[None], a_s[None],
            rows_p[0], rows_s[0], rows_p[1], rows_s[1], rows_p[2], rows_s[2],
            c_p, c_s, state["d"])
```

```python
import functools
import math

import numpy as np
import jax
import jax.numpy as jnp
from jax import lax
from jax.experimental import pallas as pl
from jax.experimental.pallas import tpu as pltpu

D = 2048
N_P, T_P = 4, 2048
N_S, T_S = 32, 8
ROWS_P = N_P * T_P
ROWS_S = N_S * T_S
ROWS = ROWS_P + ROWS_S
DEPTH = 4
HEAD_DIM = 64
N_HEADS = 24
ATTN_W = N_HEADS * HEAD_DIM
DIL_CFG = ((128, 1), (512, 4), (2048, 16))
N_DIL = len(DIL_CFG)
B_HG = 8
GROUP_W = B_HG * HEAD_DIM
C_WINDOW = 128
C_KV_HEADS = 4
C_GROUP = 6
C_KW = C_KV_HEADS * HEAD_DIM
BLK = 128
SG_GROUPS = 8
SG_GW = D // SG_GROUPS
REL_BUCKETS = 32
REL_MAX_DIST = 2048
N_EXPERTS = 8
RMS_EPS = 1e-6
LN_EPS = 1e-5
NEG = -1e30

LANE = 128
VMEM_LIMIT_CAP = 56 * 1024 * 1024
VMEM_LIMIT_FLOOR = 32 * 1024 * 1024
VMEM_MARGIN = 8 * 1024 * 1024

ROW_BLK = 256
N_ROW_BLKS = ROWS // ROW_BLK
N_PROMPT_BLKS = ROWS_P // ROW_BLK
BLKS_PER_SEQ = T_P // ROW_BLK
SAMPLE_BLK8 = ROWS_P // T_S
DENSE_TM = 768
MOE_TM = 256
GATHER_CHUNK = 512
GATHER_UNROLL = 8

BF16 = jnp.bfloat16
F32 = jnp.float32


def _params(sem, nbytes):
    limit = int(min(max(nbytes + VMEM_MARGIN, VMEM_LIMIT_FLOOR), VMEM_LIMIT_CAP))
    return pltpu.CompilerParams(dimension_semantics=sem, vmem_limit_bytes=limit)


def _weight_changed(be_ref):
    b = pl.program_id(1)
    prev = be_ref[jnp.maximum(b - 1, 0)]
    return jnp.logical_or(b == 0, be_ref[b] != prev)


def _gmm_kernel(be_ref, nu_ref, x_ref, w_ref, o_ref, wc_ref, *, act):
    used = pl.program_id(1) < nu_ref[0]

    @pl.when(used)
    def _():
        @pl.when(_weight_changed(be_ref))
        def _():
            wc_ref[...] = w_ref[...].astype(BF16)

        acc = jnp.dot(x_ref[...].astype(BF16), wc_ref[...], preferred_element_type=F32)
        if act == "gelu":
            acc = jax.nn.gelu(acc)
        o_ref[...] = acc.astype(o_ref.dtype)

    @pl.when(jnp.logical_not(used))
    def _():
        o_ref[...] = jnp.zeros_like(o_ref)


def _gmm_gated_kernel(be_ref, nu_ref, x_ref, wg_ref, wu_ref, o_ref, wgc_ref, wuc_ref):
    used = pl.program_id(1) < nu_ref[0]

    @pl.when(used)
    def _():
        @pl.when(_weight_changed(be_ref))
        def _():
            wgc_ref[...] = wg_ref[...].astype(BF16)
            wuc_ref[...] = wu_ref[...].astype(BF16)

        x = x_ref[...].astype(BF16)
        g = jnp.dot(x, wgc_ref[...], preferred_element_type=F32)
        u = jnp.dot(x, wuc_ref[...], preferred_element_type=F32)
        o_ref[...] = (jax.nn.silu(g) * u).astype(o_ref.dtype)

    @pl.when(jnp.logical_not(used))
    def _():
        o_ref[...] = jnp.zeros_like(o_ref)


def _gmm_specs(tm, k, tn, n_w, col_map):
    def x_map(j, b, be, nu):
        return (jnp.minimum(b, nu[0] - 1), 0)

    def w_map(j, b, be, nu):
        return (be[jnp.minimum(b, nu[0] - 1)], 0, col_map(j))

    in_specs = [pl.BlockSpec((tm, k), x_map)] + [pl.BlockSpec((None, k, tn), w_map)] * n_w
    out_spec = pl.BlockSpec((tm, tn), lambda j, b, be, nu: (b, j))
    return in_specs, out_spec


def _gmm(x, w, block_expert, n_used, *, tm, tn, n_out, out_dtype, act=None, col_map=lambda j: j, name):
    m, k = x.shape
    assert k == w.shape[1] and m % tm == 0 and n_out % tn == 0
    in_specs, out_spec = _gmm_specs(tm, k, tn, 1, col_map)
    nbytes = (2 * tm * k * x.dtype.itemsize + 2 * k * tn * 4 + k * tn * 2
              + 2 * tm * tn * jnp.dtype(out_dtype).itemsize)
    return pl.pallas_call(
        functools.partial(_gmm_kernel, act=act),
        out_shape=jax.ShapeDtypeStruct((m, n_out), out_dtype),
        grid_spec=pltpu.PrefetchScalarGridSpec(
            num_scalar_prefetch=2,
            grid=(n_out // tn, m // tm),
            in_specs=in_specs,
            out_specs=out_spec,
            scratch_shapes=[pltpu.VMEM((k, tn), BF16)],
        ),
        compiler_params=_params(("arbitrary", "arbitrary"), nbytes),
        name=name,
    )(block_expert, n_used, x, w)


def _gmm_gated(x, wg, wu, block_expert, n_used, *, tm, tn, name):
    m, k = x.shape
    n = wg.shape[2]
    assert k == wg.shape[1] and m % tm == 0 and n % tn == 0 and wu.shape == wg.shape
    in_specs, out_spec = _gmm_specs(tm, k, tn, 2, lambda j: j)
    nbytes = 2 * tm * k * x.dtype.itemsize + 4 * k * tn * 4 + 2 * k * tn * 2 + 2 * tm * tn * 2
    return pl.pallas_call(
        _gmm_gated_kernel,
        out_shape=jax.ShapeDtypeStruct((m, n), BF16),
        grid_spec=pltpu.PrefetchScalarGridSpec(
            num_scalar_prefetch=2,
            grid=(n // tn, m // tm),
            in_specs=in_specs,
            out_specs=out_spec,
            scratch_shapes=[pltpu.VMEM((k, tn), BF16), pltpu.VMEM((k, tn), BF16)],
        ),
        compiler_params=_params(("arbitrary", "arbitrary"), nbytes),
        name=name,
    )(block_expert, n_used, x, wg, wu)


def _single(m, tm, which=0):
    return jnp.full((m // tm,), which, jnp.int32), jnp.full((1,), m // tm, jnp.int32)


def _dense(x, w, *, tn, out_dtype, act=None, name, tm=DENSE_TM, which=0, n_out=None, col_map=lambda j: j):
    be, nu = _single(x.shape[0], tm, which)
    return _gmm(x, w, be, nu, tm=tm, tn=tn, n_out=n_out or w.shape[2], out_dtype=out_dtype, act=act,
                col_map=col_map, name=name)


def _rms(x, g):
    return x * lax.rsqrt(jnp.mean(x * x, axis=-1, keepdims=True) + RMS_EPS) * g


def _route(h, wr_ref, br_ref):
    logits = jnp.dot(h, wr_ref[...], preferred_element_type=F32, precision=lax.Precision.HIGHEST) + br_ref[...]
    lane = lax.broadcasted_iota(jnp.int32, logits.shape, 1)
    logits = jnp.where(lane < N_EXPERTS, logits, -jnp.inf)
    m1 = jnp.max(logits, axis=-1, keepdims=True)
    i1 = jnp.min(jnp.where(logits == m1, lane, LANE), axis=-1, keepdims=True)
    rest = jnp.where(lane == i1, -jnp.inf, logits)
    m2 = jnp.max(rest, axis=-1, keepdims=True)
    i2 = jnp.min(jnp.where(rest == m2, lane, LANE), axis=-1, keepdims=True)
    e2 = jnp.exp(m2 - m1)
    g1 = 1.0 / (1.0 + e2)
    g2 = e2 / (1.0 + e2)
    out = jnp.where(lane == 0, i1.astype(F32),
                    jnp.where(lane == 1, i2.astype(F32),
                              jnp.where(lane == 2, g1, jnp.where(lane == 3, g2, 0.0))))
    return out


def _edge_kernel(*refs, split_in, resid, moe_in, norm, route, split_out):
    refs = list(refs)

    def pop(n):
        return [refs.pop(0) for _ in range(n)]

    x_refs = pop(2) if split_in else pop(1) * 2
    if resid:
        y_refs = pop(3) if moe_in else pop(1)
        gr_ref, gate_p, gate_s = pop(3)
    if norm:
        gn_ref, shift_p, scale_p, shift_s, scale_s = pop(5)
        if route:
            wr_ref, br_ref = pop(2)
    if resid:
        xo_refs = pop(2) if split_out else pop(1) * 2
    if norm:
        (h_ref,) = pop(1)
        if route:
            (r_ref,) = pop(1)

    def run(sample):
        x = x_refs[sample][...]
        if resid:
            if moe_in:
                ya_ref, yb_ref, rp_ref = y_refs
                rp = rp_ref[...]
                out = rp[:, 2:3] * ya_ref[...] + rp[:, 3:4] * yb_ref[...]
            else:
                out = y_refs[0][...]
            x = x + (gate_s if sample else gate_p)[...] * _rms(out, gr_ref[...])
            xo_refs[sample][...] = x
        if norm:
            shift = (shift_s if sample else shift_p)[...]
            scale = (scale_s if sample else scale_p)[...]
            h = _rms(x, gn_ref[...]) * (1.0 + scale) + shift
            h_ref[...] = h.astype(h_ref.dtype)
            if route:
                r_ref[...] = _route(h, wr_ref, br_ref)

    i = pl.program_id(0)
    pl.when(i < N_PROMPT_BLKS)(functools.partial(run, 0))
    pl.when(i == N_PROMPT_BLKS)(functools.partial(run, 1))


def _mod_specs(cols):
    specs = []
    for c in cols:
        specs.append(pl.BlockSpec((None, 1, D), lambda i, c=c: (jnp.minimum(i // BLKS_PER_SEQ, N_P - 1), 0, c)))
    for c in cols:
        specs.append(pl.BlockSpec((ROW_BLK, D), lambda i, c=c: (0, c)))
    return specs


def _edge(x, *, y=None, route_in=None, res=None, nrm=None, router=None, split_out=False, name):
    split_in = isinstance(x, tuple)
    resid, norm, moe_in, route = res is not None, nrm is not None, route_in is not None, router is not None
    row = pl.BlockSpec((ROW_BLK, D), lambda i: (i, 0))
    row_p = pl.BlockSpec((ROW_BLK, D), lambda i: (jnp.minimum(i, N_PROMPT_BLKS - 1), 0))
    row_s = pl.BlockSpec((ROW_BLK, D), lambda i: (0, 0))
    gain = pl.BlockSpec((1, D), lambda i: (0, 0))
    lane_row = pl.BlockSpec((ROW_BLK, LANE), lambda i: (i, 0))
    in_specs = [row_p, row_s] if split_in else [row]
    args = list(x) if split_in else [x]
    out_shape, out_specs = [], []
    if resid:
        g, modp, mods = res
        if moe_in:
            in_specs += [row, pl.BlockSpec((ROW_BLK, D), lambda i: (i + N_ROW_BLKS, 0)), lane_row]
            args += [y, y, route_in]
        else:
            in_specs += [row]
            args += [y]
        in_specs += [gain] + _mod_specs((2,))
        args += [g.reshape(1, D), modp, mods]
        if split_out:
            out_shape += [jax.ShapeDtypeStruct((ROWS_P, D), F32), jax.ShapeDtypeStruct((ROWS_S, D), F32)]
            out_specs += [row_p, row_s]
        else:
            out_shape += [jax.ShapeDtypeStruct((ROWS, D), F32)]
            out_specs += [row]
    if norm:
        g, modp, mods = nrm
        in_specs += [gain] + _mod_specs((0, 1))
        args += [g.reshape(1, D), modp, modp, mods, mods]
        if route:
            in_specs += [pl.BlockSpec((D, LANE), lambda i: (0, 0)), pl.BlockSpec((1, LANE), lambda i: (0, 0))]
            args += list(router)
        out_shape += [jax.ShapeDtypeStruct((ROWS, D), BF16)]
        out_specs += [row]
        if route:
            out_shape += [jax.ShapeDtypeStruct((ROWS, LANE), F32)]
            out_specs += [lane_row]
    n_blocks = len(in_specs) + len(out_specs)
    return pl.pallas_call(
        functools.partial(_edge_kernel, split_in=split_in, resid=resid, moe_in=moe_in, norm=norm, route=route,
                          split_out=split_out),
        out_shape=out_shape,
        grid=(N_ROW_BLKS,),
        in_specs=in_specs,
        out_specs=out_specs,
        compiler_params=_params(("arbitrary",), n_blocks * ROW_BLK * D * 4 * 2),
        name=name,
    )(*args)


def _conv_kernel(gb_ref, gc_ref, z_ref, cw_ref, b1_ref, b2_ref, y_ref, stp_ref, us_ref, tail_ref):
    i = pl.program_id(0)
    u = gc_ref[...] * z_ref[...]
    r = lax.broadcasted_iota(jnp.int32, u.shape, 0)
    roll1 = pltpu.roll(u, 1, axis=0)
    roll2 = pltpu.roll(u, 2, axis=0)

    def emit(um1, um2):
        conv = cw_ref[0:1, :] * um2 + cw_ref[1:2, :] * um1 + cw_ref[2:3, :] * u
        y_ref[...] = (gb_ref[...] * conv).astype(y_ref.dtype)

    @pl.when(i < N_PROMPT_BLKS)
    def _():
        @pl.when(i % BLKS_PER_SEQ == 0)
        def _():
            tail_ref[...] = jnp.zeros_like(tail_ref)

        prev2 = tail_ref[6:7, :]
        prev1 = tail_ref[7:8, :]
        emit(jnp.where(r == 0, prev1, roll1),
             jnp.where(r == 0, prev2, jnp.where(r == 1, prev1, roll2)))
        tail_ref[...] = u[ROW_BLK - 8:ROW_BLK, :]
        stp_ref[...] = u[ROW_BLK - 2:ROW_BLK, :]

    @pl.when(i == N_PROMPT_BLKS)
    def _():
        t = r % T_S
        emit(jnp.where(t == 0, b1_ref[...], roll1), jnp.where(t < 2, b2_ref[...], roll2))
        us_ref[...] = u


def _short_conv(bcz, cache, conv_w, *, name):
    zeros = jnp.zeros((N_S, T_S, D), F32)
    b1 = zeros.at[:, 0].set(cache[:, 1]).reshape(ROWS_S, D)
    b2 = zeros.at[:, 0].set(cache[:, 0]).at[:, 1].set(cache[:, 1]).reshape(ROWS_S, D)

    def col(c):
        return pl.BlockSpec((ROW_BLK, D), lambda i, c=c: (i, c))

    const = pl.BlockSpec((ROW_BLK, D), lambda i: (0, 0))
    y, st_p, u_s = pl.pallas_call(
        _conv_kernel,
        out_shape=[jax.ShapeDtypeStruct((ROWS, D), BF16), jax.ShapeDtypeStruct((N_P, 2, D), F32),
                   jax.ShapeDtypeStruct((ROWS_S, D), F32)],
        grid=(N_ROW_BLKS,),
        in_specs=[col(0), col(1), col(2), pl.BlockSpec((3, D), lambda i: (0, 0)), const, const],
        out_specs=[pl.BlockSpec((ROW_BLK, D), lambda i: (i, 0)),
                   pl.BlockSpec((None, 2, D), lambda i: (jnp.minimum(i // BLKS_PER_SEQ, N_P - 1), 0, 0)),
                   const],
        scratch_shapes=[pltpu.VMEM((8, D), F32)],
        compiler_params=_params(("arbitrary",), ROW_BLK * D * 4 * 7 * 2),
        name=name,
    )(bcz, bcz, bcz, conv_w, b1, b2)
    return y, st_p, u_s.reshape(N_S, T_S, D)[:, T_S - 2:]


def _rel_bucket(dist):
    n = np.maximum(np.asarray(dist), 0)
    exact = REL_BUCKETS // 2
    ratio = np.log(np.maximum(n, 1).astype(np.float32) / exact) / math.log(REL_MAX_DIST / exact)
    large = np.minimum(exact + (ratio * (REL_BUCKETS - exact)).astype(np.int32), REL_BUCKETS - 1)
    return np.where(n < exact, n, large).astype(np.int32)


def _toeplitz(a, n_rows, n_cols):
    h, p = a.shape
    assert n_cols < p
    t = jnp.tile(a, (1, n_rows))[:, :n_rows * (p - 1)].reshape(h, n_rows, p - 1)
    return t[:, :, :n_cols]


def _bias_line(rel_bias, heads, dist, valid):
    tab = rel_bias[_rel_bucket(dist)][:, heads]
    return jnp.where(valid[:, None], tab, NEG).T.astype(F32)


def _band_bias(rel_bias, heads, dil, max_dist):
    p = 3 * BLK
    m = np.arange(p)
    dist = np.where(m < 2 * BLK, BLK - m, BLK + p - m)
    valid = (dist >= 0) & (dist <= max_dist)
    return _toeplitz(_bias_line(rel_bias, heads, dist * dil, valid), BLK, 2 * BLK)


def _sample_bias(rel_bias, heads, win, dil, max_dist):
    lk = win + BLK
    p = lk + T_S
    m = np.arange(p)
    dist = np.where(m < lk, win - m, win + p - m)
    valid = (dist >= 0) & (dist % dil == 0) & (dist <= max_dist)
    return _toeplitz(_bias_line(rel_bias, heads, dist, valid), T_S, lk)


def _softmax_pv(s, v, sink=None):
    m = jnp.max(s, axis=-1, keepdims=True)
    if sink is not None:
        m = jnp.maximum(m, sink)
    p = jnp.exp(s - m)
    l = jnp.sum(p, axis=-1, keepdims=True)
    if sink is not None:
        l = l + jnp.exp(sink - m)
    o = jnp.dot(p.astype(BF16), v, preferred_element_type=F32) / l
    return o, m + jnp.log(l)


def _nt_dot(a, b):
    return lax.dot_general(a, b, (((1,), (1,)), ((), ())), preferred_element_type=F32)


def _banded_kernel(*refs, n_seq, **kw):
    out_refs = refs[-2:] if kw["with_lse"] else refs[-1:]

    @pl.when(pl.program_id(0) < n_seq)
    def _():
        _banded_body(*refs, **kw)

    @pl.when(pl.program_id(0) >= n_seq)
    def _():
        for ref in out_refs:
            ref[...] = jnp.zeros_like(ref)


def _banded_body(*refs, n_heads, group, with_lse, has_sink):
    refs = list(refs)
    sink_ref = refs.pop(0) if has_sink else None
    q_ref, kp_ref, kc_ref, vp_ref, vc_ref, bias_ref = refs[:6]
    o_ref = refs[6]
    lse_ref = refs[7] if with_lse else None
    qb = pl.program_id(1)
    q = (q_ref[...] * (HEAD_DIM ** -0.5)).astype(BF16)
    k = jnp.concatenate([kp_ref[...], kc_ref[...]], axis=0).astype(BF16)
    v = jnp.concatenate([vp_ref[...], vc_ref[...]], axis=0).astype(BF16)
    col = lax.broadcasted_iota(jnp.int32, (BLK, 2 * BLK), 1)
    no_prev = jnp.logical_and(qb == 0, col < BLK)
    for h in range(n_heads):
        hk = h // group
        qs = slice(h * HEAD_DIM, (h + 1) * HEAD_DIM)
        ks = slice(hk * HEAD_DIM, (hk + 1) * HEAD_DIM)
        s = _nt_dot(q[:, qs], k[:, ks]) + bias_ref[h]
        s = jnp.where(no_prev, NEG, s)
        sink = sink_ref[h] if has_sink else None
        o, lse = _softmax_pv(s, v[:, ks], sink)
        o_ref[:, qs] = o.astype(o_ref.dtype)
        if with_lse:
            lse_ref[:, qs] = jnp.broadcast_to(lse, (BLK, HEAD_DIM))


def _banded_attention(src, bias, sink=None, *, n_seq, q_col, k_col, v_col, qw, kw, out_dtype, group, with_lse, name):
    n_heads = qw // HEAD_DIM
    has_sink = sink is not None
    n_blk = ROWS_P // BLK // n_seq
    fill_row0 = ROWS_P // BLK
    fill_blks = ROWS_S // BLK
    n_fill_seq = -(-fill_blks // n_blk)

    def cur(col):
        return lambda s, b: (jnp.minimum(s, n_seq - 1) * n_blk + b, col)

    def prev(col):
        return lambda s, b: (jnp.minimum(s, n_seq - 1) * n_blk + jnp.maximum(b - 1, 0), col)

    def out_map(s, b):
        fill = jnp.minimum((s - n_seq) * n_blk + b, fill_blks - 1)
        return (jnp.where(s >= n_seq, fill_row0 + fill, s * n_blk + b), 0)

    in_specs = [pl.BlockSpec((BLK, qw), cur(q_col)),
                pl.BlockSpec((BLK, kw), prev(k_col)), pl.BlockSpec((BLK, kw), cur(k_col)),
                pl.BlockSpec((BLK, kw), prev(v_col)), pl.BlockSpec((BLK, kw), cur(v_col)),
                pl.BlockSpec((n_heads, BLK, 2 * BLK), lambda s, b: (0, 0, 0))]
    args = [src, src, src, src, src, bias]
    if has_sink:
        in_specs = [pl.BlockSpec(memory_space=pltpu.SMEM)] + in_specs
        args = [sink] + args
    out_shapes = [jax.ShapeDtypeStruct((ROWS, qw), out_dtype)]
    out_specs = [pl.BlockSpec((BLK, qw), out_map)]
    if with_lse:
        out_shapes.append(jax.ShapeDtypeStruct((ROWS, qw), F32))
        out_specs.append(pl.BlockSpec((BLK, qw), out_map))
    nbytes = (BLK * qw * 4 * 3 + BLK * kw * 4 * 4) * 2 + n_heads * BLK * 2 * BLK * 4 * 2
    res = pl.pallas_call(
        functools.partial(_banded_kernel, n_seq=n_seq, n_heads=n_heads, group=group, with_lse=with_lse,
                          has_sink=has_sink),
        out_shape=out_shapes,
        grid=(n_seq + n_fill_seq, n_blk),
        in_specs=in_specs,
        out_specs=out_specs,
        compiler_params=_params(("arbitrary", "arbitrary"), nbytes),
        name=name,
    )(*args)
    return res if with_lse else res[0]


def _dil_sample_kernel(q_ref, kn_ref, vn_ref, cache_ref, bc_ref, bn_ref, o_in, lse_in, o_ref, lse_ref):
    del o_in, lse_in
    q = q_ref[...] * (HEAD_DIM ** -0.5)
    qt = jnp.concatenate([q] * B_HG, axis=0)
    row_h = lax.broadcasted_iota(jnp.int32, qt.shape, 0) // T_S
    col_h = lax.broadcasted_iota(jnp.int32, qt.shape, 1) // HEAD_DIM
    qbd = jnp.where(row_h == col_h, qt, 0.0).astype(BF16)
    pad = jnp.zeros((BLK - T_S, GROUP_W), F32)
    kn = jnp.concatenate([kn_ref[...], pad], axis=0).astype(BF16)
    vn = jnp.concatenate([vn_ref[...], pad], axis=0).astype(BF16)
    kc = cache_ref[:, 0:GROUP_W].astype(BF16)
    vc = cache_ref[:, GROUP_W:2 * GROUP_W].astype(BF16)
    sc = _nt_dot(qbd, kc) + bc_ref[...]
    sn = _nt_dot(qbd, kn) + bn_ref[...]
    m = jnp.maximum(jnp.max(sc, axis=-1, keepdims=True), jnp.max(sn, axis=-1, keepdims=True))
    pc = jnp.exp(sc - m)
    pn = jnp.exp(sn - m)
    l = jnp.sum(pc, axis=-1, keepdims=True) + jnp.sum(pn, axis=-1, keepdims=True)
    big = (jnp.dot(pc.astype(BF16), vc, preferred_element_type=F32)
           + jnp.dot(pn.astype(BF16), vn, preferred_element_type=F32)) / l
    lse = m + jnp.log(l)
    for h in range(B_HG):
        rs = slice(h * T_S, (h + 1) * T_S)
        cs = slice(h * HEAD_DIM, (h + 1) * HEAD_DIM)
        o_ref[:, cs] = big[rs, cs]
        lse_ref[:, cs] = jnp.broadcast_to(lse[rs], (T_S, HEAD_DIM))


def _dil_sample_attention(qkv, cache, bias_c, bias_n, o, lse, *, name):
    lb = cache.shape[1]
    row = lambda c: pl.BlockSpec((T_S, GROUP_W), lambda n, c=c: (SAMPLE_BLK8 + n, c))
    const = lambda n: (0, 0)
    anyspec = pl.BlockSpec(memory_space=pl.ANY)
    nbytes = lb * 2 * GROUP_W * 4 * 2 + 64 * lb * 4 * 6
    return pl.pallas_call(
        _dil_sample_kernel,
        out_shape=[jax.ShapeDtypeStruct((ROWS, GROUP_W), F32)] * 2,
        grid=(N_S,),
        in_specs=[row(0), row(1), row(2),
                  pl.BlockSpec((None, lb, 2 * GROUP_W), lambda n: (n, 0, 0)),
                  pl.BlockSpec((B_HG * T_S, lb), const), pl.BlockSpec((B_HG * T_S, BLK), const),
                  anyspec, anyspec],
        out_specs=[row(0)] * 2,
        input_output_aliases={6: 0, 7: 1},
        compiler_params=_params(("parallel",), nbytes),
        name=name,
    )(qkv, qkv, qkv, cache, bias_c, bias_n, o, lse)


def _merge_kernel(o0, o1, o2, l0, l1, l2, out_ref):
    ls = [l0[...], l1[...], l2[...]]
    m = jnp.maximum(jnp.maximum(ls[0], ls[1]), ls[2])
    es = [jnp.exp(l - m) for l in ls]
    tot = es[0] + es[1] + es[2]
    for g, o in enumerate((o0, o1, o2)):
        out_ref[:, g * GROUP_W:(g + 1) * GROUP_W] = (o[...] * (es[g] / tot)).astype(out_ref.dtype)


def _merge_dilations(outs, lses, *, name):
    spec = pl.BlockSpec((ROW_BLK, GROUP_W), lambda i: (i, 0))
    return pl.pallas_call(
        _merge_kernel,
        out_shape=jax.ShapeDtypeStruct((ROWS, ATTN_W), BF16),
        grid=(N_ROW_BLKS,),
        in_specs=[spec] * 6,
        out_specs=pl.BlockSpec((ROW_BLK, ATTN_W), lambda i: (i, 0)),
        compiler_params=_params(("parallel",), ROW_BLK * GROUP_W * 4 * 8 * 2),
        name=name,
    )(*outs, *lses)


def _swa_sample_kernel(q_ref, kvn_ref, cache_ref, bc_ref, bn_ref, sink_ref, o_in, o_ref):
    del o_in
    q = q_ref[...] * (HEAD_DIM ** -0.5)
    pad = jnp.zeros((BLK - T_S, 2 * C_KW), F32)
    kvn = jnp.concatenate([kvn_ref[...], pad], axis=0).astype(BF16)
    kvc = cache_ref[...].astype(BF16)
    for hk in range(C_KV_HEADS):
        heads = range(hk * C_GROUP, (hk + 1) * C_GROUP)
        qh = jnp.concatenate([q[:, h * HEAD_DIM:(h + 1) * HEAD_DIM] for h in heads], axis=0).astype(BF16)
        ks = slice(hk * HEAD_DIM, (hk + 1) * HEAD_DIM)
        vs = slice(C_KW + hk * HEAD_DIM, C_KW + (hk + 1) * HEAD_DIM)
        sc = _nt_dot(qh, kvc[:, ks]) + bc_ref[hk]
        sn = _nt_dot(qh, kvn[:, ks]) + bn_ref[hk]
        sink = sink_ref[hk]
        m = jnp.maximum(jnp.maximum(jnp.max(sc, axis=-1, keepdims=True),
                                    jnp.max(sn, axis=-1, keepdims=True)), sink)
        pc = jnp.exp(sc - m)
        pn = jnp.exp(sn - m)
        l = jnp.sum(pc, axis=-1, keepdims=True) + jnp.sum(pn, axis=-1, keepdims=True) + jnp.exp(sink - m)
        o = (jnp.dot(pc.astype(BF16), kvc[:, vs], preferred_element_type=F32)
             + jnp.dot(pn.astype(BF16), kvn[:, vs], preferred_element_type=F32)) / l
        for gi, h in enumerate(heads):
            o_ref[:, h * HEAD_DIM:(h + 1) * HEAD_DIM] = o[gi * T_S:(gi + 1) * T_S].astype(o_ref.dtype)


def _swa_sample_attention(qkv, cache, bias_c, bias_n, sink_tab, o, *, name):
    rows = C_GROUP * T_S
    const = lambda n: (0, 0, 0)
    return pl.pallas_call(
        _swa_sample_kernel,
        out_shape=jax.ShapeDtypeStruct((ROWS, ATTN_W), o.dtype),
        grid=(N_S,),
        in_specs=[pl.BlockSpec((T_S, ATTN_W), lambda n: (SAMPLE_BLK8 + n, 0)),
                  pl.BlockSpec((T_S, 2 * C_KW), lambda n: (SAMPLE_BLK8 + n, ATTN_W // (2 * C_KW))),
                  pl.BlockSpec((None, C_WINDOW, 2 * C_KW), lambda n: (n, 0, 0)),
                  pl.BlockSpec((C_KV_HEADS, rows, BLK), const), pl.BlockSpec((C_KV_HEADS, rows, BLK), const),
                  pl.BlockSpec((C_KV_HEADS, rows, 1), const),
                  pl.BlockSpec(memory_space=pl.ANY)],
        out_specs=pl.BlockSpec((T_S, ATTN_W), lambda n: (SAMPLE_BLK8 + n, 0)),
        input_output_aliases={6: 0},
        compiler_params=_params(("parallel",), 1 << 20),
        name=name,
    )(qkv, qkv, cache, bias_c, bias_n, sink_tab, o)


def _sgu_kernel(u_ref, v_ref, w_ref, b_ref, ws_ref, bs_ref, y_ref, vn_ref):
    i = pl.program_id(0)
    v = v_ref[...]
    mu = jnp.mean(v, axis=-1, keepdims=True)
    var = jnp.mean(jnp.square(v - mu), axis=-1, keepdims=True)
    vn = (v - mu) * lax.rsqrt(var + LN_EPS)
    vb = vn.astype(BF16)

    @pl.when(i < N_PROMPT_BLKS)
    def _():
        causal = (lax.broadcasted_iota(jnp.int32, (BLK, BLK), 0) >= lax.broadcasted_iota(jnp.int32, (BLK, BLK), 1))
        for g in range(SG_GROUPS):
            cs = slice(g * SG_GW, (g + 1) * SG_GW)
            wg = jnp.where(causal, w_ref[g], 0.0).astype(BF16)
            for c in range(ROW_BLK // BLK):
                rs = slice(c * BLK, (c + 1) * BLK)
                s = jnp.dot(wg, vb[rs, cs], preferred_element_type=F32) + b_ref[g]
                y_ref[rs, cs] = (u_ref[rs, cs] * s).astype(y_ref.dtype)

    @pl.when(i == N_PROMPT_BLKS)
    def _():
        vn_ref[...] = vn
        for g in range(SG_GROUPS):
            cs = slice(g * SG_GW, (g + 1) * SG_GW)
            s = jnp.dot(ws_ref[g].astype(BF16), vb[:, cs], preferred_element_type=F32) + bs_ref[g]
            y_ref[:, cs] = (u_ref[:, cs] * s).astype(y_ref.dtype)


def _spatial_gate(uv, w_s, b_s, *, name):
    r = np.arange(ROWS_S)
    same_seq = (r[:, None] // T_S) == (r[None, :] // T_S)
    causal = (r[:, None] % T_S) >= (r[None, :] % T_S)
    w8 = jnp.tile(w_s[:, :T_S, :T_S], (1, N_S, N_S))
    ws = jnp.where((same_seq & causal)[None], w8, 0.0)
    bs = jnp.tile(b_s[:, :T_S], (1, N_S)).reshape(SG_GROUPS, ROWS_S, 1)
    const3 = lambda i: (0, 0, 0)
    nbytes = ROW_BLK * D * 4 * 4 * 2 + SG_GROUPS * (BLK * BLK + ROW_BLK * ROW_BLK) * 4 * 2
    return pl.pallas_call(
        _sgu_kernel,
        out_shape=[jax.ShapeDtypeStruct((ROWS, D), BF16), jax.ShapeDtypeStruct((ROWS_S, D), F32)],
        grid=(N_ROW_BLKS,),
        in_specs=[pl.BlockSpec((ROW_BLK, D), lambda i: (i, 0)), pl.BlockSpec((ROW_BLK, D), lambda i: (i, 1)),
                  pl.BlockSpec((SG_GROUPS, BLK, BLK), const3), pl.BlockSpec((SG_GROUPS, BLK, 1), const3),
                  pl.BlockSpec((SG_GROUPS, ROWS_S, ROWS_S), const3), pl.BlockSpec((SG_GROUPS, ROWS_S, 1), const3)],
        out_specs=[pl.BlockSpec((ROW_BLK, D), lambda i: (i, 0)), pl.BlockSpec((ROWS_S, D), lambda i: (0, 0))],
        compiler_params=_params(("arbitrary",), nbytes),
        name=name,
    )(uv, uv, w_s, b_s.reshape(SG_GROUPS, BLK, 1), ws, bs)


def _gather_kernel(idx_ref, src_ref, o_ref, sem):
    base = pl.program_id(0) * GATHER_CHUNK

    def row_copy(r):
        return pltpu.make_async_copy(src_ref.at[idx_ref[base + r]], o_ref.at[r], sem)

    def start(g, carry):
        for u in range(GATHER_UNROLL):
            row_copy(g * GATHER_UNROLL + u).start()
        return carry

    def wait(g, carry):
        for u in range(GATHER_UNROLL):
            row_copy(g * GATHER_UNROLL + u).wait()
        return carry

    lax.fori_loop(0, GATHER_CHUNK // GATHER_UNROLL, start, 0)
    lax.fori_loop(0, GATHER_CHUNK // GATHER_UNROLL, wait, 0)


def _gather_rows(src, idx, *, name):
    n = src.shape[0]
    r = idx.shape[0]
    assert r % GATHER_CHUNK == 0
    slab = (D // LANE, LANE)
    out = pl.pallas_call(
        _gather_kernel,
        out_shape=jax.ShapeDtypeStruct((r,) + slab, src.dtype),
        grid_spec=pltpu.PrefetchScalarGridSpec(
            num_scalar_prefetch=1,
            grid=(r // GATHER_CHUNK,),
            in_specs=[pl.BlockSpec(memory_space=pl.ANY)],
            out_specs=pl.BlockSpec((GATHER_CHUNK,) + slab, lambda c, idx: (c, 0, 0)),
            scratch_shapes=[pltpu.SemaphoreType.DMA],
        ),
        compiler_params=_params(("arbitrary",), 2 * GATHER_CHUNK * D * src.dtype.itemsize),
        name=name,
    )(idx, src.reshape((n,) + slab))
    return out.reshape(r, D)


N_ASSIGN = 2 * ROWS
_MOE_ROWS_MIN = N_ASSIGN + N_EXPERTS * (MOE_TM - 1)
_MOE_UNIT = MOE_TM * GATHER_CHUNK // math.gcd(MOE_TM, GATHER_CHUNK)
MOE_ROWS = -(-_MOE_ROWS_MIN // _MOE_UNIT) * _MOE_UNIT
assert N_ASSIGN % GATHER_CHUNK == 0


def _moe_plan(route):
    e_flat = route[:, 0:2].astype(jnp.int32).reshape(-1)
    onehot = (e_flat[:, None] == jnp.arange(N_EXPERTS, dtype=jnp.int32)[None, :]).astype(jnp.int32)
    csum = jnp.cumsum(onehot, axis=0)
    rank = jnp.sum(csum * onehot, axis=1) - 1
    counts = csum[-1]
    padded = (counts + MOE_TM - 1) // MOE_TM * MOE_TM
    pad_end = jnp.cumsum(padded)
    pad_start = pad_end - padded
    dest = pad_start[e_flat] + rank
    token = jnp.arange(N_ASSIGN, dtype=jnp.int32) // 2
    src_row = jnp.zeros((MOE_ROWS,), jnp.int32).at[dest].set(token)
    block_start = jnp.arange(MOE_ROWS // MOE_TM, dtype=jnp.int32) * MOE_TM
    block_expert = jnp.minimum(jnp.searchsorted(pad_end, block_start, side="right"), N_EXPERTS - 1).astype(jnp.int32)
    slots = jnp.concatenate([dest[0::2], dest[1::2]])
    n_used = (pad_end[-1:] // MOE_TM).astype(jnp.int32)
    return slots, src_row, block_expert, n_used


def _moe(h, route, w_gate, w_up, w_down, layer, *, name):
    slots, src_row, block_expert, n_used = _moe_plan(route)
    block_expert = block_expert + layer * N_EXPERTS
    xs = _gather_rows(h, src_row, name=name + "_dispatch")
    act = _gmm_gated(xs, w_gate, w_up, block_expert, n_used, tm=MOE_TM, tn=1024, name=name + "_up")
    ys = _gmm(act, w_down, block_expert, n_used, tm=MOE_TM, tn=512, n_out=D, out_dtype=F32, name=name + "_down")
    return _gather_rows(ys, slots, name=name + "_combine")


def _mixer_conv(h, cache, w_in, conv_w, w_out):
    bcz = _dense(h, w_in, tn=512, out_dtype=F32, name="a_in")
    y, st_p, st_s = _short_conv(bcz, cache, conv_w, name="a_conv")
    out = _dense(y, w_out, tn=512, out_dtype=F32, name="a_out")
    return out, (st_p, st_s)


def _to_residues(a, dil):
    if dil == 1:
        return a
    w = a.shape[1]
    p = a[:ROWS_P].reshape(N_P, T_P // dil, dil, w).swapaxes(1, 2).reshape(ROWS_P, w)
    return jnp.concatenate([p, a[ROWS_P:]], axis=0)


def _from_residues(a, dil):
    if dil == 1:
        return a
    w = a.shape[1]
    p = a[:ROWS_P].reshape(N_P, dil, T_P // dil, w).swapaxes(1, 2).reshape(ROWS_P, w)
    return jnp.concatenate([p, a[ROWS_P:]], axis=0)


def _mixer_dilated(h, caches, w_qkv, w_o, rel_bias):
    outs, lses, rows_p, rows_s = [], [], [], []
    for gi, (win, dil) in enumerate(DIL_CFG):
        heads = slice(gi * B_HG, (gi + 1) * B_HG)
        qkv = _dense(_to_residues(h, dil), w_qkv, tn=GROUP_W, out_dtype=F32, n_out=3 * GROUP_W,
                     col_map=lambda j, gi=gi: N_DIL * j + gi, name=f"b_qkv{gi}")
        o, lse = _banded_attention(qkv, _band_bias(rel_bias, heads, dil, win // dil),
                                   n_seq=N_P * dil, q_col=0, k_col=1, v_col=2, qw=GROUP_W, kw=GROUP_W,
                                   out_dtype=F32, group=1, with_lse=True, name=f"b_attn_prompt{gi}")
        bias = _sample_bias(rel_bias, heads, win, dil, win).reshape(B_HG * T_S, win + BLK)
        cache = caches[gi].reshape(N_S, win, 2 * GROUP_W)
        o, lse = _dil_sample_attention(qkv, cache, bias[:, :win], bias[:, win:], o, lse, name=f"b_attn_sample{gi}")
        outs.append(_from_residues(o, dil))
        lses.append(_from_residues(lse, dil))
        keep = min(win, T_P)
        kv = qkv[:ROWS_P].reshape(N_P, dil, T_P // dil, 3, GROUP_W)[:, :, (T_P - keep) // dil:, 1:]
        rows_p.append(kv.swapaxes(1, 2).reshape(1, N_P, keep, 2, B_HG, HEAD_DIM))
        rows_s.append(qkv[ROWS_P:].reshape(N_S, T_S, 3, GROUP_W)[:, :, 1:]
                      .reshape(1, N_S, T_S, 2, B_HG, HEAD_DIM))
    merged = _merge_dilations(outs, lses, name="b_merge")
    out = _dense(merged, w_o, tn=512, out_dtype=F32, name="b_out")
    return out, (rows_p, rows_s)


def _mixer_swa(h, cache, w_qkv, sink, w_o, rel_bias):
    qkv = _dense(h, w_qkv, tn=512, out_dtype=F32, name="c_qkv")
    heads = slice(0, N_HEADS)
    o = _banded_attention(qkv, _band_bias(rel_bias, heads, 1, C_WINDOW - 1), sink.astype(F32),
                          n_seq=N_P, q_col=0, k_col=ATTN_W // C_KW, v_col=ATTN_W // C_KW + 1, qw=ATTN_W, kw=C_KW,
                          out_dtype=BF16, group=C_GROUP, with_lse=False, name="c_attn_prompt")
    bias = _sample_bias(rel_bias, heads, C_WINDOW, 1, C_WINDOW - 1)
    bias = bias.reshape(C_KV_HEADS, C_GROUP * T_S, 2 * BLK)
    sink_tab = jnp.repeat(sink.astype(F32).reshape(C_KV_HEADS, C_GROUP), T_S, axis=1)[:, :, None]
    o = _swa_sample_attention(qkv, cache.reshape(N_S, C_WINDOW, 2 * C_KW), bias[:, :, :BLK], bias[:, :, BLK:],
                              sink_tab, o, name="c_attn_sample")
    kv_p = jnp.stack([qkv[(n + 1) * T_P - C_WINDOW:(n + 1) * T_P, ATTN_W:] for n in range(N_P)])
    kv_p = kv_p.reshape(1, N_P, C_WINDOW, 2, C_KV_HEADS, HEAD_DIM)
    kv_s = qkv[ROWS_P:, ATTN_W:].reshape(1, N_S, T_S, 2, C_KV_HEADS, HEAD_DIM)
    out = _dense(o, w_o, tn=512, out_dtype=F32, name="c_out")
    return out, (kv_p, kv_s)


def _mixer_sgu(h, w_uv, w_s, b_s, w_out):
    uv = _dense(h, w_uv, tn=512, out_dtype=F32, act="gelu", name="d_uv")
    y, vn_s = _spatial_gate(uv, w_s, b_s, name="d_gate")
    out = _dense(y, w_out, tn=512, out_dtype=F32, name="d_out")
    return out, vn_s.reshape(1, N_S, T_S, D)


def _modulations(c_prompt, c_sample, ada_w, ada_b):
    n_sub = 2 * DEPTH
    pad_rows = 48
    cond = jax.nn.silu(jnp.concatenate([c_prompt, c_sample], axis=0))
    cond = jnp.concatenate([cond, jnp.zeros((pad_rows - N_P - N_S, D), F32)], axis=0).astype(BF16)
    x = jnp.tile(cond, (n_sub, 1))
    mod = _gmm(x, ada_w.reshape(n_sub, D, 3 * D), jnp.arange(n_sub, dtype=jnp.int32),
               jnp.full((1,), n_sub, jnp.int32), tm=pad_rows, tn=512, n_out=3 * D, out_dtype=F32, name="ada")
    mod = mod.reshape(n_sub, pad_rows, 3 * D) + ada_b.reshape(n_sub, 1, 3 * D)
    res = []
    for s in range(n_sub):
        modp = mod[s, :N_P].reshape(N_P, 1, 3 * D)
        mods = jnp.repeat(mod[s, N_P:N_P + N_S], T_S, axis=0)
        res.append((modp, mods))
    return res


def kernel(x_prompt, x_sample, c_prompt, c_sample, cache_a_conv, cache_b_kv0, cache_b_kv1, cache_b_kv2, cache_c_kv, norm_g, ada_w, ada_b, rel_bias, a_w_in, a_conv_w, a_w_out, b_w_qkv, b_w_o, c_w_qkv, c_sink, c_w_o, d_w_uv, d_w_s, d_b_s, d_w_out, f_w_gate, f_w_up, f_w_down, m_w_router, m_b_router, m_w_gate, m_w_up, m_w_down):
    mods = _modulations(c_prompt, c_sample, ada_w, ada_b)
    n_moe = m_w_gate.shape[0]
    mw_gate = m_w_gate.reshape((n_moe * N_EXPERTS,) + m_w_gate.shape[2:])
    mw_up = m_w_up.reshape((n_moe * N_EXPERTS,) + m_w_up.shape[2:])
    mw_down = m_w_down.reshape((n_moe * N_EXPERTS,) + m_w_down.shape[2:])

    def sub(i, j):
        return norm_g[i, 2 * j], norm_g[i, 2 * j + 1], mods[2 * i + j]

    def router(fi):
        wr = jnp.zeros((D, LANE), F32).at[:, :N_EXPERTS].set(m_w_router[fi])
        br = jnp.zeros((1, LANE), F32).at[0, :N_EXPERTS].set(m_b_router[fi])
        return wr, br

    x = (x_prompt.reshape(ROWS_P, D), x_sample.reshape(ROWS_S, D))
    g_in, _, (modp, modsr) = sub(0, 0)
    (h,) = _edge(x, nrm=(g_in, modp, modsr), name="norm_mix0")
    state = {}
    for i in range(DEPTH):
        kind = i % 4
        if kind == 0:
            out, state["a"] = _mixer_conv(h, cache_a_conv[0], a_w_in, a_conv_w[0], a_w_out)
        elif kind == 1:
            out, state["b"] = _mixer_dilated(h, (cache_b_kv0[0], cache_b_kv1[0], cache_b_kv2[0]),
                                             b_w_qkv, b_w_o, rel_bias)
        elif kind == 2:
            out, state["c"] = _mixer_swa(h, cache_c_kv[0], c_w_qkv, c_sink[0], c_w_o, rel_bias)
        else:
            out, state["d"] = _mixer_sgu(h, d_w_uv, d_w_s[0], d_b_s[0], d_w_out)

        _, g_out, (modp, modsr) = sub(i, 0)
        g_in, _, (modp2, modsr2) = sub(i, 1)
        fi = i // 2
        moe = i % 2 == 1
        res = _edge(x, y=out, res=(g_out, modp, modsr), nrm=(g_in, modp2, modsr2),
                    router=router(fi) if moe else None, name=f"edge_mix{i}")
        x, h = res[0], res[1]
        if moe:
            route = res[2]
            y = _moe(h, route, mw_gate, mw_up, mw_down, fi, name=f"moe{i}")
        else:
            route = None
            be, nu = _single(ROWS, DENSE_TM, fi)
            act = _gmm_gated(h, f_w_gate, f_w_up, be, nu, tm=DENSE_TM, tn=512, name=f"ffn_up{i}")
            y = _dense(act, f_w_down, tn=512, out_dtype=F32, name=f"ffn_down{i}", tm=ROW_BLK, which=fi)

        _, g_out, (modp, modsr) = sub(i, 1)
        if i + 1 < DEPTH:
            g_in, _, (modp2, modsr2) = sub(i + 1, 0)
            x, h = _edge(x, y=y, route_in=route, res=(g_out, modp, modsr), nrm=(g_in, modp2, modsr2),
                         name=f"edge_ffn{i}")
        else:
            y_prompt, y_sample = _edge(x, y=y, route_in=route, res=(g_out, modp, modsr), split_out=True,
                                       name=f"edge_ffn{i}")

    (a_p, a_s) = state["a"]
    rows_p, rows_s = state["b"]
    c_p, c_s = state["c"]
    return (y_prompt.reshape(N_P, T_P, D), y_sample.reshape(N_S, T_S, D), a_p[None], a_s[None],
            rows_p[0], rows_s[0], rows_p[1], rows_s[1], rows_p[2], rows_s[2],
            c_p, c_s, state["d"])
```

```python
import functools
import math

import numpy as np
import jax
import jax.numpy as jnp
from jax import lax
from jax.experimental import pallas as pl
from jax.experimental.pallas import tpu as pltpu

D = 2048
N_P, T_P = 4, 2048
N_S, T_S = 32, 8
ROWS_P = N_P * T_P
ROWS_S = N_S * T_S
ROWS = ROWS_P + ROWS_S
DEPTH = 4
HEAD_DIM = 64
N_HEADS = 24
ATTN_W = N_HEADS * HEAD_DIM
DIL_CFG = ((128, 1), (512, 4), (2048, 16))
N_DIL = len(DIL_CFG)
B_HG = 8
GROUP_W = B_HG * HEAD_DIM
C_WINDOW = 128
C_KV_HEADS = 4
C_GROUP = 6
C_KW = C_KV_HEADS * HEAD_DIM
BLK = 128
SG_GROUPS = 8
SG_GW = D // SG_GROUPS
REL_BUCKETS = 32
REL_MAX_DIST = 2048
N_EXPERTS = 8
RMS_EPS = 1e-6
LN_EPS = 1e-5
NEG = -1e30

LANE = 128
VMEM_LIMIT_CAP = 56 * 1024 * 1024
VMEM_LIMIT_FLOOR = 32 * 1024 * 1024
VMEM_MARGIN = 8 * 1024 * 1024

ROW_BLK = 256
N_ROW_BLKS = ROWS // ROW_BLK
N_PROMPT_BLKS = ROWS_P // ROW_BLK
BLKS_PER_SEQ = T_P // ROW_BLK
SAMPLE_BLK8 = ROWS_P // T_S
DENSE_TM = 768
DENSE_TN = 1024
MOE_TM = 512
MOE_DOWN_TM = 256
GATHER_CHUNK = 512
GATHER_UNROLL = 8

BF16 = jnp.bfloat16
F32 = jnp.float32


def _params(sem, nbytes):
    limit = int(min(max(nbytes + VMEM_MARGIN, VMEM_LIMIT_FLOOR), VMEM_LIMIT_CAP))
    return pltpu.CompilerParams(dimension_semantics=sem, vmem_limit_bytes=limit)


def _weight_changed(be_ref):
    b = pl.program_id(1)
    prev = be_ref[jnp.maximum(b - 1, 0)]
    return jnp.logical_or(b == 0, be_ref[b] != prev)


def _gmm_kernel(be_ref, nu_ref, x_ref, w_ref, o_ref, wc_ref, *, act):
    used = pl.program_id(1) < nu_ref[0]

    @pl.when(used)
    def _():
        @pl.when(_weight_changed(be_ref))
        def _():
            wc_ref[...] = w_ref[...].astype(BF16)

        acc = jnp.dot(x_ref[...].astype(BF16), wc_ref[...], preferred_element_type=F32)
        if act == "gelu":
            acc = jax.nn.gelu(acc)
        o_ref[...] = acc.astype(o_ref.dtype)

    @pl.when(jnp.logical_not(used))
    def _():
        o_ref[...] = jnp.zeros_like(o_ref)


def _gmm_gated_kernel(be_ref, nu_ref, x_ref, wg_ref, wu_ref, o_ref, wgc_ref, wuc_ref):
    used = pl.program_id(1) < nu_ref[0]

    @pl.when(used)
    def _():
        @pl.when(_weight_changed(be_ref))
        def _():
            wgc_ref[...] = wg_ref[...].astype(BF16)
            wuc_ref[...] = wu_ref[...].astype(BF16)

        x = x_ref[...].astype(BF16)
        g = jnp.dot(x, wgc_ref[...], preferred_element_type=F32)
        u = jnp.dot(x, wuc_ref[...], preferred_element_type=F32)
        o_ref[...] = (jax.nn.silu(g) * u).astype(o_ref.dtype)

    @pl.when(jnp.logical_not(used))
    def _():
        o_ref[...] = jnp.zeros_like(o_ref)


def _gmm_specs(tm, k, tn, n_w, col_map):
    def x_map(j, b, be, nu):
        return (jnp.minimum(b, nu[0] - 1), 0)

    def w_map(j, b, be, nu):
        return (be[jnp.minimum(b, nu[0] - 1)], 0, col_map(j))

    in_specs = [pl.BlockSpec((tm, k), x_map)] + [pl.BlockSpec((None, k, tn), w_map)] * n_w
    out_spec = pl.BlockSpec((tm, tn), lambda j, b, be, nu: (b, j))
    return in_specs, out_spec


def _gmm(x, w, block_expert, n_used, *, tm, tn, n_out, out_dtype, act=None, col_map=lambda j: j, name):
    m, k = x.shape
    assert k == w.shape[1] and m % tm == 0 and n_out % tn == 0
    in_specs, out_spec = _gmm_specs(tm, k, tn, 1, col_map)
    nbytes = (2 * tm * k * x.dtype.itemsize + 2 * k * tn * 4 + k * tn * 2
              + 2 * tm * tn * jnp.dtype(out_dtype).itemsize)
    return pl.pallas_call(
        functools.partial(_gmm_kernel, act=act),
        out_shape=jax.ShapeDtypeStruct((m, n_out), out_dtype),
        grid_spec=pltpu.PrefetchScalarGridSpec(
            num_scalar_prefetch=2,
            grid=(n_out // tn, m // tm),
            in_specs=in_specs,
            out_specs=out_spec,
            scratch_shapes=[pltpu.VMEM((k, tn), BF16)],
        ),
        compiler_params=_params(("arbitrary", "arbitrary"), nbytes),
        name=name,
    )(block_expert, n_used, x, w)


def _gmm_gated(x, wg, wu, block_expert, n_used, *, tm, tn, name):
    m, k = x.shape
    n = wg.shape[2]
    assert k == wg.shape[1] and m % tm == 0 and n % tn == 0 and wu.shape == wg.shape
    in_specs, out_spec = _gmm_specs(tm, k, tn, 2, lambda j: j)
    nbytes = 2 * tm * k * x.dtype.itemsize + 4 * k * tn * 4 + 2 * k * tn * 2 + 2 * tm * tn * 2
    return pl.pallas_call(
        _gmm_gated_kernel,
        out_shape=jax.ShapeDtypeStruct((m, n), BF16),
        grid_spec=pltpu.PrefetchScalarGridSpec(
            num_scalar_prefetch=2,
            grid=(n // tn, m // tm),
            in_specs=in_specs,
            out_specs=out_spec,
            scratch_shapes=[pltpu.VMEM((k, tn), BF16), pltpu.VMEM((k, tn), BF16)],
        ),
        compiler_params=_params(("arbitrary", "arbitrary"), nbytes),
        name=name,
    )(block_expert, n_used, x, wg, wu)


def _single(m, tm, which=0):
    return jnp.full((m // tm,), which, jnp.int32), jnp.full((1,), m // tm, jnp.int32)


def _dense(x, w, *, tn, out_dtype, act=None, name, tm=DENSE_TM, which=0, n_out=None, col_map=lambda j: j):
    be, nu = _single(x.shape[0], tm, which)
    return _gmm(x, w, be, nu, tm=tm, tn=tn, n_out=n_out or w.shape[2], out_dtype=out_dtype, act=act,
                col_map=col_map, name=name)


def _rms(x, g):
    return x * lax.rsqrt(jnp.mean(x * x, axis=-1, keepdims=True) + RMS_EPS) * g


def _route(h, wr_ref, br_ref):
    logits = jnp.dot(h, wr_ref[...], preferred_element_type=F32, precision=lax.Precision.HIGHEST) + br_ref[...]
    lane = lax.broadcasted_iota(jnp.int32, logits.shape, 1)
    logits = jnp.where(lane < N_EXPERTS, logits, -jnp.inf)
    m1 = jnp.max(logits, axis=-1, keepdims=True)
    i1 = jnp.min(jnp.where(logits == m1, lane, LANE), axis=-1, keepdims=True)
    rest = jnp.where(lane == i1, -jnp.inf, logits)
    m2 = jnp.max(rest, axis=-1, keepdims=True)
    i2 = jnp.min(jnp.where(rest == m2, lane, LANE), axis=-1, keepdims=True)
    e2 = jnp.exp(m2 - m1)
    g1 = 1.0 / (1.0 + e2)
    g2 = e2 / (1.0 + e2)
    out = jnp.where(lane == 0, i1.astype(F32),
                    jnp.where(lane == 1, i2.astype(F32),
                              jnp.where(lane == 2, g1, jnp.where(lane == 3, g2, 0.0))))
    return out


def _edge_kernel(*refs, split_in, resid, moe_in, norm, route, split_out):
    refs = list(refs)

    def pop(n):
        return [refs.pop(0) for _ in range(n)]

    x_refs = pop(2) if split_in else pop(1) * 2
    if resid:
        y_refs = pop(3) if moe_in else pop(1)
        gr_ref, gate_p, gate_s = pop(3)
    if norm:
        gn_ref, shift_p, scale_p, shift_s, scale_s = pop(5)
        if route:
            wr_ref, br_ref = pop(2)
    if resid:
        xo_refs = pop(2) if split_out else pop(1) * 2
    if norm:
        (h_ref,) = pop(1)
        if route:
            (r_ref,) = pop(1)

    def run(sample):
        x = x_refs[sample][...]
        if resid:
            if moe_in:
                ya_ref, yb_ref, rp_ref = y_refs
                rp = rp_ref[...]
                out = rp[:, 2:3] * ya_ref[...] + rp[:, 3:4] * yb_ref[...]
            else:
                out = y_refs[0][...]
            x = x + (gate_s if sample else gate_p)[...] * _rms(out, gr_ref[...])
            xo_refs[sample][...] = x
        if norm:
            shift = (shift_s if sample else shift_p)[...]
            scale = (scale_s if sample else scale_p)[...]
            h = _rms(x, gn_ref[...]) * (1.0 + scale) + shift
            h_ref[...] = h.astype(h_ref.dtype)
            if route:
                r_ref[...] = _route(h, wr_ref, br_ref)

    i = pl.program_id(0)
    pl.when(i < N_PROMPT_BLKS)(functools.partial(run, 0))
    pl.when(i == N_PROMPT_BLKS)(functools.partial(run, 1))


def _mod_specs(cols):
    specs = []
    for c in cols:
        specs.append(pl.BlockSpec((None, 1, D), lambda i, c=c: (jnp.minimum(i // BLKS_PER_SEQ, N_P - 1), 0, c)))
    for c in cols:
        specs.append(pl.BlockSpec((ROW_BLK, D), lambda i, c=c: (0, c)))
    return specs


def _edge(x, *, y=None, route_in=None, res=None, nrm=None, router=None, split_out=False, name):
    split_in = isinstance(x, tuple)
    resid, norm, moe_in, route = res is not None, nrm is not None, route_in is not None, router is not None
    row = pl.BlockSpec((ROW_BLK, D), lambda i: (i, 0))
    row_p = pl.BlockSpec((ROW_BLK, D), lambda i: (jnp.minimum(i, N_PROMPT_BLKS - 1), 0))
    row_s = pl.BlockSpec((ROW_BLK, D), lambda i: (0, 0))
    gain = pl.BlockSpec((1, D), lambda i: (0, 0))
    lane_row = pl.BlockSpec((ROW_BLK, LANE), lambda i: (i, 0))
    in_specs = [row_p, row_s] if split_in else [row]
    args = list(x) if split_in else [x]
    out_shape, out_specs = [], []
    if resid:
        g, modp, mods = res
        if moe_in:
            in_specs += [row, pl.BlockSpec((ROW_BLK, D), lambda i: (i + N_ROW_BLKS, 0)), lane_row]
            args += [y, y, route_in]
        else:
            in_specs += [row]
            args += [y]
        in_specs += [gain] + _mod_specs((2,))
        args += [g.reshape(1, D), modp, mods]
        if split_out:
            out_shape += [jax.ShapeDtypeStruct((ROWS_P, D), F32), jax.ShapeDtypeStruct((ROWS_S, D), F32)]
            out_specs += [row_p, row_s]
        else:
            out_shape += [jax.ShapeDtypeStruct((ROWS, D), F32)]
            out_specs += [row]
    if norm:
        g, modp, mods = nrm
        in_specs += [gain] + _mod_specs((0, 1))
        args += [g.reshape(1, D), modp, modp, mods, mods]
        if route:
            in_specs += [pl.BlockSpec((D, LANE), lambda i: (0, 0)), pl.BlockSpec((1, LANE), lambda i: (0, 0))]
            args += list(router)
        out_shape += [jax.ShapeDtypeStruct((ROWS, D), BF16)]
        out_specs += [row]
        if route:
            out_shape += [jax.ShapeDtypeStruct((ROWS, LANE), F32)]
            out_specs += [lane_row]
    n_blocks = len(in_specs) + len(out_specs)
    return pl.pallas_call(
        functools.partial(_edge_kernel, split_in=split_in, resid=resid, moe_in=moe_in, norm=norm, route=route,
                          split_out=split_out),
        out_shape=out_shape,
        grid=(N_ROW_BLKS,),
        in_specs=in_specs,
        out_specs=out_specs,
        compiler_params=_params(("arbitrary",), n_blocks * ROW_BLK * D * 4 * 2),
        name=name,
    )(*args)


def _conv_kernel(gb_ref, gc_ref, z_ref, cw_ref, b1_ref, b2_ref, y_ref, stp_ref, us_ref, tail_ref):
    i = pl.program_id(0)
    u = gc_ref[...] * z_ref[...]
    r = lax.broadcasted_iota(jnp.int32, u.shape, 0)
    roll1 = pltpu.roll(u, 1, axis=0)
    roll2 = pltpu.roll(u, 2, axis=0)

    def emit(um1, um2):
        conv = cw_ref[0:1, :] * um2 + cw_ref[1:2, :] * um1 + cw_ref[2:3, :] * u
        y_ref[...] = (gb_ref[...] * conv).astype(y_ref.dtype)

    @pl.when(i < N_PROMPT_BLKS)
    def _():
        @pl.when(i % BLKS_PER_SEQ == 0)
        def _():
            tail_ref[...] = jnp.zeros_like(tail_ref)

        prev2 = tail_ref[6:7, :]
        prev1 = tail_ref[7:8, :]
        emit(jnp.where(r == 0, prev1, roll1),
             jnp.where(r == 0, prev2, jnp.where(r == 1, prev1, roll2)))
        tail_ref[...] = u[ROW_BLK - 8:ROW_BLK, :]
        stp_ref[...] = u[ROW_BLK - 2:ROW_BLK, :]

    @pl.when(i == N_PROMPT_BLKS)
    def _():
        t = r % T_S
        emit(jnp.where(t == 0, b1_ref[...], roll1), jnp.where(t < 2, b2_ref[...], roll2))
        us_ref[...] = u


def _short_conv(bcz, cache, conv_w, *, name):
    zeros = jnp.zeros((N_S, T_S, D), F32)
    b1 = zeros.at[:, 0].set(cache[:, 1]).reshape(ROWS_S, D)
    b2 = zeros.at[:, 0].set(cache[:, 0]).at[:, 1].set(cache[:, 1]).reshape(ROWS_S, D)

    def col(c):
        return pl.BlockSpec((ROW_BLK, D), lambda i, c=c: (i, c))

    const = pl.BlockSpec((ROW_BLK, D), lambda i: (0, 0))
    y, st_p, u_s = pl.pallas_call(
        _conv_kernel,
        out_shape=[jax.ShapeDtypeStruct((ROWS, D), BF16), jax.ShapeDtypeStruct((N_P, 2, D), F32),
                   jax.ShapeDtypeStruct((ROWS_S, D), F32)],
        grid=(N_ROW_BLKS,),
        in_specs=[col(0), col(1), col(2), pl.BlockSpec((3, D), lambda i: (0, 0)), const, const],
        out_specs=[pl.BlockSpec((ROW_BLK, D), lambda i: (i, 0)),
                   pl.BlockSpec((None, 2, D), lambda i: (jnp.minimum(i // BLKS_PER_SEQ, N_P - 1), 0, 0)),
                   const],
        scratch_shapes=[pltpu.VMEM((8, D), F32)],
        compiler_params=_params(("arbitrary",), ROW_BLK * D * 4 * 7 * 2),
        name=name,
    )(bcz, bcz, bcz, conv_w, b1, b2)
    return y, st_p, u_s.reshape(N_S, T_S, D)[:, T_S - 2:]


def _rel_bucket(dist):
    n = np.maximum(np.asarray(dist), 0)
    exact = REL_BUCKETS // 2
    ratio = np.log(np.maximum(n, 1).astype(np.float32) / exact) / math.log(REL_MAX_DIST / exact)
    large = np.minimum(exact + (ratio * (REL_BUCKETS - exact)).astype(np.int32), REL_BUCKETS - 1)
    return np.where(n < exact, n, large).astype(np.int32)


def _toeplitz(a, n_rows, n_cols):
    h, p = a.shape
    assert n_cols < p
    t = jnp.tile(a, (1, n_rows))[:, :n_rows * (p - 1)].reshape(h, n_rows, p - 1)
    return t[:, :, :n_cols]


def _bias_line(rel_bias, heads, dist, valid):
    tab = rel_bias[_rel_bucket(dist)][:, heads]
    return jnp.where(valid[:, None], tab, NEG).T.astype(F32)


def _band_bias(rel_bias, heads, dil, max_dist):
    p = 3 * BLK
    m = np.arange(p)
    dist = np.where(m < 2 * BLK, BLK - m, BLK + p - m)
    valid = (dist >= 0) & (dist <= max_dist)
    return _toeplitz(_bias_line(rel_bias, heads, dist * dil, valid), BLK, 2 * BLK)


def _sample_bias(rel_bias, heads, win, dil, max_dist):
    lk = win + BLK
    p = lk + T_S
    m = np.arange(p)
    dist = np.where(m < lk, win - m, win + p - m)
    valid = (dist >= 0) & (dist % dil == 0) & (dist <= max_dist)
    return _toeplitz(_bias_line(rel_bias, heads, dist, valid), T_S, lk)


def _softmax_pv(s, v, sink=None):
    m = jnp.max(s, axis=-1, keepdims=True)
    if sink is not None:
        m = jnp.maximum(m, sink)
    p = jnp.exp(s - m)
    l = jnp.sum(p, axis=-1, keepdims=True)
    if sink is not None:
        l = l + jnp.exp(sink - m)
    o = jnp.dot(p.astype(BF16), v, preferred_element_type=F32) / l
    return o, m + jnp.log(l)


def _nt_dot(a, b):
    return lax.dot_general(a, b, (((1,), (1,)), ((), ())), preferred_element_type=F32)


def _banded_kernel(*refs, n_seq, **kw):
    out_refs = refs[-2:] if kw["with_lse"] else refs[-1:]

    @pl.when(pl.program_id(0) < n_seq)
    def _():
        _banded_body(*refs, **kw)

    @pl.when(pl.program_id(0) >= n_seq)
    def _():
        for ref in out_refs:
            ref[...] = jnp.zeros_like(ref)


def _banded_body(*refs, n_heads, group, with_lse, has_sink):
    refs = list(refs)
    sink_ref = refs.pop(0) if has_sink else None
    q_ref, kp_ref, kc_ref, vp_ref, vc_ref, bias_ref = refs[:6]
    o_ref = refs[6]
    lse_ref = refs[7] if with_lse else None
    qb = pl.program_id(1)
    q = (q_ref[...] * (HEAD_DIM ** -0.5)).astype(BF16)
    k = jnp.concatenate([kp_ref[...], kc_ref[...]], axis=0).astype(BF16)
    v = jnp.concatenate([vp_ref[...], vc_ref[...]], axis=0).astype(BF16)
    col = lax.broadcasted_iota(jnp.int32, (BLK, 2 * BLK), 1)
    no_prev = jnp.logical_and(qb == 0, col < BLK)
    for h in range(n_heads):
        hk = h // group
        qs = slice(h * HEAD_DIM, (h + 1) * HEAD_DIM)
        ks = slice(hk * HEAD_DIM, (hk + 1) * HEAD_DIM)
        s = _nt_dot(q[:, qs], k[:, ks]) + bias_ref[h]
        s = jnp.where(no_prev, NEG, s)
        sink = sink_ref[h] if has_sink else None
        o, lse = _softmax_pv(s, v[:, ks], sink)
        o_ref[:, qs] = o.astype(o_ref.dtype)
        if with_lse:
            lse_ref[:, qs] = jnp.broadcast_to(lse, (BLK, HEAD_DIM))


def _banded_attention(src, bias, sink=None, *, n_seq, q_col, k_col, v_col, qw, kw, out_dtype, group, with_lse, name):
    n_heads = qw // HEAD_DIM
    has_sink = sink is not None
    n_blk = ROWS_P // BLK // n_seq
    fill_row0 = ROWS_P // BLK
    fill_blks = ROWS_S // BLK
    n_fill_seq = -(-fill_blks // n_blk)

    def cur(col):
        return lambda s, b: (jnp.minimum(s, n_seq - 1) * n_blk + b, col)

    def prev(col):
        return lambda s, b: (jnp.minimum(s, n_seq - 1) * n_blk + jnp.maximum(b - 1, 0), col)

    def out_map(s, b):
        fill = jnp.minimum((s - n_seq) * n_blk + b, fill_blks - 1)
        return (jnp.where(s >= n_seq, fill_row0 + fill, s * n_blk + b), 0)

    in_specs = [pl.BlockSpec((BLK, qw), cur(q_col)),
                pl.BlockSpec((BLK, kw), prev(k_col)), pl.BlockSpec((BLK, kw), cur(k_col)),
                pl.BlockSpec((BLK, kw), prev(v_col)), pl.BlockSpec((BLK, kw), cur(v_col)),
                pl.BlockSpec((n_heads, BLK, 2 * BLK), lambda s, b: (0, 0, 0))]
    args = [src, src, src, src, src, bias]
    if has_sink:
        in_specs = [pl.BlockSpec(memory_space=pltpu.SMEM)] + in_specs
        args = [sink] + args
    out_shapes = [jax.ShapeDtypeStruct((ROWS, qw), out_dtype)]
    out_specs = [pl.BlockSpec((BLK, qw), out_map)]
    if with_lse:
        out_shapes.append(jax.ShapeDtypeStruct((ROWS, qw), F32))
        out_specs.append(pl.BlockSpec((BLK, qw), out_map))
    nbytes = (BLK * qw * 4 * 3 + BLK * kw * 4 * 4) * 2 + n_heads * BLK * 2 * BLK * 4 * 2
    res = pl.pallas_call(
        functools.partial(_banded_kernel, n_seq=n_seq, n_heads=n_heads, group=group, with_lse=with_lse,
                          has_sink=has_sink),
        out_shape=out_shapes,
        grid=(n_seq + n_fill_seq, n_blk),
        in_specs=in_specs,
        out_specs=out_specs,
        compiler_params=_params(("arbitrary", "arbitrary"), nbytes),
        name=name,
    )(*args)
    return res if with_lse else res[0]


def _dil_sample_kernel(q_ref, kn_ref, vn_ref, cache_ref, bc_ref, bn_ref, o_in, lse_in, o_ref, lse_ref):
    del o_in, lse_in
    q = q_ref[...] * (HEAD_DIM ** -0.5)
    qt = jnp.concatenate([q] * B_HG, axis=0)
    row_h = lax.broadcasted_iota(jnp.int32, qt.shape, 0) // T_S
    col_h = lax.broadcasted_iota(jnp.int32, qt.shape, 1) // HEAD_DIM
    qbd = jnp.where(row_h == col_h, qt, 0.0).astype(BF16)
    pad = jnp.zeros((BLK - T_S, GROUP_W), F32)
    kn = jnp.concatenate([kn_ref[...], pad], axis=0).astype(BF16)
    vn = jnp.concatenate([vn_ref[...], pad], axis=0).astype(BF16)
    kc = cache_ref[:, 0:GROUP_W].astype(BF16)
    vc = cache_ref[:, GROUP_W:2 * GROUP_W].astype(BF16)
    sc = _nt_dot(qbd, kc) + bc_ref[...]
    sn = _nt_dot(qbd, kn) + bn_ref[...]
    m = jnp.maximum(jnp.max(sc, axis=-1, keepdims=True), jnp.max(sn, axis=-1, keepdims=True))
    pc = jnp.exp(sc - m)
    pn = jnp.exp(sn - m)
    l = jnp.sum(pc, axis=-1, keepdims=True) + jnp.sum(pn, axis=-1, keepdims=True)
    big = (jnp.dot(pc.astype(BF16), vc, preferred_element_type=F32)
           + jnp.dot(pn.astype(BF16), vn, preferred_element_type=F32)) / l
    lse = m + jnp.log(l)
    for h in range(B_HG):
        rs = slice(h * T_S, (h + 1) * T_S)
        cs = slice(h * HEAD_DIM, (h + 1) * HEAD_DIM)
        o_ref[:, cs] = big[rs, cs]
        lse_ref[:, cs] = jnp.broadcast_to(lse[rs], (T_S, HEAD_DIM))


def _dil_sample_attention(qkv, cache, bias_c, bias_n, o, lse, *, name):
    lb = cache.shape[1]
    row = lambda c: pl.BlockSpec((T_S, GROUP_W), lambda n, c=c: (SAMPLE_BLK8 + n, c))
    const = lambda n: (0, 0)
    anyspec = pl.BlockSpec(memory_space=pl.ANY)
    nbytes = lb * 2 * GROUP_W * 4 * 2 + 64 * lb * 4 * 6
    return pl.pallas_call(
        _dil_sample_kernel,
        out_shape=[jax.ShapeDtypeStruct((ROWS, GROUP_W), F32)] * 2,
        grid=(N_S,),
        in_specs=[row(0), row(1), row(2),
                  pl.BlockSpec((None, lb, 2 * GROUP_W), lambda n: (n, 0, 0)),
                  pl.BlockSpec((B_HG * T_S, lb), const), pl.BlockSpec((B_HG * T_S, BLK), const),
                  anyspec, anyspec],
        out_specs=[row(0)] * 2,
        input_output_aliases={6: 0, 7: 1},
        compiler_params=_params(("parallel",), nbytes),
        name=name,
    )(qkv, qkv, qkv, cache, bias_c, bias_n, o, lse)


def _merge_kernel(o0, o1, o2, l0, l1, l2, out_ref):
    ls = [l0[...], l1[...], l2[...]]
    m = jnp.maximum(jnp.maximum(ls[0], ls[1]), ls[2])
    es = [jnp.exp(l - m) for l in ls]
    tot = es[0] + es[1] + es[2]
    for g, o in enumerate((o0, o1, o2)):
        out_ref[:, g * GROUP_W:(g + 1) * GROUP_W] = (o[...] * (es[g] / tot)).astype(out_ref.dtype)


def _merge_dilations(outs, lses, *, name):
    spec = pl.BlockSpec((ROW_BLK, GROUP_W), lambda i: (i, 0))
    return pl.pallas_call(
        _merge_kernel,
        out_shape=jax.ShapeDtypeStruct((ROWS, ATTN_W), BF16),
        grid=(N_ROW_BLKS,),
        in_specs=[spec] * 6,
        out_specs=pl.BlockSpec((ROW_BLK, ATTN_W), lambda i: (i, 0)),
        compiler_params=_params(("parallel",), ROW_BLK * GROUP_W * 4 * 8 * 2),
        name=name,
    )(*outs, *lses)


def _swa_sample_kernel(q_ref, kvn_ref, cache_ref, bc_ref, bn_ref, sink_ref, o_in, o_ref):
    del o_in
    q = q_ref[...] * (HEAD_DIM ** -0.5)
    pad = jnp.zeros((BLK - T_S, 2 * C_KW), F32)
    kvn = jnp.concatenate([kvn_ref[...], pad], axis=0).astype(BF16)
    kvc = cache_ref[...].astype(BF16)
    for hk in range(C_KV_HEADS):
        heads = range(hk * C_GROUP, (hk + 1) * C_GROUP)
        qh = jnp.concatenate([q[:, h * HEAD_DIM:(h + 1) * HEAD_DIM] for h in heads], axis=0).astype(BF16)
        ks = slice(hk * HEAD_DIM, (hk + 1) * HEAD_DIM)
        vs = slice(C_KW + hk * HEAD_DIM, C_KW + (hk + 1) * HEAD_DIM)
        sc = _nt_dot(qh, kvc[:, ks]) + bc_ref[hk]
        sn = _nt_dot(qh, kvn[:, ks]) + bn_ref[hk]
        sink = sink_ref[hk]
        m = jnp.maximum(jnp.maximum(jnp.max(sc, axis=-1, keepdims=True),
                                    jnp.max(sn, axis=-1, keepdims=True)), sink)
        pc = jnp.exp(sc - m)
        pn = jnp.exp(sn - m)
        l = jnp.sum(pc, axis=-1, keepdims=True) + jnp.sum(pn, axis=-1, keepdims=True) + jnp.exp(sink - m)
        o = (jnp.dot(pc.astype(BF16), kvc[:, vs], preferred_element_type=F32)
             + jnp.dot(pn.astype(BF16), kvn[:, vs], preferred_element_type=F32)) / l
        for gi, h in enumerate(heads):
            o_ref[:, h * HEAD_DIM:(h + 1) * HEAD_DIM] = o[gi * T_S:(gi + 1) * T_S].astype(o_ref.dtype)


def _swa_sample_attention(qkv, cache, bias_c, bias_n, sink_tab, o, *, name):
    rows = C_GROUP * T_S
    const = lambda n: (0, 0, 0)
    return pl.pallas_call(
        _swa_sample_kernel,
        out_shape=jax.ShapeDtypeStruct((ROWS, ATTN_W), o.dtype),
        grid=(N_S,),
        in_specs=[pl.BlockSpec((T_S, ATTN_W), lambda n: (SAMPLE_BLK8 + n, 0)),
                  pl.BlockSpec((T_S, 2 * C_KW), lambda n: (SAMPLE_BLK8 + n, ATTN_W // (2 * C_KW))),
                  pl.BlockSpec((None, C_WINDOW, 2 * C_KW), lambda n: (n, 0, 0)),
                  pl.BlockSpec((C_KV_HEADS, rows, BLK), const), pl.BlockSpec((C_KV_HEADS, rows, BLK), const),
                  pl.BlockSpec((C_KV_HEADS, rows, 1), const),
                  pl.BlockSpec(memory_space=pl.ANY)],
        out_specs=pl.BlockSpec((T_S, ATTN_W), lambda n: (SAMPLE_BLK8 + n, 0)),
        input_output_aliases={6: 0},
        compiler_params=_params(("parallel",), 1 << 20),
        name=name,
    )(qkv, qkv, cache, bias_c, bias_n, sink_tab, o)


def _sgu_kernel(u_ref, v_ref, w_ref, b_ref, ws_ref, bs_ref, y_ref, vn_ref):
    i = pl.program_id(0)
    v = v_ref[...]
    mu = jnp.mean(v, axis=-1, keepdims=True)
    var = jnp.mean(jnp.square(v - mu), axis=-1, keepdims=True)
    vn = (v - mu) * lax.rsqrt(var + LN_EPS)
    vb = vn.astype(BF16)

    @pl.when(i < N_PROMPT_BLKS)
    def _():
        causal = (lax.broadcasted_iota(jnp.int32, (BLK, BLK), 0) >= lax.broadcasted_iota(jnp.int32, (BLK, BLK), 1))
        for g in range(SG_GROUPS):
            cs = slice(g * SG_GW, (g + 1) * SG_GW)
            wg = jnp.where(causal, w_ref[g], 0.0).astype(BF16)
            for c in range(ROW_BLK // BLK):
                rs = slice(c * BLK, (c + 1) * BLK)
                s = jnp.dot(wg, vb[rs, cs], preferred_element_type=F32) + b_ref[g]
                y_ref[rs, cs] = (u_ref[rs, cs] * s).astype(y_ref.dtype)

    @pl.when(i == N_PROMPT_BLKS)
    def _():
        vn_ref[...] = vn
        for g in range(SG_GROUPS):
            cs = slice(g * SG_GW, (g + 1) * SG_GW)
            s = jnp.dot(ws_ref[g].astype(BF16), vb[:, cs], preferred_element_type=F32) + bs_ref[g]
            y_ref[:, cs] = (u_ref[:, cs] * s).astype(y_ref.dtype)


def _spatial_gate(uv, w_s, b_s, *, name):
    r = np.arange(ROWS_S)
    same_seq = (r[:, None] // T_S) == (r[None, :] // T_S)
    causal = (r[:, None] % T_S) >= (r[None, :] % T_S)
    w8 = jnp.tile(w_s[:, :T_S, :T_S], (1, N_S, N_S))
    ws = jnp.where((same_seq & causal)[None], w8, 0.0)
    bs = jnp.tile(b_s[:, :T_S], (1, N_S)).reshape(SG_GROUPS, ROWS_S, 1)
    const3 = lambda i: (0, 0, 0)
    nbytes = ROW_BLK * D * 4 * 4 * 2 + SG_GROUPS * (BLK * BLK + ROW_BLK * ROW_BLK) * 4 * 2
    return pl.pallas_call(
        _sgu_kernel,
        out_shape=[jax.ShapeDtypeStruct((ROWS, D), BF16), jax.ShapeDtypeStruct((ROWS_S, D), F32)],
        grid=(N_ROW_BLKS,),
        in_specs=[pl.BlockSpec((ROW_BLK, D), lambda i: (i, 0)), pl.BlockSpec((ROW_BLK, D), lambda i: (i, 1)),
                  pl.BlockSpec((SG_GROUPS, BLK, BLK), const3), pl.BlockSpec((SG_GROUPS, BLK, 1), const3),
                  pl.BlockSpec((SG_GROUPS, ROWS_S, ROWS_S), const3), pl.BlockSpec((SG_GROUPS, ROWS_S, 1), const3)],
        out_specs=[pl.BlockSpec((ROW_BLK, D), lambda i: (i, 0)), pl.BlockSpec((ROWS_S, D), lambda i: (0, 0))],
        compiler_params=_params(("arbitrary",), nbytes),
        name=name,
    )(uv, uv, w_s, b_s.reshape(SG_GROUPS, BLK, 1), ws, bs)


def _gather_kernel(idx_ref, src_ref, o_ref, sem):
    base = pl.program_id(0) * GATHER_CHUNK

    def row_copy(r):
        return pltpu.make_async_copy(src_ref.at[idx_ref[base + r]], o_ref.at[r], sem)

    def start(g, carry):
        for u in range(GATHER_UNROLL):
            row_copy(g * GATHER_UNROLL + u).start()
        return carry

    def wait(g, carry):
        for u in range(GATHER_UNROLL):
            row_copy(g * GATHER_UNROLL + u).wait()
        return carry

    lax.fori_loop(0, GATHER_CHUNK // GATHER_UNROLL, start, 0)
    lax.fori_loop(0, GATHER_CHUNK // GATHER_UNROLL, wait, 0)


def _gather_rows(src, idx, *, name):
    n = src.shape[0]
    r = idx.shape[0]
    assert r % GATHER_CHUNK == 0
    slab = (D // LANE, LANE)
    out = pl.pallas_call(
        _gather_kernel,
        out_shape=jax.ShapeDtypeStruct((r,) + slab, src.dtype),
        grid_spec=pltpu.PrefetchScalarGridSpec(
            num_scalar_prefetch=1,
            grid=(r // GATHER_CHUNK,),
            in_specs=[pl.BlockSpec(memory_space=pl.ANY)],
            out_specs=pl.BlockSpec((GATHER_CHUNK,) + slab, lambda c, idx: (c, 0, 0)),
            scratch_shapes=[pltpu.SemaphoreType.DMA],
        ),
        compiler_params=_params(("arbitrary",), 2 * GATHER_CHUNK * D * src.dtype.itemsize),
        name=name,
    )(idx, src.reshape((n,) + slab))
    return out.reshape(r, D)


N_ASSIGN = 2 * ROWS
_MOE_ROWS_MIN = N_ASSIGN + N_EXPERTS * (MOE_TM - 1)
_MOE_UNIT = MOE_TM * GATHER_CHUNK // math.gcd(MOE_TM, GATHER_CHUNK)
MOE_ROWS = -(-_MOE_ROWS_MIN // _MOE_UNIT) * _MOE_UNIT
assert N_ASSIGN % GATHER_CHUNK == 0


def _moe_plan(route):
    e_flat = route[:, 0:2].astype(jnp.int32).reshape(-1)
    onehot = (e_flat[:, None] == jnp.arange(N_EXPERTS, dtype=jnp.int32)[None, :]).astype(jnp.int32)
    csum = jnp.cumsum(onehot, axis=0)
    rank = jnp.sum(csum * onehot, axis=1) - 1
    counts = csum[-1]
    padded = (counts + MOE_TM - 1) // MOE_TM * MOE_TM
    pad_end = jnp.cumsum(padded)
    pad_start = pad_end - padded
    dest = pad_start[e_flat] + rank
    token = jnp.arange(N_ASSIGN, dtype=jnp.int32) // 2
    src_row = jnp.zeros((MOE_ROWS,), jnp.int32).at[dest].set(token)
    block_start = jnp.arange(MOE_ROWS // MOE_TM, dtype=jnp.int32) * MOE_TM
    block_expert = jnp.minimum(jnp.searchsorted(pad_end, block_start, side="right"), N_EXPERTS - 1).astype(jnp.int32)
    slots = jnp.concatenate([dest[0::2], dest[1::2]])
    n_used = (pad_end[-1:] // MOE_TM).astype(jnp.int32)
    return slots, src_row, block_expert, n_used


def _moe(h, route, w_gate, w_up, w_down, layer, *, name):
    slots, src_row, block_expert, n_used = _moe_plan(route)
    block_expert = block_expert + layer * N_EXPERTS
    xs = _gather_rows(h, src_row, name=name + "_dispatch")
    act = _gmm_gated(xs, w_gate, w_up, block_expert, n_used, tm=MOE_TM, tn=1024, name=name + "_up")
    split = MOE_TM // MOE_DOWN_TM
    ys = _gmm(act, w_down, jnp.repeat(block_expert, split), n_used * split, tm=MOE_DOWN_TM, tn=512, n_out=D,
              out_dtype=F32, name=name + "_down")
    return _gather_rows(ys, slots, name=name + "_combine")


def _mixer_conv(h, cache, w_in, conv_w, w_out):
    bcz = _dense(h, w_in, tn=DENSE_TN, out_dtype=F32, name="a_in")
    y, st_p, st_s = _short_conv(bcz, cache, conv_w, name="a_conv")
    out = _dense(y, w_out, tn=DENSE_TN, out_dtype=F32, name="a_out")
    return out, (st_p, st_s)


def _to_residues(a, dil):
    if dil == 1:
        return a
    w = a.shape[1]
    p = a[:ROWS_P].reshape(N_P, T_P // dil, dil, w).swapaxes(1, 2).reshape(ROWS_P, w)
    return jnp.concatenate([p, a[ROWS_P:]], axis=0)


def _from_residues(a, dil):
    if dil == 1:
        return a
    w = a.shape[1]
    p = a[:ROWS_P].reshape(N_P, dil, T_P // dil, w).swapaxes(1, 2).reshape(ROWS_P, w)
    return jnp.concatenate([p, a[ROWS_P:]], axis=0)


def _mixer_dilated(h, caches, w_qkv, w_o, rel_bias):
    outs, lses, rows_p, rows_s = [], [], [], []
    for gi, (win, dil) in enumerate(DIL_CFG):
        heads = slice(gi * B_HG, (gi + 1) * B_HG)
        qkv = _dense(_to_residues(h, dil), w_qkv, tn=GROUP_W, out_dtype=F32, n_out=3 * GROUP_W,
                     col_map=lambda j, gi=gi: N_DIL * j + gi, name=f"b_qkv{gi}")
        o, lse = _banded_attention(qkv, _band_bias(rel_bias, heads, dil, win // dil),
                                   n_seq=N_P * dil, q_col=0, k_col=1, v_col=2, qw=GROUP_W, kw=GROUP_W,
                                   out_dtype=F32, group=1, with_lse=True, name=f"b_attn_prompt{gi}")
        bias = _sample_bias(rel_bias, heads, win, dil, win).reshape(B_HG * T_S, win + BLK)
        bias_c, bias_n = bias[:, :win], bias[:, win:]
        cache = caches[gi]
        if dil >= T_S:
            cache = cache.reshape(N_S, win // dil, dil, 2 * GROUP_W)[:, :, :T_S]
            bias_c = bias_c.reshape(B_HG * T_S, win // dil, dil)[:, :, :T_S].reshape(B_HG * T_S, -1)
        cache = cache.reshape(N_S, -1, 2 * GROUP_W)
        o, lse = _dil_sample_attention(qkv, cache, bias_c, bias_n, o, lse, name=f"b_attn_sample{gi}")
        outs.append(_from_residues(o, dil))
        lses.append(_from_residues(lse, dil))
        keep = min(win, T_P)
        seq, tail = T_P // dil, keep // dil
        kv = jnp.stack([qkv[(s + 1) * seq - tail:(s + 1) * seq, GROUP_W:] for s in range(N_P * dil)])
        kv = kv.reshape(N_P, dil, tail, 2, GROUP_W)
        rows_p.append(kv.swapaxes(1, 2).reshape(1, N_P, keep, 2, B_HG, HEAD_DIM))
        rows_s.append(qkv[ROWS_P:].reshape(N_S, T_S, 3, GROUP_W)[:, :, 1:]
                      .reshape(1, N_S, T_S, 2, B_HG, HEAD_DIM))
    merged = _merge_dilations(outs, lses, name="b_merge")
    out = _dense(merged, w_o, tn=DENSE_TN, out_dtype=F32, name="b_out")
    return out, (rows_p, rows_s)


def _mixer_swa(h, cache, w_qkv, sink, w_o, rel_bias):
    qkv = _dense(h, w_qkv, tn=DENSE_TN, out_dtype=F32, name="c_qkv")
    heads = slice(0, N_HEADS)
    o = _banded_attention(qkv, _band_bias(rel_bias, heads, 1, C_WINDOW - 1), sink.astype(F32),
                          n_seq=N_P, q_col=0, k_col=ATTN_W // C_KW, v_col=ATTN_W // C_KW + 1, qw=ATTN_W, kw=C_KW,
                          out_dtype=BF16, group=C_GROUP, with_lse=False, name="c_attn_prompt")
    bias = _sample_bias(rel_bias, heads, C_WINDOW, 1, C_WINDOW - 1)
    bias = bias.reshape(C_KV_HEADS, C_GROUP * T_S, 2 * BLK)
    sink_tab = jnp.repeat(sink.astype(F32).reshape(C_KV_HEADS, C_GROUP), T_S, axis=1)[:, :, None]
    o = _swa_sample_attention(qkv, cache.reshape(N_S, C_WINDOW, 2 * C_KW), bias[:, :, :BLK], bias[:, :, BLK:],
                              sink_tab, o, name="c_attn_sample")
    kv_p = jnp.stack([qkv[(n + 1) * T_P - C_WINDOW:(n + 1) * T_P, ATTN_W:] for n in range(N_P)])
    kv_p = kv_p.reshape(1, N_P, C_WINDOW, 2, C_KV_HEADS, HEAD_DIM)
    kv_s = qkv[ROWS_P:, ATTN_W:].reshape(1, N_S, T_S, 2, C_KV_HEADS, HEAD_DIM)
    out = _dense(o, w_o, tn=DENSE_TN, out_dtype=F32, name="c_out")
    return out, (kv_p, kv_s)


def _mixer_sgu(h, w_uv, w_s, b_s, w_out):
    uv = _dense(h, w_uv, tn=DENSE_TN, out_dtype=F32, act="gelu", name="d_uv")
    y, vn_s = _spatial_gate(uv, w_s, b_s, name="d_gate")
    out = _dense(y, w_out, tn=DENSE_TN, out_dtype=F32, name="d_out")
    return out, vn_s.reshape(1, N_S, T_S, D)


def _modulations(c_prompt, c_sample, ada_w, ada_b):
    n_sub = 2 * DEPTH
    pad_rows = 48
    cond = jax.nn.silu(jnp.concatenate([c_prompt, c_sample], axis=0))
    cond = jnp.concatenate([cond, jnp.zeros((pad_rows - N_P - N_S, D), F32)], axis=0).astype(BF16)
    x = jnp.tile(cond, (n_sub, 1))
    mod = _gmm(x, ada_w.reshape(n_sub, D, 3 * D), jnp.arange(n_sub, dtype=jnp.int32),
               jnp.full((1,), n_sub, jnp.int32), tm=pad_rows, tn=512, n_out=3 * D, out_dtype=F32, name="ada")
    mod = mod.reshape(n_sub, pad_rows, 3 * D) + ada_b.reshape(n_sub, 1, 3 * D)
    res = []
    for s in range(n_sub):
        modp = mod[s, :N_P].reshape(N_P, 1, 3 * D)
        mods = jnp.repeat(mod[s, N_P:N_P + N_S], T_S, axis=0)
        res.append((modp, mods))
    return res


def kernel(x_prompt, x_sample, c_prompt, c_sample, cache_a_conv, cache_b_kv0, cache_b_kv1, cache_b_kv2, cache_c_kv, norm_g, ada_w, ada_b, rel_bias, a_w_in, a_conv_w, a_w_out, b_w_qkv, b_w_o, c_w_qkv, c_sink, c_w_o, d_w_uv, d_w_s, d_b_s, d_w_out, f_w_gate, f_w_up, f_w_down, m_w_router, m_b_router, m_w_gate, m_w_up, m_w_down):
    mods = _modulations(c_prompt, c_sample, ada_w, ada_b)
    n_moe = m_w_gate.shape[0]
    mw_gate = m_w_gate.reshape((n_moe * N_EXPERTS,) + m_w_gate.shape[2:])
    mw_up = m_w_up.reshape((n_moe * N_EXPERTS,) + m_w_up.shape[2:])
    mw_down = m_w_down.reshape((n_moe * N_EXPERTS,) + m_w_down.shape[2:])

    def sub(i, j):
        return norm_g[i, 2 * j], norm_g[i, 2 * j + 1], mods[2 * i + j]

    def router(fi):
        wr = jnp.zeros((D, LANE), F32).at[:, :N_EXPERTS].set(m_w_router[fi])
        br = jnp.zeros((1, LANE), F32).at[0, :N_EXPERTS].set(m_b_router[fi])
        return wr, br

    x = (x_prompt.reshape(ROWS_P, D), x_sample.reshape(ROWS_S, D))
    g_in, _, (modp, modsr) = sub(0, 0)
    (h,) = _edge(x, nrm=(g_in, modp, modsr), name="norm_mix0")
    state = {}
    for i in range(DEPTH):
        kind = i % 4
        if kind == 0:
            out, state["a"] = _mixer_conv(h, cache_a_conv[0], a_w_in, a_conv_w[0], a_w_out)
        elif kind == 1:
            out, state["b"] = _mixer_dilated(h, (cache_b_kv0[0], cache_b_kv1[0], cache_b_kv2[0]),
                                             b_w_qkv, b_w_o, rel_bias)
        elif kind == 2:
            out, state["c"] = _mixer_swa(h, cache_c_kv[0], c_w_qkv, c_sink[0], c_w_o, rel_bias)
        else:
            out, state["d"] = _mixer_sgu(h, d_w_uv, d_w_s[0], d_b_s[0], d_w_out)

        _, g_out, (modp, modsr) = sub(i, 0)
        g_in, _, (modp2, modsr2) = sub(i, 1)
        fi = i // 2
        moe = i % 2 == 1
        res = _edge(x, y=out, res=(g_out, modp, modsr), nrm=(g_in, modp2, modsr2),
                    router=router(fi) if moe else None, name=f"edge_mix{i}")
        x, h = res[0], res[1]
        if moe:
            route = res[2]
            y = _moe(h, route, mw_gate, mw_up, mw_down, fi, name=f"moe{i}")
        else:
            route = None
            be, nu = _single(ROWS, DENSE_TM, fi)
            act = _gmm_gated(h, f_w_gate, f_w_up, be, nu, tm=DENSE_TM, tn=512, name=f"ffn_up{i}")
            y = _dense(act, f_w_down, tn=512, out_dtype=F32, name=f"ffn_down{i}", which=fi)

        _, g_out, (modp, modsr) = sub(i, 1)
        if i + 1 < DEPTH:
            g_in, _, (modp2, modsr2) = sub(i + 1, 0)
            x, h = _edge(x, y=y, route_in=route, res=(g_out, modp, modsr), nrm=(g_in, modp2, modsr2),
                         name=f"edge_ffn{i}")
        else:
            y_prompt, y_sample = _edge(x, y=y, route_in=route, res=(g_out, modp, modsr), split_out=True,
                                       name=f"edge_ffn{i}")

    (a_p, a_s) = state["a"]
    rows_p, rows_s = state["b"]
    c_p, c_s = state["c"]
    return (y_prompt.reshape(N_P, T_P, D), y_sample.reshape(N_S, T_S, D), a_p[None], a_s[None],
            rows_p[0], rows_s[0], rows_p[1], rows_s[1], rows_p[2], rows_s[2],
            c_p, c_s, state["d"])
```

```python
import functools
import math

import numpy as np
import jax
import jax.numpy as jnp
from jax import lax
from jax.experimental import pallas as pl
from jax.experimental.pallas import tpu as pltpu

D = 2048
N_P, T_P = 4, 2048
N_S, T_S = 32, 8
ROWS_P = N_P * T_P
ROWS_S = N_S * T_S
ROWS = ROWS_P + ROWS_S
DEPTH = 4
HEAD_DIM = 64
N_HEADS = 24
ATTN_W = N_HEADS * HEAD_DIM
DIL_CFG = ((128, 1), (512, 4), (2048, 16))
N_DIL = len(DIL_CFG)
B_HG = 8
GROUP_W = B_HG * HEAD_DIM
C_WINDOW = 128
C_KV_HEADS = 4
C_GROUP = 6
C_KW = C_KV_HEADS * HEAD_DIM
BLK = 128
SG_GROUPS = 8
SG_GW = D // SG_GROUPS
REL_BUCKETS = 32
REL_MAX_DIST = 2048
N_EXPERTS = 8
RMS_EPS = 1e-6
LN_EPS = 1e-5
NEG = -1e30

LANE = 128
VMEM_LIMIT_CAP = 56 * 1024 * 1024
VMEM_LIMIT_FLOOR = 32 * 1024 * 1024
VMEM_MARGIN = 8 * 1024 * 1024

ROW_BLK = 256
N_ROW_BLKS = ROWS // ROW_BLK
N_PROMPT_BLKS = ROWS_P // ROW_BLK
BLKS_PER_SEQ = T_P // ROW_BLK
SAMPLE_BLK8 = ROWS_P // T_S
DENSE_TM = 768
DENSE_TN = 1024
MOE_TM = 256
MOE_DOWN_TM = 256
GATHER_CHUNK = 512
GATHER_UNROLL = 8

BF16 = jnp.bfloat16
F32 = jnp.float32


def _params(sem, nbytes):
    limit = int(min(max(nbytes + VMEM_MARGIN, VMEM_LIMIT_FLOOR), VMEM_LIMIT_CAP))
    return pltpu.CompilerParams(dimension_semantics=sem, vmem_limit_bytes=limit)


def _weight_changed(be_ref):
    b = pl.program_id(1)
    prev = be_ref[jnp.maximum(b - 1, 0)]
    return jnp.logical_or(b == 0, be_ref[b] != prev)


def _gmm_kernel(be_ref, nu_ref, x_ref, w_ref, o_ref, wc_ref, *, act):
    used = pl.program_id(1) < nu_ref[0]

    @pl.when(used)
    def _():
        @pl.when(_weight_changed(be_ref))
        def _():
            wc_ref[...] = w_ref[...].astype(BF16)

        acc = jnp.dot(x_ref[...].astype(BF16), wc_ref[...], preferred_element_type=F32)
        if act == "gelu":
            acc = jax.nn.gelu(acc)
        o_ref[...] = acc.astype(o_ref.dtype)

    @pl.when(jnp.logical_not(used))
    def _():
        o_ref[...] = jnp.zeros_like(o_ref)


def _gmm_gated_kernel(be_ref, nu_ref, x_ref, wg_ref, wu_ref, o_ref, wgc_ref, wuc_ref):
    used = pl.program_id(1) < nu_ref[0]

    @pl.when(used)
    def _():
        @pl.when(_weight_changed(be_ref))
        def _():
            wgc_ref[...] = wg_ref[...].astype(BF16)
            wuc_ref[...] = wu_ref[...].astype(BF16)

        x = x_ref[...].astype(BF16)
        g = jnp.dot(x, wgc_ref[...], preferred_element_type=F32)
        u = jnp.dot(x, wuc_ref[...], preferred_element_type=F32)
        o_ref[...] = (jax.nn.silu(g) * u).astype(o_ref.dtype)

    @pl.when(jnp.logical_not(used))
    def _():
        o_ref[...] = jnp.zeros_like(o_ref)


def _gmm_specs(tm, k, tn, n_w, col_map):
    def x_map(j, b, be, nu):
        return (jnp.minimum(b, nu[0] - 1), 0)

    def w_map(j, b, be, nu):
        return (be[jnp.minimum(b, nu[0] - 1)], 0, col_map(j))

    in_specs = [pl.BlockSpec((tm, k), x_map)] + [pl.BlockSpec((None, k, tn), w_map)] * n_w
    out_spec = pl.BlockSpec((tm, tn), lambda j, b, be, nu: (b, j))
    return in_specs, out_spec


def _gmm(x, w, block_expert, n_used, *, tm, tn, n_out, out_dtype, act=None, col_map=lambda j: j, name):
    m, k = x.shape
    assert k == w.shape[1] and m % tm == 0 and n_out % tn == 0
    in_specs, out_spec = _gmm_specs(tm, k, tn, 1, col_map)
    nbytes = (2 * tm * k * x.dtype.itemsize + 2 * k * tn * 4 + k * tn * 2
              + 2 * tm * tn * jnp.dtype(out_dtype).itemsize)
    return pl.pallas_call(
        functools.partial(_gmm_kernel, act=act),
        out_shape=jax.ShapeDtypeStruct((m, n_out), out_dtype),
        grid_spec=pltpu.PrefetchScalarGridSpec(
            num_scalar_prefetch=2,
            grid=(n_out // tn, m // tm),
            in_specs=in_specs,
            out_specs=out_spec,
            scratch_shapes=[pltpu.VMEM((k, tn), BF16)],
        ),
        compiler_params=_params(("arbitrary", "arbitrary"), nbytes),
        name=name,
    )(block_expert, n_used, x, w)


def _gmm_gated(x, wg, wu, block_expert, n_used, *, tm, tn, name):
    m, k = x.shape
    n = wg.shape[2]
    assert k == wg.shape[1] and m % tm == 0 and n % tn == 0 and wu.shape == wg.shape
    in_specs, out_spec = _gmm_specs(tm, k, tn, 2, lambda j: j)
    nbytes = 2 * tm * k * x.dtype.itemsize + 4 * k * tn * 4 + 2 * k * tn * 2 + 2 * tm * tn * 2
    return pl.pallas_call(
        _gmm_gated_kernel,
        out_shape=jax.ShapeDtypeStruct((m, n), BF16),
        grid_spec=pltpu.PrefetchScalarGridSpec(
            num_scalar_prefetch=2,
            grid=(n // tn, m // tm),
            in_specs=in_specs,
            out_specs=out_spec,
            scratch_shapes=[pltpu.VMEM((k, tn), BF16), pltpu.VMEM((k, tn), BF16)],
        ),
        compiler_params=_params(("arbitrary", "arbitrary"), nbytes),
        name=name,
    )(block_expert, n_used, x, wg, wu)


def _single(m, tm, which=0):
    return jnp.full((m // tm,), which, jnp.int32), jnp.full((1,), m // tm, jnp.int32)


def _dense(x, w, *, tn, out_dtype, act=None, name, tm=DENSE_TM, which=0, n_out=None, col_map=lambda j: j):
    be, nu = _single(x.shape[0], tm, which)
    return _gmm(x, w, be, nu, tm=tm, tn=tn, n_out=n_out or w.shape[2], out_dtype=out_dtype, act=act,
                col_map=col_map, name=name)


def _rms(x, g):
    return x * lax.rsqrt(jnp.mean(x * x, axis=-1, keepdims=True) + RMS_EPS) * g


def _route(h, wr_ref, br_ref):
    logits = jnp.dot(h, wr_ref[...], preferred_element_type=F32, precision=lax.Precision.HIGHEST) + br_ref[...]
    lane = lax.broadcasted_iota(jnp.int32, logits.shape, 1)
    logits = jnp.where(lane < N_EXPERTS, logits, -jnp.inf)
    m1 = jnp.max(logits, axis=-1, keepdims=True)
    i1 = jnp.min(jnp.where(logits == m1, lane, LANE), axis=-1, keepdims=True)
    rest = jnp.where(lane == i1, -jnp.inf, logits)
    m2 = jnp.max(rest, axis=-1, keepdims=True)
    i2 = jnp.min(jnp.where(rest == m2, lane, LANE), axis=-1, keepdims=True)
    e2 = jnp.exp(m2 - m1)
    g1 = 1.0 / (1.0 + e2)
    g2 = e2 / (1.0 + e2)
    out = jnp.where(lane == 0, i1.astype(F32),
                    jnp.where(lane == 1, i2.astype(F32),
                              jnp.where(lane == 2, g1, jnp.where(lane == 3, g2, 0.0))))
    return out


def _edge_kernel(*refs, split_in, resid, moe_in, norm, route, split_out):
    refs = list(refs)

    def pop(n):
        return [refs.pop(0) for _ in range(n)]

    x_refs = pop(2) if split_in else pop(1) * 2
    if resid:
        y_refs = pop(3) if moe_in else pop(1)
        gr_ref, gate_p, gate_s = pop(3)
    if norm:
        gn_ref, shift_p, scale_p, shift_s, scale_s = pop(5)
        if route:
            wr_ref, br_ref = pop(2)
    if resid:
        xo_refs = pop(2) if split_out else pop(1) * 2
    if norm:
        (h_ref,) = pop(1)
        if route:
            (r_ref,) = pop(1)

    def run(sample):
        x = x_refs[sample][...]
        if resid:
            if moe_in:
                ya_ref, yb_ref, rp_ref = y_refs
                rp = rp_ref[...]
                out = rp[:, 2:3] * ya_ref[...] + rp[:, 3:4] * yb_ref[...]
            else:
                out = y_refs[0][...]
            x = x + (gate_s if sample else gate_p)[...] * _rms(out, gr_ref[...])
            xo_refs[sample][...] = x
        if norm:
            shift = (shift_s if sample else shift_p)[...]
            scale = (scale_s if sample else scale_p)[...]
            h = _rms(x, gn_ref[...]) * (1.0 + scale) + shift
            h_ref[...] = h.astype(h_ref.dtype)
            if route:
                r_ref[...] = _route(h, wr_ref, br_ref)

    i = pl.program_id(0)
    pl.when(i < N_PROMPT_BLKS)(functools.partial(run, 0))
    pl.when(i == N_PROMPT_BLKS)(functools.partial(run, 1))


def _mod_specs(cols):
    specs = []
    for c in cols:
        specs.append(pl.BlockSpec((None, 1, D), lambda i, c=c: (jnp.minimum(i // BLKS_PER_SEQ, N_P - 1), 0, c)))
    for c in cols:
        specs.append(pl.BlockSpec((ROW_BLK, D), lambda i, c=c: (0, c)))
    return specs


def _edge(x, *, y=None, route_in=None, res=None, nrm=None, router=None, split_out=False, name):
    split_in = isinstance(x, tuple)
    resid, norm, moe_in, route = res is not None, nrm is not None, route_in is not None, router is not None
    row = pl.BlockSpec((ROW_BLK, D), lambda i: (i, 0))
    row_p = pl.BlockSpec((ROW_BLK, D), lambda i: (jnp.minimum(i, N_PROMPT_BLKS - 1), 0))
    row_s = pl.BlockSpec((ROW_BLK, D), lambda i: (0, 0))
    gain = pl.BlockSpec((1, D), lambda i: (0, 0))
    lane_row = pl.BlockSpec((ROW_BLK, LANE), lambda i: (i, 0))
    in_specs = [row_p, row_s] if split_in else [row]
    args = list(x) if split_in else [x]
    out_shape, out_specs = [], []
    if resid:
        g, modp, mods = res
        if moe_in:
            in_specs += [row, pl.BlockSpec((ROW_BLK, D), lambda i: (i + N_ROW_BLKS, 0)), lane_row]
            args += [y, y, route_in]
        else:
            in_specs += [row]
            args += [y]
        in_specs += [gain] + _mod_specs((2,))
        args += [g.reshape(1, D), modp, mods]
        if split_out:
            out_shape += [jax.ShapeDtypeStruct((ROWS_P, D), F32), jax.ShapeDtypeStruct((ROWS_S, D), F32)]
            out_specs += [row_p, row_s]
        else:
            out_shape += [jax.ShapeDtypeStruct((ROWS, D), F32)]
            out_specs += [row]
    if norm:
        g, modp, mods = nrm
        in_specs += [gain] + _mod_specs((0, 1))
        args += [g.reshape(1, D), modp, modp, mods, mods]
        if route:
            in_specs += [pl.BlockSpec((D, LANE), lambda i: (0, 0)), pl.BlockSpec((1, LANE), lambda i: (0, 0))]
            args += list(router)
        out_shape += [jax.ShapeDtypeStruct((ROWS, D), BF16)]
        out_specs += [row]
        if route:
            out_shape += [jax.ShapeDtypeStruct((ROWS, LANE), F32)]
            out_specs += [lane_row]
    n_blocks = len(in_specs) + len(out_specs)
    return pl.pallas_call(
        functools.partial(_edge_kernel, split_in=split_in, resid=resid, moe_in=moe_in, norm=norm, route=route,
                          split_out=split_out),
        out_shape=out_shape,
        grid=(N_ROW_BLKS,),
        in_specs=in_specs,
        out_specs=out_specs,
        compiler_params=_params(("arbitrary",), n_blocks * ROW_BLK * D * 4 * 2),
        name=name,
    )(*args)


def _conv_kernel(gb_ref, gc_ref, z_ref, cw_ref, b1_ref, b2_ref, y_ref, stp_ref, us_ref, tail_ref):
    i = pl.program_id(0)
    u = gc_ref[...] * z_ref[...]
    r = lax.broadcasted_iota(jnp.int32, u.shape, 0)
    roll1 = pltpu.roll(u, 1, axis=0)
    roll2 = pltpu.roll(u, 2, axis=0)

    def emit(um1, um2):
        conv = cw_ref[0:1, :] * um2 + cw_ref[1:2, :] * um1 + cw_ref[2:3, :] * u
        y_ref[...] = (gb_ref[...] * conv).astype(y_ref.dtype)

    @pl.when(i < N_PROMPT_BLKS)
    def _():
        @pl.when(i % BLKS_PER_SEQ == 0)
        def _():
            tail_ref[...] = jnp.zeros_like(tail_ref)

        prev2 = tail_ref[6:7, :]
        prev1 = tail_ref[7:8, :]
        emit(jnp.where(r == 0, prev1, roll1),
             jnp.where(r == 0, prev2, jnp.where(r == 1, prev1, roll2)))
        tail_ref[...] = u[ROW_BLK - 8:ROW_BLK, :]
        stp_ref[...] = u[ROW_BLK - 2:ROW_BLK, :]

    @pl.when(i == N_PROMPT_BLKS)
    def _():
        t = r % T_S
        emit(jnp.where(t == 0, b1_ref[...], roll1), jnp.where(t < 2, b2_ref[...], roll2))
        us_ref[...] = u


def _short_conv(bcz, cache, conv_w, *, name):
    zeros = jnp.zeros((N_S, T_S, D), F32)
    b1 = zeros.at[:, 0].set(cache[:, 1]).reshape(ROWS_S, D)
    b2 = zeros.at[:, 0].set(cache[:, 0]).at[:, 1].set(cache[:, 1]).reshape(ROWS_S, D)

    def col(c):
        return pl.BlockSpec((ROW_BLK, D), lambda i, c=c: (i, c))

    const = pl.BlockSpec((ROW_BLK, D), lambda i: (0, 0))
    y, st_p, u_s = pl.pallas_call(
        _conv_kernel,
        out_shape=[jax.ShapeDtypeStruct((ROWS, D), BF16), jax.ShapeDtypeStruct((N_P, 2, D), F32),
                   jax.ShapeDtypeStruct((ROWS_S, D), F32)],
        grid=(N_ROW_BLKS,),
        in_specs=[col(0), col(1), col(2), pl.BlockSpec((3, D), lambda i: (0, 0)), const, const],
        out_specs=[pl.BlockSpec((ROW_BLK, D), lambda i: (i, 0)),
                   pl.BlockSpec((None, 2, D), lambda i: (jnp.minimum(i // BLKS_PER_SEQ, N_P - 1), 0, 0)),
                   const],
        scratch_shapes=[pltpu.VMEM((8, D), F32)],
        compiler_params=_params(("arbitrary",), ROW_BLK * D * 4 * 7 * 2),
        name=name,
    )(bcz, bcz, bcz, conv_w, b1, b2)
    return y, st_p, u_s.reshape(N_S, T_S, D)[:, T_S - 2:]


def _rel_bucket(dist):
    n = np.maximum(np.asarray(dist), 0)
    exact = REL_BUCKETS // 2
    ratio = np.log(np.maximum(n, 1).astype(np.float32) / exact) / math.log(REL_MAX_DIST / exact)
    large = np.minimum(exact + (ratio * (REL_BUCKETS - exact)).astype(np.int32), REL_BUCKETS - 1)
    return np.where(n < exact, n, large).astype(np.int32)


def _toeplitz(a, n_rows, n_cols):
    h, p = a.shape
    assert n_cols < p
    t = jnp.tile(a, (1, n_rows))[:, :n_rows * (p - 1)].reshape(h, n_rows, p - 1)
    return t[:, :, :n_cols]


def _bias_line(rel_bias, heads, dist, valid):
    tab = rel_bias[_rel_bucket(dist)][:, heads]
    return jnp.where(valid[:, None], tab, NEG).T.astype(F32)


def _band_bias(rel_bias, heads, dil, max_dist):
    p = 3 * BLK
    m = np.arange(p)
    dist = np.where(m < 2 * BLK, BLK - m, BLK + p - m)
    valid = (dist >= 0) & (dist <= max_dist)
    return _toeplitz(_bias_line(rel_bias, heads, dist * dil, valid), BLK, 2 * BLK)


def _sample_bias(rel_bias, heads, win, dil, max_dist):
    lk = win + BLK
    p = lk + T_S
    m = np.arange(p)
    dist = np.where(m < lk, win - m, win + p - m)
    valid = (dist >= 0) & (dist % dil == 0) & (dist <= max_dist)
    return _toeplitz(_bias_line(rel_bias, heads, dist, valid), T_S, lk)


def _softmax_pv(s, v, sink=None):
    m = jnp.max(s, axis=-1, keepdims=True)
    if sink is not None:
        m = jnp.maximum(m, sink)
    p = jnp.exp(s - m)
    l = jnp.sum(p, axis=-1, keepdims=True)
    if sink is not None:
        l = l + jnp.exp(sink - m)
    o = jnp.dot(p.astype(BF16), v, preferred_element_type=F32) / l
    return o, m + jnp.log(l)


def _nt_dot(a, b):
    return lax.dot_general(a, b, (((1,), (1,)), ((), ())), preferred_element_type=F32)


def _banded_kernel(*refs, n_seq, **kw):
    out_refs = refs[-2:] if kw["with_lse"] else refs[-1:]

    @pl.when(pl.program_id(0) < n_seq)
    def _():
        _banded_body(*refs, **kw)

    @pl.when(pl.program_id(0) >= n_seq)
    def _():
        for ref in out_refs:
            ref[...] = jnp.zeros_like(ref)


def _banded_body(*refs, n_heads, group, with_lse, has_sink):
    refs = list(refs)
    sink_ref = refs.pop(0) if has_sink else None
    q_ref, kp_ref, kc_ref, vp_ref, vc_ref, bias_ref = refs[:6]
    o_ref = refs[6]
    lse_ref = refs[7] if with_lse else None
    qb = pl.program_id(1)
    q = (q_ref[...] * (HEAD_DIM ** -0.5)).astype(BF16)
    k = jnp.concatenate([kp_ref[...], kc_ref[...]], axis=0).astype(BF16)
    v = jnp.concatenate([vp_ref[...], vc_ref[...]], axis=0).astype(BF16)
    col = lax.broadcasted_iota(jnp.int32, (BLK, 2 * BLK), 1)
    no_prev = jnp.logical_and(qb == 0, col < BLK)
    for h in range(n_heads):
        hk = h // group
        qs = slice(h * HEAD_DIM, (h + 1) * HEAD_DIM)
        ks = slice(hk * HEAD_DIM, (hk + 1) * HEAD_DIM)
        s = _nt_dot(q[:, qs], k[:, ks]) + bias_ref[h]
        s = jnp.where(no_prev, NEG, s)
        sink = sink_ref[h] if has_sink else None
        o, lse = _softmax_pv(s, v[:, ks], sink)
        o_ref[:, qs] = o.astype(o_ref.dtype)
        if with_lse:
            lse_ref[:, qs] = jnp.broadcast_to(lse, (BLK, HEAD_DIM))


def _banded_attention(src, bias, sink=None, *, n_seq, q_col, k_col, v_col, qw, kw, out_dtype, group, with_lse, name):
    n_heads = qw // HEAD_DIM
    has_sink = sink is not None
    n_blk = ROWS_P // BLK // n_seq
    fill_row0 = ROWS_P // BLK
    fill_blks = ROWS_S // BLK
    n_fill_seq = -(-fill_blks // n_blk)

    def cur(col):
        return lambda s, b: (jnp.minimum(s, n_seq - 1) * n_blk + b, col)

    def prev(col):
        return lambda s, b: (jnp.minimum(s, n_seq - 1) * n_blk + jnp.maximum(b - 1, 0), col)

    def out_map(s, b):
        fill = jnp.minimum((s - n_seq) * n_blk + b, fill_blks - 1)
        return (jnp.where(s >= n_seq, fill_row0 + fill, s * n_blk + b), 0)

    in_specs = [pl.BlockSpec((BLK, qw), cur(q_col)),
                pl.BlockSpec((BLK, kw), prev(k_col)), pl.BlockSpec((BLK, kw), cur(k_col)),
                pl.BlockSpec((BLK, kw), prev(v_col)), pl.BlockSpec((BLK, kw), cur(v_col)),
                pl.BlockSpec((n_heads, BLK, 2 * BLK), lambda s, b: (0, 0, 0))]
    args = [src, src, src, src, src, bias]
    if has_sink:
        in_specs = [pl.BlockSpec(memory_space=pltpu.SMEM)] + in_specs
        args = [sink] + args
    out_shapes = [jax.ShapeDtypeStruct((ROWS, qw), out_dtype)]
    out_specs = [pl.BlockSpec((BLK, qw), out_map)]
    if with_lse:
        out_shapes.append(jax.ShapeDtypeStruct((ROWS, qw), F32))
        out_specs.append(pl.BlockSpec((BLK, qw), out_map))
    nbytes = (BLK * qw * 4 * 3 + BLK * kw * 4 * 4) * 2 + n_heads * BLK * 2 * BLK * 4 * 2
    res = pl.pallas_call(
        functools.partial(_banded_kernel, n_seq=n_seq, n_heads=n_heads, group=group, with_lse=with_lse,
                          has_sink=has_sink),
        out_shape=out_shapes,
        grid=(n_seq + n_fill_seq, n_blk),
        in_specs=in_specs,
        out_specs=out_specs,
        compiler_params=_params(("arbitrary", "arbitrary"), nbytes),
        name=name,
    )(*args)
    return res if with_lse else res[0]


def _dil_sample_kernel(q_ref, kn_ref, vn_ref, cache_ref, bc_ref, bn_ref, o_in, lse_in, o_ref, lse_ref):
    del o_in, lse_in
    q = q_ref[...] * (HEAD_DIM ** -0.5)
    qt = jnp.concatenate([q] * B_HG, axis=0)
    row_h = lax.broadcasted_iota(jnp.int32, qt.shape, 0) // T_S
    col_h = lax.broadcasted_iota(jnp.int32, qt.shape, 1) // HEAD_DIM
    qbd = jnp.where(row_h == col_h, qt, 0.0).astype(BF16)
    pad = jnp.zeros((BLK - T_S, GROUP_W), F32)
    kn = jnp.concatenate([kn_ref[...], pad], axis=0).astype(BF16)
    vn = jnp.concatenate([vn_ref[...], pad], axis=0).astype(BF16)
    kc = cache_ref[:, 0:GROUP_W].astype(BF16)
    vc = cache_ref[:, GROUP_W:2 * GROUP_W].astype(BF16)
    sc = _nt_dot(qbd, kc) + bc_ref[...]
    sn = _nt_dot(qbd, kn) + bn_ref[...]
    m = jnp.maximum(jnp.max(sc, axis=-1, keepdims=True), jnp.max(sn, axis=-1, keepdims=True))
    pc = jnp.exp(sc - m)
    pn = jnp.exp(sn - m)
    l = jnp.sum(pc, axis=-1, keepdims=True) + jnp.sum(pn, axis=-1, keepdims=True)
    big = (jnp.dot(pc.astype(BF16), vc, preferred_element_type=F32)
           + jnp.dot(pn.astype(BF16), vn, preferred_element_type=F32)) / l
    lse = m + jnp.log(l)
    for h in range(B_HG):
        rs = slice(h * T_S, (h + 1) * T_S)
        cs = slice(h * HEAD_DIM, (h + 1) * HEAD_DIM)
        o_ref[:, cs] = big[rs, cs]
        lse_ref[:, cs] = jnp.broadcast_to(lse[rs], (T_S, HEAD_DIM))


def _dil_sample_attention(qkv, cache, bias_c, bias_n, o, lse, *, name):
    lb = cache.shape[1]
    row = lambda c: pl.BlockSpec((T_S, GROUP_W), lambda n, c=c: (SAMPLE_BLK8 + n, c))
    const = lambda n: (0, 0)
    anyspec = pl.BlockSpec(memory_space=pl.ANY)
    nbytes = lb * 2 * GROUP_W * 4 * 2 + 64 * lb * 4 * 6
    return pl.pallas_call(
        _dil_sample_kernel,
        out_shape=[jax.ShapeDtypeStruct((ROWS, GROUP_W), F32)] * 2,
        grid=(N_S,),
        in_specs=[row(0), row(1), row(2),
                  pl.BlockSpec((None, lb, 2 * GROUP_W), lambda n: (n, 0, 0)),
                  pl.BlockSpec((B_HG * T_S, lb), const), pl.BlockSpec((B_HG * T_S, BLK), const),
                  anyspec, anyspec],
        out_specs=[row(0)] * 2,
        input_output_aliases={6: 0, 7: 1},
        compiler_params=_params(("parallel",), nbytes),
        name=name,
    )(qkv, qkv, qkv, cache, bias_c, bias_n, o, lse)


def _merge_kernel(o0, o1, o2, l0, l1, l2, out_ref):
    ls = [l0[...], l1[...], l2[...]]
    m = jnp.maximum(jnp.maximum(ls[0], ls[1]), ls[2])
    es = [jnp.exp(l - m) for l in ls]
    tot = es[0] + es[1] + es[2]
    for g, o in enumerate((o0, o1, o2)):
        out_ref[:, g * GROUP_W:(g + 1) * GROUP_W] = (o[...] * (es[g] / tot)).astype(out_ref.dtype)


def _merge_dilations(outs, lses, *, name):
    spec = pl.BlockSpec((ROW_BLK, GROUP_W), lambda i: (i, 0))
    return pl.pallas_call(
        _merge_kernel,
        out_shape=jax.ShapeDtypeStruct((ROWS, ATTN_W), BF16),
        grid=(N_ROW_BLKS,),
        in_specs=[spec] * 6,
        out_specs=pl.BlockSpec((ROW_BLK, ATTN_W), lambda i: (i, 0)),
        compiler_params=_params(("parallel",), ROW_BLK * GROUP_W * 4 * 8 * 2),
        name=name,
    )(*outs, *lses)


def _swa_sample_kernel(q_ref, kvn_ref, cache_ref, bc_ref, bn_ref, sink_ref, o_in, o_ref):
    del o_in
    q = q_ref[...] * (HEAD_DIM ** -0.5)
    pad = jnp.zeros((BLK - T_S, 2 * C_KW), F32)
    kvn = jnp.concatenate([kvn_ref[...], pad], axis=0).astype(BF16)
    kvc = cache_ref[...].astype(BF16)
    for hk in range(C_KV_HEADS):
        heads = range(hk * C_GROUP, (hk + 1) * C_GROUP)
        qh = jnp.concatenate([q[:, h * HEAD_DIM:(h + 1) * HEAD_DIM] for h in heads], axis=0).astype(BF16)
        ks = slice(hk * HEAD_DIM, (hk + 1) * HEAD_DIM)
        vs = slice(C_KW + hk * HEAD_DIM, C_KW + (hk + 1) * HEAD_DIM)
        sc = _nt_dot(qh, kvc[:, ks]) + bc_ref[hk]
        sn = _nt_dot(qh, kvn[:, ks]) + bn_ref[hk]
        sink = sink_ref[hk]
        m = jnp.maximum(jnp.maximum(jnp.max(sc, axis=-1, keepdims=True),
                                    jnp.max(sn, axis=-1, keepdims=True)), sink)
        pc = jnp.exp(sc - m)
        pn = jnp.exp(sn - m)
        l = jnp.sum(pc, axis=-1, keepdims=True) + jnp.sum(pn, axis=-1, keepdims=True) + jnp.exp(sink - m)
        o = (jnp.dot(pc.astype(BF16), kvc[:, vs], preferred_element_type=F32)
             + jnp.dot(pn.astype(BF16), kvn[:, vs], preferred_element_type=F32)) / l
        for gi, h in enumerate(heads):
            o_ref[:, h * HEAD_DIM:(h + 1) * HEAD_DIM] = o[gi * T_S:(gi + 1) * T_S].astype(o_ref.dtype)


def _swa_sample_attention(qkv, cache, bias_c, bias_n, sink_tab, o, *, name):
    rows = C_GROUP * T_S
    const = lambda n: (0, 0, 0)
    return pl.pallas_call(
        _swa_sample_kernel,
        out_shape=jax.ShapeDtypeStruct((ROWS, ATTN_W), o.dtype),
        grid=(N_S,),
        in_specs=[pl.BlockSpec((T_S, ATTN_W), lambda n: (SAMPLE_BLK8 + n, 0)),
                  pl.BlockSpec((T_S, 2 * C_KW), lambda n: (SAMPLE_BLK8 + n, ATTN_W // (2 * C_KW))),
                  pl.BlockSpec((None, C_WINDOW, 2 * C_KW), lambda n: (n, 0, 0)),
                  pl.BlockSpec((C_KV_HEADS, rows, BLK), const), pl.BlockSpec((C_KV_HEADS, rows, BLK), const),
                  pl.BlockSpec((C_KV_HEADS, rows, 1), const),
                  pl.BlockSpec(memory_space=pl.ANY)],
        out_specs=pl.BlockSpec((T_S, ATTN_W), lambda n: (SAMPLE_BLK8 + n, 0)),
        input_output_aliases={6: 0},
        compiler_params=_params(("parallel",), 1 << 20),
        name=name,
    )(qkv, qkv, cache, bias_c, bias_n, sink_tab, o)


def _sgu_kernel(u_ref, v_ref, w_ref, b_ref, ws_ref, bs_ref, y_ref, vn_ref):
    i = pl.program_id(0)
    v = v_ref[...]
    mu = jnp.mean(v, axis=-1, keepdims=True)
    var = jnp.mean(jnp.square(v - mu), axis=-1, keepdims=True)
    vn = (v - mu) * lax.rsqrt(var + LN_EPS)
    vb = vn.astype(BF16)

    @pl.when(i < N_PROMPT_BLKS)
    def _():
        causal = (lax.broadcasted_iota(jnp.int32, (BLK, BLK), 0) >= lax.broadcasted_iota(jnp.int32, (BLK, BLK), 1))
        for g in range(SG_GROUPS):
            cs = slice(g * SG_GW, (g + 1) * SG_GW)
            wg = jnp.where(causal, w_ref[g], 0.0).astype(BF16)
            for c in range(ROW_BLK // BLK):
                rs = slice(c * BLK, (c + 1) * BLK)
                s = jnp.dot(wg, vb[rs, cs], preferred_element_type=F32) + b_ref[g]
                y_ref[rs, cs] = (u_ref[rs, cs] * s).astype(y_ref.dtype)

    @pl.when(i == N_PROMPT_BLKS)
    def _():
        vn_ref[...] = vn
        for g in range(SG_GROUPS):
            cs = slice(g * SG_GW, (g + 1) * SG_GW)
            s = jnp.dot(ws_ref[g].astype(BF16), vb[:, cs], preferred_element_type=F32) + bs_ref[g]
            y_ref[:, cs] = (u_ref[:, cs] * s).astype(y_ref.dtype)


def _spatial_gate(uv, w_s, b_s, *, name):
    r = np.arange(ROWS_S)
    same_seq = (r[:, None] // T_S) == (r[None, :] // T_S)
    causal = (r[:, None] % T_S) >= (r[None, :] % T_S)
    w8 = jnp.tile(w_s[:, :T_S, :T_S], (1, N_S, N_S))
    ws = jnp.where((same_seq & causal)[None], w8, 0.0)
    bs = jnp.tile(b_s[:, :T_S], (1, N_S)).reshape(SG_GROUPS, ROWS_S, 1)
    const3 = lambda i: (0, 0, 0)
    nbytes = ROW_BLK * D * 4 * 4 * 2 + SG_GROUPS * (BLK * BLK + ROW_BLK * ROW_BLK) * 4 * 2
    return pl.pallas_call(
        _sgu_kernel,
        out_shape=[jax.ShapeDtypeStruct((ROWS, D), BF16), jax.ShapeDtypeStruct((ROWS_S, D), F32)],
        grid=(N_ROW_BLKS,),
        in_specs=[pl.BlockSpec((ROW_BLK, D), lambda i: (i, 0)), pl.BlockSpec((ROW_BLK, D), lambda i: (i, 1)),
                  pl.BlockSpec((SG_GROUPS, BLK, BLK), const3), pl.BlockSpec((SG_GROUPS, BLK, 1), const3),
                  pl.BlockSpec((SG_GROUPS, ROWS_S, ROWS_S), const3), pl.BlockSpec((SG_GROUPS, ROWS_S, 1), const3)],
        out_specs=[pl.BlockSpec((ROW_BLK, D), lambda i: (i, 0)), pl.BlockSpec((ROWS_S, D), lambda i: (0, 0))],
        compiler_params=_params(("arbitrary",), nbytes),
        name=name,
    )(uv, uv, w_s, b_s.reshape(SG_GROUPS, BLK, 1), ws, bs)


def _gather_kernel(idx_ref, src_ref, o_ref, sem):
    base = pl.program_id(0) * GATHER_CHUNK

    def row_copy(r):
        return pltpu.make_async_copy(src_ref.at[idx_ref[base + r]], o_ref.at[r], sem)

    def start(g, carry):
        for u in range(GATHER_UNROLL):
            row_copy(g * GATHER_UNROLL + u).start()
        return carry

    def wait(g, carry):
        for u in range(GATHER_UNROLL):
            row_copy(g * GATHER_UNROLL + u).wait()
        return carry

    lax.fori_loop(0, GATHER_CHUNK // GATHER_UNROLL, start, 0)
    lax.fori_loop(0, GATHER_CHUNK // GATHER_UNROLL, wait, 0)


def _gather_rows(src, idx, *, name):
    n = src.shape[0]
    r = idx.shape[0]
    assert r % GATHER_CHUNK == 0
    slab = (D // LANE, LANE)
    out = pl.pallas_call(
        _gather_kernel,
        out_shape=jax.ShapeDtypeStruct((r,) + slab, src.dtype),
        grid_spec=pltpu.PrefetchScalarGridSpec(
            num_scalar_prefetch=1,
            grid=(r // GATHER_CHUNK,),
            in_specs=[pl.BlockSpec(memory_space=pl.ANY)],
            out_specs=pl.BlockSpec((GATHER_CHUNK,) + slab, lambda c, idx: (c, 0, 0)),
            scratch_shapes=[pltpu.SemaphoreType.DMA],
        ),
        compiler_params=_params(("arbitrary",), 2 * GATHER_CHUNK * D * src.dtype.itemsize),
        name=name,
    )(idx, src.reshape((n,) + slab))
    return out.reshape(r, D)


N_ASSIGN = 2 * ROWS
_MOE_ROWS_MIN = N_ASSIGN + N_EXPERTS * (MOE_TM - 1)
_MOE_UNIT = MOE_TM * GATHER_CHUNK // math.gcd(MOE_TM, GATHER_CHUNK)
MOE_ROWS = -(-_MOE_ROWS_MIN // _MOE_UNIT) * _MOE_UNIT
assert N_ASSIGN % GATHER_CHUNK == 0


def _moe_plan(route):
    e_flat = route[:, 0:2].astype(jnp.int32).reshape(-1)
    onehot = (e_flat[:, None] == jnp.arange(N_EXPERTS, dtype=jnp.int32)[None, :]).astype(jnp.int32)
    csum = jnp.cumsum(onehot, axis=0)
    rank = jnp.sum(csum * onehot, axis=1) - 1
    counts = csum[-1]
    padded = (counts + MOE_TM - 1) // MOE_TM * MOE_TM
    pad_end = jnp.cumsum(padded)
    pad_start = pad_end - padded
    dest = pad_start[e_flat] + rank
    token = jnp.arange(N_ASSIGN, dtype=jnp.int32) // 2
    src_row = (jnp.arange(MOE_ROWS, dtype=jnp.int32) % ROWS).at[dest].set(token)
    block_start = jnp.arange(MOE_ROWS // MOE_TM, dtype=jnp.int32) * MOE_TM
    block_expert = jnp.minimum(jnp.searchsorted(pad_end, block_start, side="right"), N_EXPERTS - 1).astype(jnp.int32)
    slots = jnp.concatenate([dest[0::2], dest[1::2]])
    n_used = (pad_end[-1:] // MOE_TM).astype(jnp.int32)
    return slots, src_row, block_expert, n_used


def _moe(h, route, w_gate, w_up, w_down, layer, *, name):
    slots, src_row, block_expert, n_used = _moe_plan(route)
    block_expert = block_expert + layer * N_EXPERTS
    xs = _gather_rows(h, src_row, name=name + "_dispatch")
    act = _gmm_gated(xs, w_gate, w_up, block_expert, n_used, tm=MOE_TM, tn=1024, name=name + "_up")
    split = MOE_TM // MOE_DOWN_TM
    ys = _gmm(act, w_down, jnp.repeat(block_expert, split), n_used * split, tm=MOE_DOWN_TM, tn=512, n_out=D,
              out_dtype=F32, name=name + "_down")
    return _gather_rows(ys, slots, name=name + "_combine")


def _mixer_conv(h, cache, w_in, conv_w, w_out):
    bcz = _dense(h, w_in, tn=DENSE_TN, out_dtype=F32, name="a_in")
    y, st_p, st_s = _short_conv(bcz, cache, conv_w, name="a_conv")
    out = _dense(y, w_out, tn=DENSE_TN, out_dtype=F32, name="a_out")
    return out, (st_p, st_s)


def _to_residues(a, dil):
    if dil == 1:
        return a
    w = a.shape[1]
    p = a[:ROWS_P].reshape(N_P, T_P // dil, dil, w).swapaxes(1, 2).reshape(ROWS_P, w)
    return jnp.concatenate([p, a[ROWS_P:]], axis=0)


def _from_residues(a, dil):
    if dil == 1:
        return a
    w = a.shape[1]
    p = a[:ROWS_P].reshape(N_P, dil, T_P // dil, w).swapaxes(1, 2).reshape(ROWS_P, w)
    return jnp.concatenate([p, a[ROWS_P:]], axis=0)


def _mixer_dilated(h, caches, w_qkv, w_o, rel_bias):
    outs, lses, rows_p, rows_s = [], [], [], []
    for gi, (win, dil) in enumerate(DIL_CFG):
        heads = slice(gi * B_HG, (gi + 1) * B_HG)
        qkv = _dense(_to_residues(h, dil), w_qkv, tn=GROUP_W, out_dtype=F32, n_out=3 * GROUP_W,
                     col_map=lambda j, gi=gi: N_DIL * j + gi, name=f"b_qkv{gi}")
        o, lse = _banded_attention(qkv, _band_bias(rel_bias, heads, dil, win // dil),
                                   n_seq=N_P * dil, q_col=0, k_col=1, v_col=2, qw=GROUP_W, kw=GROUP_W,
                                   out_dtype=F32, group=1, with_lse=True, name=f"b_attn_prompt{gi}")
        bias = _sample_bias(rel_bias, heads, win, dil, win).reshape(B_HG * T_S, win + BLK)
        bias_c, bias_n = bias[:, :win], bias[:, win:]
        cache = caches[gi].reshape(N_S, win, 2 * GROUP_W)
        o, lse = _dil_sample_attention(qkv, cache, bias_c, bias_n, o, lse, name=f"b_attn_sample{gi}")
        outs.append(_from_residues(o, dil))
        lses.append(_from_residues(lse, dil))
        keep = min(win, T_P)
        seq, tail = T_P // dil, keep // dil
        kv = jnp.stack([qkv[(s + 1) * seq - tail:(s + 1) * seq, GROUP_W:] for s in range(N_P * dil)])
        kv = kv.reshape(N_P, dil, tail, 2, GROUP_W)
        rows_p.append(kv.swapaxes(1, 2).reshape(1, N_P, keep, 2, B_HG, HEAD_DIM))
        rows_s.append(qkv[ROWS_P:].reshape(N_S, T_S, 3, GROUP_W)[:, :, 1:]
                      .reshape(1, N_S, T_S, 2, B_HG, HEAD_DIM))
    merged = _merge_dilations(outs, lses, name="b_merge")
    out = _dense(merged, w_o, tn=DENSE_TN, out_dtype=F32, name="b_out")
    return out, (rows_p, rows_s)


def _mixer_swa(h, cache, w_qkv, sink, w_o, rel_bias):
    qkv = _dense(h, w_qkv, tn=DENSE_TN, out_dtype=F32, name="c_qkv")
    heads = slice(0, N_HEADS)
    o = _banded_attention(qkv, _band_bias(rel_bias, heads, 1, C_WINDOW - 1), sink.astype(F32),
                          n_seq=N_P, q_col=0, k_col=ATTN_W // C_KW, v_col=ATTN_W // C_KW + 1, qw=ATTN_W, kw=C_KW,
                          out_dtype=BF16, group=C_GROUP, with_lse=False, name="c_attn_prompt")
    bias = _sample_bias(rel_bias, heads, C_WINDOW, 1, C_WINDOW - 1)
    bias = bias.reshape(C_KV_HEADS, C_GROUP * T_S, 2 * BLK)
    sink_tab = jnp.repeat(sink.astype(F32).reshape(C_KV_HEADS, C_GROUP), T_S, axis=1)[:, :, None]
    o = _swa_sample_attention(qkv, cache.reshape(N_S, C_WINDOW, 2 * C_KW), bias[:, :, :BLK], bias[:, :, BLK:],
                              sink_tab, o, name="c_attn_sample")
    kv_p = jnp.stack([qkv[(n + 1) * T_P - C_WINDOW:(n + 1) * T_P, ATTN_W:] for n in range(N_P)])
    kv_p = kv_p.reshape(1, N_P, C_WINDOW, 2, C_KV_HEADS, HEAD_DIM)
    kv_s = qkv[ROWS_P:, ATTN_W:].reshape(1, N_S, T_S, 2, C_KV_HEADS, HEAD_DIM)
    out = _dense(o, w_o, tn=DENSE_TN, out_dtype=F32, name="c_out")
    return out, (kv_p, kv_s)


def _mixer_sgu(h, w_uv, w_s, b_s, w_out):
    uv = _dense(h, w_uv, tn=DENSE_TN, out_dtype=F32, act="gelu", name="d_uv")
    y, vn_s = _spatial_gate(uv, w_s, b_s, name="d_gate")
    out = _dense(y, w_out, tn=DENSE_TN, out_dtype=F32, name="d_out")
    return out, vn_s.reshape(1, N_S, T_S, D)


def _modulations(c_prompt, c_sample, ada_w, ada_b):
    n_sub = 2 * DEPTH
    pad_rows = 48
    cond = jax.nn.silu(jnp.concatenate([c_prompt, c_sample], axis=0))
    cond = jnp.concatenate([cond, jnp.zeros((pad_rows - N_P - N_S, D), F32)], axis=0).astype(BF16)
    x = jnp.tile(cond, (n_sub, 1))
    mod = _gmm(x, ada_w.reshape(n_sub, D, 3 * D), jnp.arange(n_sub, dtype=jnp.int32),
               jnp.full((1,), n_sub, jnp.int32), tm=pad_rows, tn=512, n_out=3 * D, out_dtype=F32, name="ada")
    mod = mod.reshape(n_sub, pad_rows, 3 * D) + ada_b.reshape(n_sub, 1, 3 * D)
    res = []
    for s in range(n_sub):
        modp = mod[s, :N_P].reshape(N_P, 1, 3 * D)
        mods = jnp.repeat(mod[s, N_P:N_P + N_S], T_S, axis=0)
        res.append((modp, mods))
    return res


def kernel(x_prompt, x_sample, c_prompt, c_sample, cache_a_conv, cache_b_kv0, cache_b_kv1, cache_b_kv2, cache_c_kv, norm_g, ada_w, ada_b, rel_bias, a_w_in, a_conv_w, a_w_out, b_w_qkv, b_w_o, c_w_qkv, c_sink, c_w_o, d_w_uv, d_w_s, d_b_s, d_w_out, f_w_gate, f_w_up, f_w_down, m_w_router, m_b_router, m_w_gate, m_w_up, m_w_down):
    mods = _modulations(c_prompt, c_sample, ada_w, ada_b)
    n_moe = m_w_gate.shape[0]
    mw_gate = m_w_gate.reshape((n_moe * N_EXPERTS,) + m_w_gate.shape[2:])
    mw_up = m_w_up.reshape((n_moe * N_EXPERTS,) + m_w_up.shape[2:])
    mw_down = m_w_down.reshape((n_moe * N_EXPERTS,) + m_w_down.shape[2:])

    def sub(i, j):
        return norm_g[i, 2 * j], norm_g[i, 2 * j + 1], mods[2 * i + j]

    def router(fi):
        wr = jnp.zeros((D, LANE), F32).at[:, :N_EXPERTS].set(m_w_router[fi])
        br = jnp.zeros((1, LANE), F32).at[0, :N_EXPERTS].set(m_b_router[fi])
        return wr, br

    x = (x_prompt.reshape(ROWS_P, D), x_sample.reshape(ROWS_S, D))
    g_in, _, (modp, modsr) = sub(0, 0)
    (h,) = _edge(x, nrm=(g_in, modp, modsr), name="norm_mix0")
    state = {}
    for i in range(DEPTH):
        kind = i % 4
        if kind == 0:
            out, state["a"] = _mixer_conv(h, cache_a_conv[0], a_w_in, a_conv_w[0], a_w_out)
        elif kind == 1:
            out, state["b"] = _mixer_dilated(h, (cache_b_kv0[0], cache_b_kv1[0], cache_b_kv2[0]),
                                             b_w_qkv, b_w_o, rel_bias)
        elif kind == 2:
            out, state["c"] = _mixer_swa(h, cache_c_kv[0], c_w_qkv, c_sink[0], c_w_o, rel_bias)
        else:
            out, state["d"] = _mixer_sgu(h, d_w_uv, d_w_s[0], d_b_s[0], d_w_out)

        _, g_out, (modp, modsr) = sub(i, 0)
        g_in, _, (modp2, modsr2) = sub(i, 1)
        fi = i // 2
        moe = i % 2 == 1
        res = _edge(x, y=out, res=(g_out, modp, modsr), nrm=(g_in, modp2, modsr2),
                    router=router(fi) if moe else None, name=f"edge_mix{i}")
        x, h = res[0], res[1]
        if moe:
            route = res[2]
            y = _moe(h, route, mw_gate, mw_up, mw_down, fi, name=f"moe{i}")
        else:
            route = None
            be, nu = _single(ROWS, DENSE_TM, fi)
            act = _gmm_gated(h, f_w_gate, f_w_up, be, nu, tm=DENSE_TM, tn=512, name=f"ffn_up{i}")
            y = _dense(act, f_w_down, tn=512, out_dtype=F32, name=f"ffn_down{i}", which=fi)

        _, g_out, (modp, modsr) = sub(i, 1)
        if i + 1 < DEPTH:
            g_in, _, (modp2, modsr2) = sub(i + 1, 0)
            x, h = _edge(x, y=y, route_in=route, res=(g_out, modp, modsr), nrm=(g_in, modp2, modsr2),
                         name=f"edge_ffn{i}")
        else:
            y_prompt, y_sample = _edge(x, y=y, route_in=route, res=(g_out, modp, modsr), split_out=True,
                                       name=f"edge_ffn{i}")

    (a_p, a_s) = state["a"]
    rows_p, rows_s = state["b"]
    c_p, c_s = state["c"]
    return (y_prompt.reshape(N_P, T_P, D), y_sample.reshape(N_S, T_S, D), a_p[None], a_s[None],
            rows_p[0], rows_s[0], rows_p[1], rows_s[1], rows_p[2], rows_s[2],
            c_p, c_s, state["d"])
```

```python
import functools
import math

import numpy as np
import jax
import jax.numpy as jnp
from jax import lax
from jax.experimental import pallas as pl
from jax.experimental.pallas import tpu as pltpu

D = 2048
N_P, T_P = 4, 2048
N_S, T_S = 32, 8
ROWS_P = N_P * T_P
ROWS_S = N_S * T_S
ROWS = ROWS_P + ROWS_S
DEPTH = 4
HEAD_DIM = 64
N_HEADS = 24
ATTN_W = N_HEADS * HEAD_DIM
DIL_CFG = ((128, 1), (512, 4), (2048, 16))
N_DIL = len(DIL_CFG)
B_HG = 8
GROUP_W = B_HG * HEAD_DIM
C_WINDOW = 128
C_KV_HEADS = 4
C_GROUP = 6
C_KW = C_KV_HEADS * HEAD_DIM
BLK = 128
SG_GROUPS = 8
SG_GW = D // SG_GROUPS
REL_BUCKETS = 32
REL_MAX_DIST = 2048
N_EXPERTS = 8
RMS_EPS = 1e-6
LN_EPS = 1e-5
NEG = -1e30

LANE = 128
VMEM_LIMIT_CAP = 56 * 1024 * 1024
VMEM_LIMIT_FLOOR = 32 * 1024 * 1024
VMEM_MARGIN = 8 * 1024 * 1024

ROW_BLK = 256
N_ROW_BLKS = ROWS // ROW_BLK
N_PROMPT_BLKS = ROWS_P // ROW_BLK
BLKS_PER_SEQ = T_P // ROW_BLK
SAMPLE_BLK8 = ROWS_P // T_S
DENSE_TM = 768
DENSE_TN = 1024
MOE_TM = 512
MOE_DOWN_TM = 256
GATHER_CHUNK = 512
GATHER_UNROLL = 8

BF16 = jnp.bfloat16
F32 = jnp.float32


def _params(sem, nbytes):
    limit = int(min(max(nbytes + VMEM_MARGIN, VMEM_LIMIT_FLOOR), VMEM_LIMIT_CAP))
    return pltpu.CompilerParams(dimension_semantics=sem, vmem_limit_bytes=limit)


def _weight_changed(be_ref):
    b = pl.program_id(1)
    prev = be_ref[jnp.maximum(b - 1, 0)]
    return jnp.logical_or(b == 0, be_ref[b] != prev)


def _gmm_kernel(be_ref, nu_ref, x_ref, w_ref, o_ref, wc_ref, *, act):
    used = pl.program_id(1) < nu_ref[0]

    @pl.when(used)
    def _():
        @pl.when(_weight_changed(be_ref))
        def _():
            wc_ref[...] = w_ref[...].astype(BF16)

        acc = jnp.dot(x_ref[...].astype(BF16), wc_ref[...], preferred_element_type=F32)
        if act == "gelu":
            acc = jax.nn.gelu(acc)
        o_ref[...] = acc.astype(o_ref.dtype)

    @pl.when(jnp.logical_not(used))
    def _():
        o_ref[...] = jnp.zeros_like(o_ref)


def _gmm_gated_kernel(be_ref, nu_ref, x_ref, wg_ref, wu_ref, o_ref, wgc_ref, wuc_ref):
    used = pl.program_id(1) < nu_ref[0]

    @pl.when(used)
    def _():
        @pl.when(_weight_changed(be_ref))
        def _():
            wgc_ref[...] = wg_ref[...].astype(BF16)
            wuc_ref[...] = wu_ref[...].astype(BF16)

        x = x_ref[...].astype(BF16)
        g = jnp.dot(x, wgc_ref[...], preferred_element_type=F32)
        u = jnp.dot(x, wuc_ref[...], preferred_element_type=F32)
        o_ref[...] = (jax.nn.silu(g) * u).astype(o_ref.dtype)

    @pl.when(jnp.logical_not(used))
    def _():
        o_ref[...] = jnp.zeros_like(o_ref)


def _gmm_specs(tm, k, tn, n_w, col_map):
    def x_map(j, b, be, nu):
        return (jnp.minimum(b, nu[0] - 1), 0)

    def w_map(j, b, be, nu):
        return (be[jnp.minimum(b, nu[0] - 1)], 0, col_map(j))

    in_specs = [pl.BlockSpec((tm, k), x_map)] + [pl.BlockSpec((None, k, tn), w_map)] * n_w
    out_spec = pl.BlockSpec((tm, tn), lambda j, b, be, nu: (b, j))
    return in_specs, out_spec


def _gmm(x, w, block_expert, n_used, *, tm, tn, n_out, out_dtype, act=None, col_map=lambda j: j, name):
    m, k = x.shape
    assert k == w.shape[1] and m % tm == 0 and n_out % tn == 0
    in_specs, out_spec = _gmm_specs(tm, k, tn, 1, col_map)
    nbytes = (2 * tm * k * x.dtype.itemsize + 2 * k * tn * 4 + k * tn * 2
              + 2 * tm * tn * jnp.dtype(out_dtype).itemsize)
    return pl.pallas_call(
        functools.partial(_gmm_kernel, act=act),
        out_shape=jax.ShapeDtypeStruct((m, n_out), out_dtype),
        grid_spec=pltpu.PrefetchScalarGridSpec(
            num_scalar_prefetch=2,
            grid=(n_out // tn, m // tm),
            in_specs=in_specs,
            out_specs=out_spec,
            scratch_shapes=[pltpu.VMEM((k, tn), BF16)],
        ),
        compiler_params=_params(("arbitrary", "arbitrary"), nbytes),
        name=name,
    )(block_expert, n_used, x, w)


def _gmm_gated(x, wg, wu, block_expert, n_used, *, tm, tn, name):
    m, k = x.shape
    n = wg.shape[2]
    assert k == wg.shape[1] and m % tm == 0 and n % tn == 0 and wu.shape == wg.shape
    in_specs, out_spec = _gmm_specs(tm, k, tn, 2, lambda j: j)
    nbytes = 2 * tm * k * x.dtype.itemsize + 4 * k * tn * 4 + 2 * k * tn * 2 + 2 * tm * tn * 2
    return pl.pallas_call(
        _gmm_gated_kernel,
        out_shape=jax.ShapeDtypeStruct((m, n), BF16),
        grid_spec=pltpu.PrefetchScalarGridSpec(
            num_scalar_prefetch=2,
            grid=(n // tn, m // tm),
            in_specs=in_specs,
            out_specs=out_spec,
            scratch_shapes=[pltpu.VMEM((k, tn), BF16), pltpu.VMEM((k, tn), BF16)],
        ),
        compiler_params=_params(("arbitrary", "arbitrary"), nbytes),
        name=name,
    )(block_expert, n_used, x, wg, wu)


def _single(m, tm, which=0):
    return jnp.full((m // tm,), which, jnp.int32), jnp.full((1,), m // tm, jnp.int32)


def _dense(x, w, *, tn, out_dtype, act=None, name, tm=DENSE_TM, which=0, n_out=None, col_map=lambda j: j):
    be, nu = _single(x.shape[0], tm, which)
    return _gmm(x, w, be, nu, tm=tm, tn=tn, n_out=n_out or w.shape[2], out_dtype=out_dtype, act=act,
                col_map=col_map, name=name)


def _rms(x, g):
    return x * lax.rsqrt(jnp.mean(x * x, axis=-1, keepdims=True) + RMS_EPS) * g


def _route(h, wr_ref, br_ref):
    logits = jnp.dot(h, wr_ref[...], preferred_element_type=F32, precision=lax.Precision.HIGHEST) + br_ref[...]
    lane = lax.broadcasted_iota(jnp.int32, logits.shape, 1)
    logits = jnp.where(lane < N_EXPERTS, logits, -jnp.inf)
    m1 = jnp.max(logits, axis=-1, keepdims=True)
    i1 = jnp.min(jnp.where(logits == m1, lane, LANE), axis=-1, keepdims=True)
    rest = jnp.where(lane == i1, -jnp.inf, logits)
    m2 = jnp.max(rest, axis=-1, keepdims=True)
    i2 = jnp.min(jnp.where(rest == m2, lane, LANE), axis=-1, keepdims=True)
    e2 = jnp.exp(m2 - m1)
    g1 = 1.0 / (1.0 + e2)
    g2 = e2 / (1.0 + e2)
    out = jnp.where(lane == 0, i1.astype(F32),
                    jnp.where(lane == 1, i2.astype(F32),
                              jnp.where(lane == 2, g1, jnp.where(lane == 3, g2, 0.0))))
    return out


def _edge_kernel(*refs, split_in, resid, moe_in, norm, route, split_out):
    refs = list(refs)

    def pop(n):
        return [refs.pop(0) for _ in range(n)]

    x_refs = pop(2) if split_in else pop(1) * 2
    if resid:
        y_refs = pop(3) if moe_in else pop(1)
        gr_ref, gate_p, gate_s = pop(3)
    if norm:
        gn_ref, shift_p, scale_p, shift_s, scale_s = pop(5)
        if route:
            wr_ref, br_ref = pop(2)
    if resid:
        xo_refs = pop(2) if split_out else pop(1) * 2
    if norm:
        (h_ref,) = pop(1)
        if route:
            (r_ref,) = pop(1)

    def run(sample):
        x = x_refs[sample][...]
        if resid:
            if moe_in:
                ya_ref, yb_ref, rp_ref = y_refs
                rp = rp_ref[...]
                out = rp[:, 2:3] * ya_ref[...] + rp[:, 3:4] * yb_ref[...]
            else:
                out = y_refs[0][...]
            x = x + (gate_s if sample else gate_p)[...] * _rms(out, gr_ref[...])
            xo_refs[sample][...] = x
        if norm:
            shift = (shift_s if sample else shift_p)[...]
            scale = (scale_s if sample else scale_p)[...]
            h = _rms(x, gn_ref[...]) * (1.0 + scale) + shift
            h_ref[...] = h.astype(h_ref.dtype)
            if route:
                r_ref[...] = _route(h, wr_ref, br_ref)

    i = pl.program_id(0)
    pl.when(i < N_PROMPT_BLKS)(functools.partial(run, 0))
    pl.when(i == N_PROMPT_BLKS)(functools.partial(run, 1))


def _mod_specs(cols):
    specs = []
    for c in cols:
        specs.append(pl.BlockSpec((None, 1, D), lambda i, c=c: (jnp.minimum(i // BLKS_PER_SEQ, N_P - 1), 0, c)))
    for c in cols:
        specs.append(pl.BlockSpec((ROW_BLK, D), lambda i, c=c: (0, c)))
    return specs


def _edge(x, *, y=None, route_in=None, res=None, nrm=None, router=None, split_out=False, name):
    split_in = isinstance(x, tuple)
    resid, norm, moe_in, route = res is not None, nrm is not None, route_in is not None, router is not None
    row = pl.BlockSpec((ROW_BLK, D), lambda i: (i, 0))
    row_p = pl.BlockSpec((ROW_BLK, D), lambda i: (jnp.minimum(i, N_PROMPT_BLKS - 1), 0))
    row_s = pl.BlockSpec((ROW_BLK, D), lambda i: (0, 0))
    gain = pl.BlockSpec((1, D), lambda i: (0, 0))
    lane_row = pl.BlockSpec((ROW_BLK, LANE), lambda i: (i, 0))
    in_specs = [row_p, row_s] if split_in else [row]
    args = list(x) if split_in else [x]
    out_shape, out_specs = [], []
    if resid:
        g, modp, mods = res
        if moe_in:
            in_specs += [row, pl.BlockSpec((ROW_BLK, D), lambda i: (i + N_ROW_BLKS, 0)), lane_row]
            args += [y, y, route_in]
        else:
            in_specs += [row]
            args += [y]
        in_specs += [gain] + _mod_specs((2,))
        args += [g.reshape(1, D), modp, mods]
        if split_out:
            out_shape += [jax.ShapeDtypeStruct((ROWS_P, D), F32), jax.ShapeDtypeStruct((ROWS_S, D), F32)]
            out_specs += [row_p, row_s]
        else:
            out_shape += [jax.ShapeDtypeStruct((ROWS, D), F32)]
            out_specs += [row]
    if norm:
        g, modp, mods = nrm
        in_specs += [gain] + _mod_specs((0, 1))
        args += [g.reshape(1, D), modp, modp, mods, mods]
        if route:
            in_specs += [pl.BlockSpec((D, LANE), lambda i: (0, 0)), pl.BlockSpec((1, LANE), lambda i: (0, 0))]
            args += list(router)
        out_shape += [jax.ShapeDtypeStruct((ROWS, D), BF16)]
        out_specs += [row]
        if route:
            out_shape += [jax.ShapeDtypeStruct((ROWS, LANE), F32)]
            out_specs += [lane_row]
    n_blocks = len(in_specs) + len(out_specs)
    return pl.pallas_call(
        functools.partial(_edge_kernel, split_in=split_in, resid=resid, moe_in=moe_in, norm=norm, route=route,
                          split_out=split_out),
        out_shape=out_shape,
        grid=(N_ROW_BLKS,),
        in_specs=in_specs,
        out_specs=out_specs,
        compiler_params=_params(("arbitrary",), n_blocks * ROW_BLK * D * 4 * 2),
        name=name,
    )(*args)


def _conv_kernel(gb_ref, gc_ref, z_ref, cw_ref, b1_ref, b2_ref, y_ref, stp_ref, us_ref, tail_ref):
    i = pl.program_id(0)
    u = gc_ref[...] * z_ref[...]
    r = lax.broadcasted_iota(jnp.int32, u.shape, 0)
    roll1 = pltpu.roll(u, 1, axis=0)
    roll2 = pltpu.roll(u, 2, axis=0)

    def emit(um1, um2):
        conv = cw_ref[0:1, :] * um2 + cw_ref[1:2, :] * um1 + cw_ref[2:3, :] * u
        y_ref[...] = (gb_ref[...] * conv).astype(y_ref.dtype)

    @pl.when(i < N_PROMPT_BLKS)
    def _():
        @pl.when(i % BLKS_PER_SEQ == 0)
        def _():
            tail_ref[...] = jnp.zeros_like(tail_ref)

        prev2 = tail_ref[6:7, :]
        prev1 = tail_ref[7:8, :]
        emit(jnp.where(r == 0, prev1, roll1),
             jnp.where(r == 0, prev2, jnp.where(r == 1, prev1, roll2)))
        tail_ref[...] = u[ROW_BLK - 8:ROW_BLK, :]
        stp_ref[...] = u[ROW_BLK - 2:ROW_BLK, :]

    @pl.when(i == N_PROMPT_BLKS)
    def _():
        t = r % T_S
        emit(jnp.where(t == 0, b1_ref[...], roll1), jnp.where(t < 2, b2_ref[...], roll2))
        us_ref[...] = u


def _short_conv(bcz, cache, conv_w, *, name):
    zeros = jnp.zeros((N_S, T_S, D), F32)
    b1 = zeros.at[:, 0].set(cache[:, 1]).reshape(ROWS_S, D)
    b2 = zeros.at[:, 0].set(cache[:, 0]).at[:, 1].set(cache[:, 1]).reshape(ROWS_S, D)

    def col(c):
        return pl.BlockSpec((ROW_BLK, D), lambda i, c=c: (i, c))

    const = pl.BlockSpec((ROW_BLK, D), lambda i: (0, 0))
    y, st_p, u_s = pl.pallas_call(
        _conv_kernel,
        out_shape=[jax.ShapeDtypeStruct((ROWS, D), BF16), jax.ShapeDtypeStruct((N_P, 2, D), F32),
                   jax.ShapeDtypeStruct((ROWS_S, D), F32)],
        grid=(N_ROW_BLKS,),
        in_specs=[col(0), col(1), col(2), pl.BlockSpec((3, D), lambda i: (0, 0)), const, const],
        out_specs=[pl.BlockSpec((ROW_BLK, D), lambda i: (i, 0)),
                   pl.BlockSpec((None, 2, D), lambda i: (jnp.minimum(i // BLKS_PER_SEQ, N_P - 1), 0, 0)),
                   const],
        scratch_shapes=[pltpu.VMEM((8, D), F32)],
        compiler_params=_params(("arbitrary",), ROW_BLK * D * 4 * 7 * 2),
        name=name,
    )(bcz, bcz, bcz, conv_w, b1, b2)
    return y, st_p, u_s.reshape(N_S, T_S, D)[:, T_S - 2:]


def _rel_bucket(dist):
    n = np.maximum(np.asarray(dist), 0)
    exact = REL_BUCKETS // 2
    ratio = np.log(np.maximum(n, 1).astype(np.float32) / exact) / math.log(REL_MAX_DIST / exact)
    large = np.minimum(exact + (ratio * (REL_BUCKETS - exact)).astype(np.int32), REL_BUCKETS - 1)
    return np.where(n < exact, n, large).astype(np.int32)


def _toeplitz(a, n_rows, n_cols):
    h, p = a.shape
    assert n_cols < p
    t = jnp.tile(a, (1, n_rows))[:, :n_rows * (p - 1)].reshape(h, n_rows, p - 1)
    return t[:, :, :n_cols]


def _bias_line(rel_bias, heads, dist, valid):
    tab = rel_bias[_rel_bucket(dist)][:, heads]
    return jnp.where(valid[:, None], tab, NEG).T.astype(F32)


def _band_bias(rel_bias, heads, dil, max_dist):
    p = 3 * BLK
    m = np.arange(p)
    dist = np.where(m < 2 * BLK, BLK - m, BLK + p - m)
    valid = (dist >= 0) & (dist <= max_dist)
    return _toeplitz(_bias_line(rel_bias, heads, dist * dil, valid), BLK, 2 * BLK)


def _sample_bias(rel_bias, heads, win, dil, max_dist):
    lk = win + BLK
    p = lk + T_S
    m = np.arange(p)
    dist = np.where(m < lk, win - m, win + p - m)
    valid = (dist >= 0) & (dist % dil == 0) & (dist <= max_dist)
    return _toeplitz(_bias_line(rel_bias, heads, dist, valid), T_S, lk)


def _softmax_pv(s, v, sink=None):
    m = jnp.max(s, axis=-1, keepdims=True)
    if sink is not None:
        m = jnp.maximum(m, sink)
    p = jnp.exp(s - m)
    l = jnp.sum(p, axis=-1, keepdims=True)
    if sink is not None:
        l = l + jnp.exp(sink - m)
    o = jnp.dot(p.astype(BF16), v, preferred_element_type=F32) / l
    return o, m + jnp.log(l)


def _nt_dot(a, b):
    return lax.dot_general(a, b, (((1,), (1,)), ((), ())), preferred_element_type=F32)


def _banded_kernel(*refs, n_seq, **kw):
    out_refs = refs[-2:] if kw["with_lse"] else refs[-1:]

    @pl.when(pl.program_id(0) < n_seq)
    def _():
        _banded_body(*refs, **kw)

    @pl.when(pl.program_id(0) >= n_seq)
    def _():
        for ref in out_refs:
            ref[...] = jnp.zeros_like(ref)


def _banded_body(*refs, n_heads, group, with_lse, has_sink):
    refs = list(refs)
    sink_ref = refs.pop(0) if has_sink else None
    q_ref, kp_ref, kc_ref, vp_ref, vc_ref, bias_ref = refs[:6]
    o_ref = refs[6]
    lse_ref = refs[7] if with_lse else None
    qb = pl.program_id(1)
    q = q_ref[...] * (HEAD_DIM ** -0.5)
    k = jnp.concatenate([kp_ref[...], kc_ref[...]], axis=0).astype(BF16)
    v = jnp.concatenate([vp_ref[...], vc_ref[...]], axis=0).astype(BF16)
    col = lax.broadcasted_iota(jnp.int32, (BLK, 2 * BLK), 1)
    no_prev = jnp.logical_and(qb == 0, col < BLK)
    low = lax.broadcasted_iota(jnp.int32, (BLK, LANE), 1) < HEAD_DIM
    for t in range(n_heads // 2):
        tile = slice(t * LANE, (t + 1) * LANE)
        q_tile = q[:, tile]
        outs, lses = [], []
        for half in range(2):
            h = 2 * t + half
            hk = h // group
            k_tile = slice(hk // 2 * LANE, (hk // 2 + 1) * LANE)
            k_half = hk % 2
            qh = q_tile if k_half == half else pltpu.roll(q_tile, HEAD_DIM, axis=1)
            qh = jnp.where(low if k_half == 0 else jnp.logical_not(low), qh, 0.0).astype(BF16)
            s = _nt_dot(qh, k[:, k_tile]) + bias_ref[h]
            s = jnp.where(no_prev, NEG, s)
            sink = sink_ref[h] if has_sink else None
            o, lse = _softmax_pv(s, v[:, k_tile], sink)
            outs.append(o if k_half == half else pltpu.roll(o, HEAD_DIM, axis=1))
            lses.append(lse)
        o_ref[:, tile] = jnp.where(low, outs[0], outs[1]).astype(o_ref.dtype)
        if with_lse:
            lse_ref[:, tile] = jnp.where(low, lses[0], lses[1])


def _banded_attention(src, bias, sink=None, *, n_seq, q_col, k_col, v_col, qw, kw, out_dtype, group, with_lse, name):
    n_heads = qw // HEAD_DIM
    has_sink = sink is not None
    n_blk = ROWS_P // BLK // n_seq
    fill_row0 = ROWS_P // BLK
    fill_blks = ROWS_S // BLK
    n_fill_seq = -(-fill_blks // n_blk)

    def cur(col):
        return lambda s, b: (jnp.minimum(s, n_seq - 1) * n_blk + b, col)

    def prev(col):
        return lambda s, b: (jnp.minimum(s, n_seq - 1) * n_blk + jnp.maximum(b - 1, 0), col)

    def out_map(s, b):
        fill = jnp.minimum((s - n_seq) * n_blk + b, fill_blks - 1)
        return (jnp.where(s >= n_seq, fill_row0 + fill, s * n_blk + b), 0)

    in_specs = [pl.BlockSpec((BLK, qw), cur(q_col)),
                pl.BlockSpec((BLK, kw), prev(k_col)), pl.BlockSpec((BLK, kw), cur(k_col)),
                pl.BlockSpec((BLK, kw), prev(v_col)), pl.BlockSpec((BLK, kw), cur(v_col)),
                pl.BlockSpec((n_heads, BLK, 2 * BLK), lambda s, b: (0, 0, 0))]
    args = [src, src, src, src, src, bias]
    if has_sink:
        in_specs = [pl.BlockSpec(memory_space=pltpu.SMEM)] + in_specs
        args = [sink] + args
    out_shapes = [jax.ShapeDtypeStruct((ROWS, qw), out_dtype)]
    out_specs = [pl.BlockSpec((BLK, qw), out_map)]
    if with_lse:
        out_shapes.append(jax.ShapeDtypeStruct((ROWS, qw), F32))
        out_specs.append(pl.BlockSpec((BLK, qw), out_map))
    nbytes = (BLK * qw * 4 * 3 + BLK * kw * 4 * 4) * 2 + n_heads * BLK * 2 * BLK * 4 * 2
    res = pl.pallas_call(
        functools.partial(_banded_kernel, n_seq=n_seq, n_heads=n_heads, group=group, with_lse=with_lse,
                          has_sink=has_sink),
        out_shape=out_shapes,
        grid=(n_seq + n_fill_seq, n_blk),
        in_specs=in_specs,
        out_specs=out_specs,
        compiler_params=_params(("arbitrary", "arbitrary"), nbytes),
        name=name,
    )(*args)
    return res if with_lse else res[0]


def _dil_sample_kernel(q_ref, kn_ref, vn_ref, cache_ref, bc_ref, bn_ref, o_in, lse_in, o_ref, lse_ref):
    del o_in, lse_in
    q = q_ref[...] * (HEAD_DIM ** -0.5)
    qt = jnp.concatenate([q] * B_HG, axis=0)
    row_h = lax.broadcasted_iota(jnp.int32, qt.shape, 0) // T_S
    col_h = lax.broadcasted_iota(jnp.int32, qt.shape, 1) // HEAD_DIM
    qbd = jnp.where(row_h == col_h, qt, 0.0).astype(BF16)
    pad = jnp.zeros((BLK - T_S, GROUP_W), F32)
    kn = jnp.concatenate([kn_ref[...], pad], axis=0).astype(BF16)
    vn = jnp.concatenate([vn_ref[...], pad], axis=0).astype(BF16)
    kc = cache_ref[:, 0:GROUP_W].astype(BF16)
    vc = cache_ref[:, GROUP_W:2 * GROUP_W].astype(BF16)
    sc = _nt_dot(qbd, kc) + bc_ref[...]
    sn = _nt_dot(qbd, kn) + bn_ref[...]
    m = jnp.maximum(jnp.max(sc, axis=-1, keepdims=True), jnp.max(sn, axis=-1, keepdims=True))
    pc = jnp.exp(sc - m)
    pn = jnp.exp(sn - m)
    l = jnp.sum(pc, axis=-1, keepdims=True) + jnp.sum(pn, axis=-1, keepdims=True)
    big = (jnp.dot(pc.astype(BF16), vc, preferred_element_type=F32)
           + jnp.dot(pn.astype(BF16), vn, preferred_element_type=F32)) / l
    lse = m + jnp.log(l)
    for h in range(B_HG):
        rs = slice(h * T_S, (h + 1) * T_S)
        cs = slice(h * HEAD_DIM, (h + 1) * HEAD_DIM)
        o_ref[:, cs] = big[rs, cs]
        lse_ref[:, cs] = jnp.broadcast_to(lse[rs], (T_S, HEAD_DIM))


def _dil_sample_attention(qkv, cache, bias_c, bias_n, o, lse, *, name):
    lb = cache.shape[1]
    row = lambda c: pl.BlockSpec((T_S, GROUP_W), lambda n, c=c: (SAMPLE_BLK8 + n, c))
    const = lambda n: (0, 0)
    anyspec = pl.BlockSpec(memory_space=pl.ANY)
    nbytes = lb * 2 * GROUP_W * 4 * 2 + 64 * lb * 4 * 6
    return pl.pallas_call(
        _dil_sample_kernel,
        out_shape=[jax.ShapeDtypeStruct((ROWS, GROUP_W), F32)] * 2,
        grid=(N_S,),
        in_specs=[row(0), row(1), row(2),
                  pl.BlockSpec((None, lb, 2 * GROUP_W), lambda n: (n, 0, 0)),
                  pl.BlockSpec((B_HG * T_S, lb), const), pl.BlockSpec((B_HG * T_S, BLK), const),
                  anyspec, anyspec],
        out_specs=[row(0)] * 2,
        input_output_aliases={6: 0, 7: 1},
        compiler_params=_params(("parallel",), nbytes),
        name=name,
    )(qkv, qkv, qkv, cache, bias_c, bias_n, o, lse)


def _merge_kernel(o0, o1, o2, l0, l1, l2, out_ref):
    ls = [l0[...], l1[...], l2[...]]
    m = jnp.maximum(jnp.maximum(ls[0], ls[1]), ls[2])
    es = [jnp.exp(l - m) for l in ls]
    tot = es[0] + es[1] + es[2]
    for g, o in enumerate((o0, o1, o2)):
        out_ref[:, g * GROUP_W:(g + 1) * GROUP_W] = (o[...] * (es[g] / tot)).astype(out_ref.dtype)


def _merge_dilations(outs, lses, *, name):
    spec = pl.BlockSpec((ROW_BLK, GROUP_W), lambda i: (i, 0))
    return pl.pallas_call(
        _merge_kernel,
        out_shape=jax.ShapeDtypeStruct((ROWS, ATTN_W), BF16),
        grid=(N_ROW_BLKS,),
        in_specs=[spec] * 6,
        out_specs=pl.BlockSpec((ROW_BLK, ATTN_W), lambda i: (i, 0)),
        compiler_params=_params(("parallel",), ROW_BLK * GROUP_W * 4 * 8 * 2),
        name=name,
    )(*outs, *lses)


def _swa_sample_kernel(q_ref, kvn_ref, cache_ref, bc_ref, bn_ref, sink_ref, o_in, o_ref):
    del o_in
    q = q_ref[...] * (HEAD_DIM ** -0.5)
    pad = jnp.zeros((BLK - T_S, 2 * C_KW), F32)
    kvn = jnp.concatenate([kvn_ref[...], pad], axis=0).astype(BF16)
    kvc = cache_ref[...].astype(BF16)
    for hk in range(C_KV_HEADS):
        heads = range(hk * C_GROUP, (hk + 1) * C_GROUP)
        qh = jnp.concatenate([q[:, h * HEAD_DIM:(h + 1) * HEAD_DIM] for h in heads], axis=0).astype(BF16)
        ks = slice(hk * HEAD_DIM, (hk + 1) * HEAD_DIM)
        vs = slice(C_KW + hk * HEAD_DIM, C_KW + (hk + 1) * HEAD_DIM)
        sc = _nt_dot(qh, kvc[:, ks]) + bc_ref[hk]
        sn = _nt_dot(qh, kvn[:, ks]) + bn_ref[hk]
        sink = sink_ref[hk]
        m = jnp.maximum(jnp.maximum(jnp.max(sc, axis=-1, keepdims=True),
                                    jnp.max(sn, axis=-1, keepdims=True)), sink)
        pc = jnp.exp(sc - m)
        pn = jnp.exp(sn - m)
        l = jnp.sum(pc, axis=-1, keepdims=True) + jnp.sum(pn, axis=-1, keepdims=True) + jnp.exp(sink - m)
        o = (jnp.dot(pc.astype(BF16), kvc[:, vs], preferred_element_type=F32)
             + jnp.dot(pn.astype(BF16), kvn[:, vs], preferred_element_type=F32)) / l
        for gi, h in enumerate(heads):
            o_ref[:, h * HEAD_DIM:(h + 1) * HEAD_DIM] = o[gi * T_S:(gi + 1) * T_S].astype(o_ref.dtype)


def _swa_sample_attention(qkv, cache, bias_c, bias_n, sink_tab, o, *, name):
    rows = C_GROUP * T_S
    const = lambda n: (0, 0, 0)
    return pl.pallas_call(
        _swa_sample_kernel,
        out_shape=jax.ShapeDtypeStruct((ROWS, ATTN_W), o.dtype),
        grid=(N_S,),
        in_specs=[pl.BlockSpec((T_S, ATTN_W), lambda n: (SAMPLE_BLK8 + n, 0)),
                  pl.BlockSpec((T_S, 2 * C_KW), lambda n: (SAMPLE_BLK8 + n, ATTN_W // (2 * C_KW))),
                  pl.BlockSpec((None, C_WINDOW, 2 * C_KW), lambda n: (n, 0, 0)),
                  pl.BlockSpec((C_KV_HEADS, rows, BLK), const), pl.BlockSpec((C_KV_HEADS, rows, BLK), const),
                  pl.BlockSpec((C_KV_HEADS, rows, 1), const),
                  pl.BlockSpec(memory_space=pl.ANY)],
        out_specs=pl.BlockSpec((T_S, ATTN_W), lambda n: (SAMPLE_BLK8 + n, 0)),
        input_output_aliases={6: 0},
        compiler_params=_params(("parallel",), 1 << 20),
        name=name,
    )(qkv, qkv, cache, bias_c, bias_n, sink_tab, o)


def _sgu_kernel(u_ref, v_ref, w_ref, b_ref, ws_ref, bs_ref, y_ref, vn_ref):
    i = pl.program_id(0)
    v = v_ref[...]
    mu = jnp.mean(v, axis=-1, keepdims=True)
    var = jnp.mean(jnp.square(v - mu), axis=-1, keepdims=True)
    vn = (v - mu) * lax.rsqrt(var + LN_EPS)
    vb = vn.astype(BF16)

    @pl.when(i < N_PROMPT_BLKS)
    def _():
        causal = (lax.broadcasted_iota(jnp.int32, (BLK, BLK), 0) >= lax.broadcasted_iota(jnp.int32, (BLK, BLK), 1))
        for g in range(SG_GROUPS):
            cs = slice(g * SG_GW, (g + 1) * SG_GW)
            wg = jnp.where(causal, w_ref[g], 0.0).astype(BF16)
            for c in range(ROW_BLK // BLK):
                rs = slice(c * BLK, (c + 1) * BLK)
                s = jnp.dot(wg, vb[rs, cs], preferred_element_type=F32) + b_ref[g]
                y_ref[rs, cs] = (u_ref[rs, cs] * s).astype(y_ref.dtype)

    @pl.when(i == N_PROMPT_BLKS)
    def _():
        vn_ref[...] = vn
        for g in range(SG_GROUPS):
            cs = slice(g * SG_GW, (g + 1) * SG_GW)
            s = jnp.dot(ws_ref[g].astype(BF16), vb[:, cs], preferred_element_type=F32) + bs_ref[g]
            y_ref[:, cs] = (u_ref[:, cs] * s).astype(y_ref.dtype)


def _spatial_gate(uv, w_s, b_s, *, name):
    r = np.arange(ROWS_S)
    same_seq = (r[:, None] // T_S) == (r[None, :] // T_S)
    causal = (r[:, None] % T_S) >= (r[None, :] % T_S)
    w8 = jnp.tile(w_s[:, :T_S, :T_S], (1, N_S, N_S))
    ws = jnp.where((same_seq & causal)[None], w8, 0.0)
    bs = jnp.tile(b_s[:, :T_S], (1, N_S)).reshape(SG_GROUPS, ROWS_S, 1)
    const3 = lambda i: (0, 0, 0)
    nbytes = ROW_BLK * D * 4 * 4 * 2 + SG_GROUPS * (BLK * BLK + ROW_BLK * ROW_BLK) * 4 * 2
    return pl.pallas_call(
        _sgu_kernel,
        out_shape=[jax.ShapeDtypeStruct((ROWS, D), BF16), jax.ShapeDtypeStruct((ROWS_S, D), F32)],
        grid=(N_ROW_BLKS,),
        in_specs=[pl.BlockSpec((ROW_BLK, D), lambda i: (i, 0)), pl.BlockSpec((ROW_BLK, D), lambda i: (i, 1)),
                  pl.BlockSpec((SG_GROUPS, BLK, BLK), const3), pl.BlockSpec((SG_GROUPS, BLK, 1), const3),
                  pl.BlockSpec((SG_GROUPS, ROWS_S, ROWS_S), const3), pl.BlockSpec((SG_GROUPS, ROWS_S, 1), const3)],
        out_specs=[pl.BlockSpec((ROW_BLK, D), lambda i: (i, 0)), pl.BlockSpec((ROWS_S, D), lambda i: (0, 0))],
        compiler_params=_params(("arbitrary",), nbytes),
        name=name,
    )(uv, uv, w_s, b_s.reshape(SG_GROUPS, BLK, 1), ws, bs)


def _gather_kernel(idx_ref, src_ref, o_ref, sem):
    base = pl.program_id(0) * GATHER_CHUNK

    def row_copy(r):
        return pltpu.make_async_copy(src_ref.at[idx_ref[base + r]], o_ref.at[r], sem)

    def start(g, carry):
        for u in range(GATHER_UNROLL):
            row_copy(g * GATHER_UNROLL + u).start()
        return carry

    def wait(g, carry):
        for u in range(GATHER_UNROLL):
            row_copy(g * GATHER_UNROLL + u).wait()
        return carry

    lax.fori_loop(0, GATHER_CHUNK // GATHER_UNROLL, start, 0)
    lax.fori_loop(0, GATHER_CHUNK // GATHER_UNROLL, wait, 0)


def _gather_rows(src, idx, *, name):
    n = src.shape[0]
    r = idx.shape[0]
    assert r % GATHER_CHUNK == 0
    slab = (D // LANE, LANE)
    out = pl.pallas_call(
        _gather_kernel,
        out_shape=jax.ShapeDtypeStruct((r,) + slab, src.dtype),
        grid_spec=pltpu.PrefetchScalarGridSpec(
            num_scalar_prefetch=1,
            grid=(r // GATHER_CHUNK,),
            in_specs=[pl.BlockSpec(memory_space=pl.ANY)],
            out_specs=pl.BlockSpec((GATHER_CHUNK,) + slab, lambda c, idx: (c, 0, 0)),
            scratch_shapes=[pltpu.SemaphoreType.DMA],
        ),
        compiler_params=_params(("arbitrary",), 2 * GATHER_CHUNK * D * src.dtype.itemsize),
        name=name,
    )(idx, src.reshape((n,) + slab))
    return out.reshape(r, D)


N_ASSIGN = 2 * ROWS
_MOE_ROWS_MIN = N_ASSIGN + N_EXPERTS * (MOE_TM - 1)
_MOE_UNIT = MOE_TM * GATHER_CHUNK // math.gcd(MOE_TM, GATHER_CHUNK)
MOE_ROWS = -(-_MOE_ROWS_MIN // _MOE_UNIT) * _MOE_UNIT
assert N_ASSIGN % GATHER_CHUNK == 0


def _moe_plan(route):
    e_flat = route[:, 0:2].astype(jnp.int32).reshape(-1)
    onehot = (e_flat[:, None] == jnp.arange(N_EXPERTS, dtype=jnp.int32)[None, :]).astype(jnp.int32)
    csum = jnp.cumsum(onehot, axis=0)
    rank = jnp.sum(csum * onehot, axis=1) - 1
    counts = csum[-1]
    padded = (counts + MOE_TM - 1) // MOE_TM * MOE_TM
    pad_end = jnp.cumsum(padded)
    pad_start = pad_end - padded
    dest = pad_start[e_flat] + rank
    token = jnp.arange(N_ASSIGN, dtype=jnp.int32) // 2
    src_row = (jnp.arange(MOE_ROWS, dtype=jnp.int32) % ROWS).at[dest].set(token)
    block_start = jnp.arange(MOE_ROWS // MOE_TM, dtype=jnp.int32) * MOE_TM
    block_expert = jnp.minimum(jnp.searchsorted(pad_end, block_start, side="right"), N_EXPERTS - 1).astype(jnp.int32)
    slots = jnp.concatenate([dest[0::2], dest[1::2]])
    n_used = (pad_end[-1:] // MOE_TM).astype(jnp.int32)
    return slots, src_row, block_expert, n_used


def _moe(h, route, w_gate, w_up, w_down, layer, *, name):
    slots, src_row, block_expert, n_used = _moe_plan(route)
    block_expert = block_expert + layer * N_EXPERTS
    xs = _gather_rows(h, src_row, name=name + "_dispatch")
    act = _gmm_gated(xs, w_gate, w_up, block_expert, n_used, tm=MOE_TM, tn=1024, name=name + "_up")
    split = MOE_TM // MOE_DOWN_TM
    ys = _gmm(act, w_down, jnp.repeat(block_expert, split), n_used * split, tm=MOE_DOWN_TM, tn=512, n_out=D,
              out_dtype=F32, name=name + "_down")
    return _gather_rows(ys, slots, name=name + "_combine")


def _mixer_conv(h, cache, w_in, conv_w, w_out):
    bcz = _dense(h, w_in, tn=DENSE_TN, out_dtype=F32, name="a_in")
    y, st_p, st_s = _short_conv(bcz, cache, conv_w, name="a_conv")
    out = _dense(y, w_out, tn=DENSE_TN, out_dtype=F32, name="a_out")
    return out, (st_p, st_s)


def _to_residues(a, dil):
    if dil == 1:
        return a
    w = a.shape[1]
    p = a[:ROWS_P].reshape(N_P, T_P // dil, dil, w).swapaxes(1, 2).reshape(ROWS_P, w)
    return jnp.concatenate([p, a[ROWS_P:]], axis=0)


def _from_residues(a, dil):
    if dil == 1:
        return a
    w = a.shape[1]
    p = a[:ROWS_P].reshape(N_P, dil, T_P // dil, w).swapaxes(1, 2).reshape(ROWS_P, w)
    return jnp.concatenate([p, a[ROWS_P:]], axis=0)


def _mixer_dilated(h, caches, w_qkv, w_o, rel_bias):
    outs, lses, rows_p, rows_s = [], [], [], []
    for gi, (win, dil) in enumerate(DIL_CFG):
        heads = slice(gi * B_HG, (gi + 1) * B_HG)
        qkv = _dense(_to_residues(h, dil), w_qkv, tn=GROUP_W, out_dtype=F32, n_out=3 * GROUP_W,
                     col_map=lambda j, gi=gi: N_DIL * j + gi, name=f"b_qkv{gi}")
        o, lse = _banded_attention(qkv, _band_bias(rel_bias, heads, dil, win // dil),
                                   n_seq=N_P * dil, q_col=0, k_col=1, v_col=2, qw=GROUP_W, kw=GROUP_W,
                                   out_dtype=F32, group=1, with_lse=True, name=f"b_attn_prompt{gi}")
        bias = _sample_bias(rel_bias, heads, win, dil, win).reshape(B_HG * T_S, win + BLK)
        bias_c, bias_n = bias[:, :win], bias[:, win:]
        cache = caches[gi].reshape(N_S, win, 2 * GROUP_W)
        o, lse = _dil_sample_attention(qkv, cache, bias_c, bias_n, o, lse, name=f"b_attn_sample{gi}")
        outs.append(_from_residues(o, dil))
        lses.append(_from_residues(lse, dil))
        keep = min(win, T_P)
        seq, tail = T_P // dil, keep // dil
        kv = jnp.stack([qkv[(s + 1) * seq - tail:(s + 1) * seq, GROUP_W:] for s in range(N_P * dil)])
        kv = kv.reshape(N_P, dil, tail, 2, GROUP_W)
        rows_p.append(kv.swapaxes(1, 2).reshape(1, N_P, keep, 2, B_HG, HEAD_DIM))
        rows_s.append(qkv[ROWS_P:].reshape(N_S, T_S, 3, GROUP_W)[:, :, 1:]
                      .reshape(1, N_S, T_S, 2, B_HG, HEAD_DIM))
    merged = _merge_dilations(outs, lses, name="b_merge")
    out = _dense(merged, w_o, tn=DENSE_TN, out_dtype=F32, name="b_out")
    return out, (rows_p, rows_s)


def _mixer_swa(h, cache, w_qkv, sink, w_o, rel_bias):
    qkv = _dense(h, w_qkv, tn=DENSE_TN, out_dtype=F32, name="c_qkv")
    heads = slice(0, N_HEADS)
    o = _banded_attention(qkv, _band_bias(rel_bias, heads, 1, C_WINDOW - 1), sink.astype(F32),
                          n_seq=N_P, q_col=0, k_col=ATTN_W // C_KW, v_col=ATTN_W // C_KW + 1, qw=ATTN_W, kw=C_KW,
                          out_dtype=BF16, group=C_GROUP, with_lse=False, name="c_attn_prompt")
    bias = _sample_bias(rel_bias, heads, C_WINDOW, 1, C_WINDOW - 1)
    bias = bias.reshape(C_KV_HEADS, C_GROUP * T_S, 2 * BLK)
    sink_tab = jnp.repeat(sink.astype(F32).reshape(C_KV_HEADS, C_GROUP), T_S, axis=1)[:, :, None]
    o = _swa_sample_attention(qkv, cache.reshape(N_S, C_WINDOW, 2 * C_KW), bias[:, :, :BLK], bias[:, :, BLK:],
                              sink_tab, o, name="c_attn_sample")
    kv_p = jnp.stack([qkv[(n + 1) * T_P - C_WINDOW:(n + 1) * T_P, ATTN_W:] for n in range(N_P)])
    kv_p = kv_p.reshape(1, N_P, C_WINDOW, 2, C_KV_HEADS, HEAD_DIM)
    kv_s = qkv[ROWS_P:, ATTN_W:].reshape(1, N_S, T_S, 2, C_KV_HEADS, HEAD_DIM)
    out = _dense(o, w_o, tn=DENSE_TN, out_dtype=F32, name="c_out")
    return out, (kv_p, kv_s)


def _mixer_sgu(h, w_uv, w_s, b_s, w_out):
    uv = _dense(h, w_uv, tn=DENSE_TN, out_dtype=F32, act="gelu", name="d_uv")
    y, vn_s = _spatial_gate(uv, w_s, b_s, name="d_gate")
    out = _dense(y, w_out, tn=DENSE_TN, out_dtype=F32, name="d_out")
    return out, vn_s.reshape(1, N_S, T_S, D)


def _modulations(c_prompt, c_sample, ada_w, ada_b):
    n_sub = 2 * DEPTH
    pad_rows = 48
    cond = jax.nn.silu(jnp.concatenate([c_prompt, c_sample], axis=0))
    cond = jnp.concatenate([cond, jnp.zeros((pad_rows - N_P - N_S, D), F32)], axis=0).astype(BF16)
    x = jnp.tile(cond, (n_sub, 1))
    mod = _gmm(x, ada_w.reshape(n_sub, D, 3 * D), jnp.arange(n_sub, dtype=jnp.int32),
               jnp.full((1,), n_sub, jnp.int32), tm=pad_rows, tn=512, n_out=3 * D, out_dtype=F32, name="ada")
    mod = mod.reshape(n_sub, pad_rows, 3 * D) + ada_b.reshape(n_sub, 1, 3 * D)
    res = []
    for s in range(n_sub):
        modp = mod[s, :N_P].reshape(N_P, 1, 3 * D)
        mods = jnp.repeat(mod[s, N_P:N_P + N_S], T_S, axis=0)
        res.append((modp, mods))
    return res


def kernel(x_prompt, x_sample, c_prompt, c_sample, cache_a_conv, cache_b_kv0, cache_b_kv1, cache_b_kv2, cache_c_kv, norm_g, ada_w, ada_b, rel_bias, a_w_in, a_conv_w, a_w_out, b_w_qkv, b_w_o, c_w_qkv, c_sink, c_w_o, d_w_uv, d_w_s, d_b_s, d_w_out, f_w_gate, f_w_up, f_w_down, m_w_router, m_b_router, m_w_gate, m_w_up, m_w_down):
    mods = _modulations(c_prompt, c_sample, ada_w, ada_b)
    n_moe = m_w_gate.shape[0]
    mw_gate = m_w_gate.reshape((n_moe * N_EXPERTS,) + m_w_gate.shape[2:])
    mw_up = m_w_up.reshape((n_moe * N_EXPERTS,) + m_w_up.shape[2:])
    mw_down = m_w_down.reshape((n_moe * N_EXPERTS,) + m_w_down.shape[2:])

    def sub(i, j):
        return norm_g[i, 2 * j], norm_g[i, 2 * j + 1], mods[2 * i + j]

    def router(fi):
        wr = jnp.zeros((D, LANE), F32).at[:, :N_EXPERTS].set(m_w_router[fi])
        br = jnp.zeros((1, LANE), F32).at[0, :N_EXPERTS].set(m_b_router[fi])
        return wr, br

    x = (x_prompt.reshape(ROWS_P, D), x_sample.reshape(ROWS_S, D))
    g_in, _, (modp, modsr) = sub(0, 0)
    (h,) = _edge(x, nrm=(g_in, modp, modsr), name="norm_mix0")
    state = {}
    for i in range(DEPTH):
        kind = i % 4
        if kind == 0:
            out, state["a"] = _mixer_conv(h, cache_a_conv[0], a_w_in, a_conv_w[0], a_w_out)
        elif kind == 1:
            out, state["b"] = _mixer_dilated(h, (cache_b_kv0[0], cache_b_kv1[0], cache_b_kv2[0]),
                                             b_w_qkv, b_w_o, rel_bias)
        elif kind == 2:
            out, state["c"] = _mixer_swa(h, cache_c_kv[0], c_w_qkv, c_sink[0], c_w_o, rel_bias)
        else:
            out, state["d"] = _mixer_sgu(h, d_w_uv, d_w_s[0], d_b_s[0], d_w_out)

        _, g_out, (modp, modsr) = sub(i, 0)
        g_in, _, (modp2, modsr2) = sub(i, 1)
        fi = i // 2
        moe = i % 2 == 1
        res = _edge(x, y=out, res=(g_out, modp, modsr), nrm=(g_in, modp2, modsr2),
                    router=router(fi) if moe else None, name=f"edge_mix{i}")
        x, h = res[0], res[1]
        if moe:
            route = res[2]
            y = _moe(h, route, mw_gate, mw_up, mw_down, fi, name=f"moe{i}")
        else:
            route = None
            be, nu = _single(ROWS, DENSE_TM, fi)
            act = _gmm_gated(h, f_w_gate, f_w_up, be, nu, tm=DENSE_TM, tn=512, name=f"ffn_up{i}")
            y = _dense(act, f_w_down, tn=512, out_dtype=F32, name=f"ffn_down{i}", which=fi)

        _, g_out, (modp, modsr) = sub(i, 1)
        if i + 1 < DEPTH:
            g_in, _, (modp2, modsr2) = sub(i + 1, 0)
            x, h = _edge(x, y=y, route_in=route, res=(g_out, modp, modsr), nrm=(g_in, modp2, modsr2),
                         name=f"edge_ffn{i}")
        else:
            y_prompt, y_sample = _edge(x, y=y, route_in=route, res=(g_out, modp, modsr), split_out=True,
                                       name=f"edge_ffn{i}")

    (a_p, a_s) = state["a"]
    rows_p, rows_s = state["b"]
    c_p, c_s = state["c"]
    return (y_prompt.reshape(N_P, T_P, D), y_sample.reshape(N_S, T_S, D), a_p[None], a_s[None],
            rows_p[0], rows_s[0], rows_p[1], rows_s[1], rows_p[2], rows_s[2],
            c_p, c_s, state["d"])
```

```python
import functools
import math

import numpy as np
import jax
import jax.numpy as jnp
from jax import lax
from jax.experimental import pallas as pl
from jax.experimental.pallas import tpu as pltpu

D = 2048
N_P, T_P = 4, 2048
N_S, T_S = 32, 8
ROWS_P = N_P * T_P
ROWS_S = N_S * T_S
ROWS = ROWS_P + ROWS_S
DEPTH = 4
HEAD_DIM = 64
N_HEADS = 24
ATTN_W = N_HEADS * HEAD_DIM
DIL_CFG = ((128, 1), (512, 4), (2048, 16))
N_DIL = len(DIL_CFG)
B_HG = 8
GROUP_W = B_HG * HEAD_DIM
C_WINDOW = 128
C_KV_HEADS = 4
C_GROUP = 6
C_KW = C_KV_HEADS * HEAD_DIM
BLK = 128
SG_GROUPS = 8
SG_GW = D // SG_GROUPS
REL_BUCKETS = 32
REL_MAX_DIST = 2048
N_EXPERTS = 8
RMS_EPS = 1e-6
LN_EPS = 1e-5
NEG = -1e30

LANE = 128
VMEM_LIMIT_CAP = 58 * 1024 * 1024
VMEM_LIMIT_FLOOR = 32 * 1024 * 1024
VMEM_MARGIN = 8 * 1024 * 1024

ROW_BLK = 256
N_ROW_BLKS = ROWS // ROW_BLK
N_PROMPT_BLKS = ROWS_P // ROW_BLK
BLKS_PER_SEQ = T_P // ROW_BLK
SAMPLE_BLK8 = ROWS_P // T_S
DENSE_TM = 768
DENSE_TN = 1024
MOE_TM = 512
MOE_DOWN_TM = 512
GATHER_CHUNK = 512
GATHER_UNROLL = 8

BF16 = jnp.bfloat16
F32 = jnp.float32


def _params(sem, nbytes):
    limit = int(min(max(nbytes + VMEM_MARGIN, VMEM_LIMIT_FLOOR), VMEM_LIMIT_CAP))
    return pltpu.CompilerParams(dimension_semantics=sem, vmem_limit_bytes=limit)


def _weight_changed(be_ref):
    b = pl.program_id(1)
    prev = be_ref[jnp.maximum(b - 1, 0)]
    return jnp.logical_or(b == 0, be_ref[b] != prev)


def _gmm_kernel(be_ref, nu_ref, x_ref, w_ref, o_ref, wc_ref, *, act):
    used = pl.program_id(1) < nu_ref[0]

    @pl.when(used)
    def _():
        @pl.when(_weight_changed(be_ref))
        def _():
            wc_ref[...] = w_ref[...].astype(BF16)

        acc = jnp.dot(x_ref[...].astype(BF16), wc_ref[...], preferred_element_type=F32)
        if act == "gelu":
            acc = jax.nn.gelu(acc)
        o_ref[...] = acc.astype(o_ref.dtype)

    @pl.when(jnp.logical_not(used))
    def _():
        o_ref[...] = jnp.zeros_like(o_ref)


def _gmm_gated_kernel(be_ref, nu_ref, x_ref, wg_ref, wu_ref, o_ref, wgc_ref, wuc_ref):
    used = pl.program_id(1) < nu_ref[0]

    @pl.when(used)
    def _():
        @pl.when(_weight_changed(be_ref))
        def _():
            wgc_ref[...] = wg_ref[...].astype(BF16)
            wuc_ref[...] = wu_ref[...].astype(BF16)

        x = x_ref[...].astype(BF16)
        g = jnp.dot(x, wgc_ref[...], preferred_element_type=F32)
        u = jnp.dot(x, wuc_ref[...], preferred_element_type=F32)
        o_ref[...] = (jax.nn.silu(g) * u).astype(o_ref.dtype)

    @pl.when(jnp.logical_not(used))
    def _():
        o_ref[...] = jnp.zeros_like(o_ref)


def _gmm_specs(tm, k, tn, n_w, col_map):
    def x_map(j, b, be, nu):
        return (jnp.minimum(b, nu[0] - 1), 0)

    def w_map(j, b, be, nu):
        return (be[jnp.minimum(b, nu[0] - 1)], 0, col_map(j))

    in_specs = [pl.BlockSpec((tm, k), x_map)] + [pl.BlockSpec((None, k, tn), w_map)] * n_w
    out_spec = pl.BlockSpec((tm, tn), lambda j, b, be, nu: (b, j))
    return in_specs, out_spec


def _gmm(x, w, block_expert, n_used, *, tm, tn, n_out, out_dtype, act=None, col_map=lambda j: j, name):
    m, k = x.shape
    assert k == w.shape[1] and m % tm == 0 and n_out % tn == 0
    in_specs, out_spec = _gmm_specs(tm, k, tn, 1, col_map)
    nbytes = (2 * tm * k * x.dtype.itemsize + 2 * k * tn * 4 + k * tn * 2
              + 2 * tm * tn * jnp.dtype(out_dtype).itemsize)
    return pl.pallas_call(
        functools.partial(_gmm_kernel, act=act),
        out_shape=jax.ShapeDtypeStruct((m, n_out), out_dtype),
        grid_spec=pltpu.PrefetchScalarGridSpec(
            num_scalar_prefetch=2,
            grid=(n_out // tn, m // tm),
            in_specs=in_specs,
            out_specs=out_spec,
            scratch_shapes=[pltpu.VMEM((k, tn), BF16)],
        ),
        compiler_params=_params(("arbitrary", "arbitrary"), nbytes),
        name=name,
    )(block_expert, n_used, x, w)


def _gmm_gated(x, wg, wu, block_expert, n_used, *, tm, tn, name):
    m, k = x.shape
    n = wg.shape[2]
    assert k == wg.shape[1] and m % tm == 0 and n % tn == 0 and wu.shape == wg.shape
    in_specs, out_spec = _gmm_specs(tm, k, tn, 2, lambda j: j)
    nbytes = 2 * tm * k * x.dtype.itemsize + 4 * k * tn * 4 + 2 * k * tn * 2 + 2 * tm * tn * 2
    return pl.pallas_call(
        _gmm_gated_kernel,
        out_shape=jax.ShapeDtypeStruct((m, n), BF16),
        grid_spec=pltpu.PrefetchScalarGridSpec(
            num_scalar_prefetch=2,
            grid=(n // tn, m // tm),
            in_specs=in_specs,
            out_specs=out_spec,
            scratch_shapes=[pltpu.VMEM((k, tn), BF16), pltpu.VMEM((k, tn), BF16)],
        ),
        compiler_params=_params(("arbitrary", "arbitrary"), nbytes),
        name=name,
    )(block_expert, n_used, x, wg, wu)


def _single(m, tm, which=0):
    return jnp.full((m // tm,), which, jnp.int32), jnp.full((1,), m // tm, jnp.int32)


def _dense(x, w, *, tn, out_dtype, act=None, name, tm=DENSE_TM, which=0, n_out=None, col_map=lambda j: j):
    be, nu = _single(x.shape[0], tm, which)
    return _gmm(x, w, be, nu, tm=tm, tn=tn, n_out=n_out or w.shape[2], out_dtype=out_dtype, act=act,
                col_map=col_map, name=name)


def _rms(x, g):
    return x * lax.rsqrt(jnp.mean(x * x, axis=-1, keepdims=True) + RMS_EPS) * g


def _route(h, wr_ref, br_ref):
    logits = jnp.dot(h, wr_ref[...], preferred_element_type=F32, precision=lax.Precision.HIGHEST) + br_ref[...]
    lane = lax.broadcasted_iota(jnp.int32, logits.shape, 1)
    logits = jnp.where(lane < N_EXPERTS, logits, -jnp.inf)
    m1 = jnp.max(logits, axis=-1, keepdims=True)
    i1 = jnp.min(jnp.where(logits == m1, lane, LANE), axis=-1, keepdims=True)
    rest = jnp.where(lane == i1, -jnp.inf, logits)
    m2 = jnp.max(rest, axis=-1, keepdims=True)
    i2 = jnp.min(jnp.where(rest == m2, lane, LANE), axis=-1, keepdims=True)
    e2 = jnp.exp(m2 - m1)
    g1 = 1.0 / (1.0 + e2)
    g2 = e2 / (1.0 + e2)
    out = jnp.where(lane == 0, i1.astype(F32),
                    jnp.where(lane == 1, i2.astype(F32),
                              jnp.where(lane == 2, g1, jnp.where(lane == 3, g2, 0.0))))
    return out


def _edge_kernel(*refs, split_in, resid, moe_in, norm, route, split_out):
    refs = list(refs)

    def pop(n):
        return [refs.pop(0) for _ in range(n)]

    x_refs = pop(2) if split_in else pop(1) * 2
    if resid:
        y_refs = pop(3) if moe_in else pop(1)
        gr_ref, gate_p, gate_s = pop(3)
    if norm:
        gn_ref, shift_p, scale_p, shift_s, scale_s = pop(5)
        if route:
            wr_ref, br_ref = pop(2)
    if resid:
        xo_refs = pop(2) if split_out else pop(1) * 2
    if norm:
        (h_ref,) = pop(1)
        if route:
            (r_ref,) = pop(1)

    def run(sample):
        x = x_refs[sample][...]
        if resid:
            if moe_in:
                ya_ref, yb_ref, rp_ref = y_refs
                rp = rp_ref[...]
                out = rp[:, 2:3] * ya_ref[...] + rp[:, 3:4] * yb_ref[...]
            else:
                out = y_refs[0][...]
            x = x + (gate_s if sample else gate_p)[...] * _rms(out, gr_ref[...])
            xo_refs[sample][...] = x
        if norm:
            shift = (shift_s if sample else shift_p)[...]
            scale = (scale_s if sample else scale_p)[...]
            h = _rms(x, gn_ref[...]) * (1.0 + scale) + shift
            h_ref[...] = h.astype(h_ref.dtype)
            if route:
                r_ref[...] = _route(h, wr_ref, br_ref)

    i = pl.program_id(0)
    pl.when(i < N_PROMPT_BLKS)(functools.partial(run, 0))
    pl.when(i == N_PROMPT_BLKS)(functools.partial(run, 1))


def _mod_specs(cols):
    specs = []
    for c in cols:
        specs.append(pl.BlockSpec((None, 1, D), lambda i, c=c: (jnp.minimum(i // BLKS_PER_SEQ, N_P - 1), 0, c)))
    for c in cols:
        specs.append(pl.BlockSpec((ROW_BLK, D), lambda i, c=c: (0, c)))
    return specs


def _edge(x, *, y=None, route_in=None, res=None, nrm=None, router=None, split_out=False, name):
    split_in = isinstance(x, tuple)
    resid, norm, moe_in, route = res is not None, nrm is not None, route_in is not None, router is not None
    row = pl.BlockSpec((ROW_BLK, D), lambda i: (i, 0))
    row_p = pl.BlockSpec((ROW_BLK, D), lambda i: (jnp.minimum(i, N_PROMPT_BLKS - 1), 0))
    row_s = pl.BlockSpec((ROW_BLK, D), lambda i: (0, 0))
    gain = pl.BlockSpec((1, D), lambda i: (0, 0))
    lane_row = pl.BlockSpec((ROW_BLK, LANE), lambda i: (i, 0))
    in_specs = [row_p, row_s] if split_in else [row]
    args = list(x) if split_in else [x]
    out_shape, out_specs = [], []
    if resid:
        g, modp, mods = res
        if moe_in:
            in_specs += [row, pl.BlockSpec((ROW_BLK, D), lambda i: (i + N_ROW_BLKS, 0)), lane_row]
            args += [y, y, route_in]
        else:
            in_specs += [row]
            args += [y]
        in_specs += [gain] + _mod_specs((2,))
        args += [g.reshape(1, D), modp, mods]
        if split_out:
            out_shape += [jax.ShapeDtypeStruct((ROWS_P, D), F32), jax.ShapeDtypeStruct((ROWS_S, D), F32)]
            out_specs += [row_p, row_s]
        else:
            out_shape += [jax.ShapeDtypeStruct((ROWS, D), F32)]
            out_specs += [row]
    if norm:
        g, modp, mods = nrm
        in_specs += [gain] + _mod_specs((0, 1))
        args += [g.reshape(1, D), modp, modp, mods, mods]
        if route:
            in_specs += [pl.BlockSpec((D, LANE), lambda i: (0, 0)), pl.BlockSpec((1, LANE), lambda i: (0, 0))]
            args += list(router)
        out_shape += [jax.ShapeDtypeStruct((ROWS, D), BF16)]
        out_specs += [row]
        if route:
            out_shape += [jax.ShapeDtypeStruct((ROWS, LANE), F32)]
            out_specs += [lane_row]
    n_blocks = len(in_specs) + len(out_specs)
    return pl.pallas_call(
        functools.partial(_edge_kernel, split_in=split_in, resid=resid, moe_in=moe_in, norm=norm, route=route,
                          split_out=split_out),
        out_shape=out_shape,
        grid=(N_ROW_BLKS,),
        in_specs=in_specs,
        out_specs=out_specs,
        compiler_params=_params(("arbitrary",), n_blocks * ROW_BLK * D * 4 * 2),
        name=name,
    )(*args)


def _conv_kernel(gb_ref, gc_ref, z_ref, cw_ref, b1_ref, b2_ref, y_ref, stp_ref, us_ref, tail_ref):
    i = pl.program_id(0)
    u = gc_ref[...] * z_ref[...]
    r = lax.broadcasted_iota(jnp.int32, u.shape, 0)
    roll1 = pltpu.roll(u, 1, axis=0)
    roll2 = pltpu.roll(u, 2, axis=0)

    def emit(um1, um2):
        conv = cw_ref[0:1, :] * um2 + cw_ref[1:2, :] * um1 + cw_ref[2:3, :] * u
        y_ref[...] = (gb_ref[...] * conv).astype(y_ref.dtype)

    @pl.when(i < N_PROMPT_BLKS)
    def _():
        @pl.when(i % BLKS_PER_SEQ == 0)
        def _():
            tail_ref[...] = jnp.zeros_like(tail_ref)

        prev2 = tail_ref[6:7, :]
        prev1 = tail_ref[7:8, :]
        emit(jnp.where(r == 0, prev1, roll1),
             jnp.where(r == 0, prev2, jnp.where(r == 1, prev1, roll2)))
        tail_ref[...] = u[ROW_BLK - 8:ROW_BLK, :]
        stp_ref[...] = u[ROW_BLK - 2:ROW_BLK, :]

    @pl.when(i == N_PROMPT_BLKS)
    def _():
        t = r % T_S
        emit(jnp.where(t == 0, b1_ref[...], roll1), jnp.where(t < 2, b2_ref[...], roll2))
        us_ref[...] = u


def _short_conv(bcz, cache, conv_w, *, name):
    zeros = jnp.zeros((N_S, T_S, D), F32)
    b1 = zeros.at[:, 0].set(cache[:, 1]).reshape(ROWS_S, D)
    b2 = zeros.at[:, 0].set(cache[:, 0]).at[:, 1].set(cache[:, 1]).reshape(ROWS_S, D)

    def col(c):
        return pl.BlockSpec((ROW_BLK, D), lambda i, c=c: (i, c))

    const = pl.BlockSpec((ROW_BLK, D), lambda i: (0, 0))
    y, st_p, u_s = pl.pallas_call(
        _conv_kernel,
        out_shape=[jax.ShapeDtypeStruct((ROWS, D), BF16), jax.ShapeDtypeStruct((N_P, 2, D), F32),
                   jax.ShapeDtypeStruct((ROWS_S, D), F32)],
        grid=(N_ROW_BLKS,),
        in_specs=[col(0), col(1), col(2), pl.BlockSpec((3, D), lambda i: (0, 0)), const, const],
        out_specs=[pl.BlockSpec((ROW_BLK, D), lambda i: (i, 0)),
                   pl.BlockSpec((None, 2, D), lambda i: (jnp.minimum(i // BLKS_PER_SEQ, N_P - 1), 0, 0)),
                   const],
        scratch_shapes=[pltpu.VMEM((8, D), F32)],
        compiler_params=_params(("arbitrary",), ROW_BLK * D * 4 * 7 * 2),
        name=name,
    )(bcz, bcz, bcz, conv_w, b1, b2)
    return y, st_p, u_s.reshape(N_S, T_S, D)[:, T_S - 2:]


def _rel_bucket(dist):
    n = np.maximum(np.asarray(dist), 0)
    exact = REL_BUCKETS // 2
    ratio = np.log(np.maximum(n, 1).astype(np.float32) / exact) / math.log(REL_MAX_DIST / exact)
    large = np.minimum(exact + (ratio * (REL_BUCKETS - exact)).astype(np.int32), REL_BUCKETS - 1)
    return np.where(n < exact, n, large).astype(np.int32)


def _toeplitz(a, n_rows, n_cols):
    h, p = a.shape
    assert n_cols < p
    t = jnp.tile(a, (1, n_rows))[:, :n_rows * (p - 1)].reshape(h, n_rows, p - 1)
    return t[:, :, :n_cols]


def _bias_line(rel_bias, heads, dist, valid):
    tab = rel_bias[_rel_bucket(dist)][:, heads]
    return jnp.where(valid[:, None], tab, NEG).T.astype(F32)


def _band_bias(rel_bias, heads, dil, max_dist):
    p = 3 * BLK
    m = np.arange(p)
    dist = np.where(m < 2 * BLK, BLK - m, BLK + p - m)
    valid = (dist >= 0) & (dist <= max_dist)
    return _toeplitz(_bias_line(rel_bias * LOG2E, heads, dist * dil, valid), BLK, 2 * BLK)


def _sample_bias(rel_bias, heads, win, dil, max_dist):
    lk = win + BLK
    p = lk + T_S
    m = np.arange(p)
    dist = np.where(m < lk, win - m, win + p - m)
    valid = (dist >= 0) & (dist % dil == 0) & (dist <= max_dist)
    return _toeplitz(_bias_line(rel_bias, heads, dist, valid), T_S, lk)


LOG2E = math.log2(math.e)
LN2 = math.log(2.0)


def _softmax_pv(s, v, sink=None):
    m = jnp.max(s, axis=-1, keepdims=True)
    if sink is not None:
        m = jnp.maximum(m, sink)
    p = jnp.exp2(s - m)
    l = jnp.sum(p, axis=-1, keepdims=True)
    if sink is not None:
        l = l + jnp.exp2(sink - m)
    o = jnp.dot(p.astype(BF16), v, preferred_element_type=F32) / l
    return o, (m + jnp.log2(l)) * LN2


def _nt_dot(a, b):
    return lax.dot_general(a, b, (((1,), (1,)), ((), ())), preferred_element_type=F32)


def _banded_kernel(*refs, n_seq, **kw):
    out_refs = refs[-2:] if kw["with_lse"] else refs[-1:]

    @pl.when(pl.program_id(0) < n_seq)
    def _():
        _banded_body(*refs, **kw)

    @pl.when(pl.program_id(0) >= n_seq)
    def _():
        for ref in out_refs:
            ref[...] = jnp.zeros_like(ref)


def _banded_body(*refs, n_heads, group, with_lse, has_sink):
    refs = list(refs)
    sink_ref = refs.pop(0) if has_sink else None
    q_ref, kp_ref, kc_ref, vp_ref, vc_ref, bias_ref = refs[:6]
    o_ref = refs[6]
    lse_ref = refs[7] if with_lse else None
    q = q_ref[...] * (HEAD_DIM ** -0.5 * LOG2E)
    k = jnp.concatenate([kp_ref[...], kc_ref[...]], axis=0).astype(BF16)
    v = jnp.concatenate([vp_ref[...], vc_ref[...]], axis=0).astype(BF16)
    col = lax.broadcasted_iota(jnp.int32, (BLK, 2 * BLK), 1)
    no_prev = jnp.logical_and(pl.program_id(1) == 0, col < BLK)
    low = lax.broadcasted_iota(jnp.int32, (BLK, LANE), 1) < HEAD_DIM
    for t in range(n_heads // 2):
        tile = slice(t * LANE, (t + 1) * LANE)
        q_tile = q[:, tile]
        outs, lses = [], []
        for half in range(2):
            h = 2 * t + half
            hk = h // group
            k_tile = slice(hk // 2 * LANE, (hk // 2 + 1) * LANE)
            k_half = hk % 2
            qh = q_tile if k_half == half else pltpu.roll(q_tile, HEAD_DIM, axis=1)
            qh = jnp.where(low if k_half == 0 else jnp.logical_not(low), qh, 0.0).astype(BF16)
            s = _nt_dot(qh, k[:, k_tile]) + bias_ref[h]
            s = jnp.where(no_prev, NEG, s)
            sink = sink_ref[h] if has_sink else None
            o, lse = _softmax_pv(s, v[:, k_tile], sink)
            outs.append(o if k_half == half else pltpu.roll(o, HEAD_DIM, axis=1))
            lses.append(lse)
        o_ref[:, tile] = jnp.where(low, outs[0], outs[1]).astype(o_ref.dtype)
        if with_lse:
            lse_ref[:, tile] = jnp.where(low, lses[0], lses[1])


def _banded_attention(src, bias, sink=None, *, n_seq, q_col, k_col, v_col, qw, kw, out_dtype, group, with_lse, name):
    n_heads = qw // HEAD_DIM
    has_sink = sink is not None
    n_blk = ROWS_P // BLK // n_seq
    fill_row0 = ROWS_P // BLK
    fill_blks = ROWS_S // BLK
    n_fill_seq = -(-fill_blks // n_blk)

    def cur(col):
        return lambda s, b: (jnp.minimum(s, n_seq - 1) * n_blk + b, col)

    def prev(col):
        return lambda s, b: (jnp.minimum(s, n_seq - 1) * n_blk + jnp.maximum(b - 1, 0), col)

    def out_map(s, b):
        fill = jnp.minimum((s - n_seq) * n_blk + b, fill_blks - 1)
        return (jnp.where(s >= n_seq, fill_row0 + fill, s * n_blk + b), 0)

    in_specs = [pl.BlockSpec((BLK, qw), cur(q_col)),
                pl.BlockSpec((BLK, kw), prev(k_col)), pl.BlockSpec((BLK, kw), cur(k_col)),
                pl.BlockSpec((BLK, kw), prev(v_col)), pl.BlockSpec((BLK, kw), cur(v_col)),
                pl.BlockSpec((n_heads, BLK, 2 * BLK), lambda s, b: (0, 0, 0))]
    args = [src, src, src, src, src, bias]
    if has_sink:
        in_specs = [pl.BlockSpec(memory_space=pltpu.SMEM)] + in_specs
        args = [sink] + args
    out_shapes = [jax.ShapeDtypeStruct((ROWS, qw), out_dtype)]
    out_specs = [pl.BlockSpec((BLK, qw), out_map)]
    if with_lse:
        out_shapes.append(jax.ShapeDtypeStruct((ROWS, qw), F32))
        out_specs.append(pl.BlockSpec((BLK, qw), out_map))
    nbytes = (BLK * qw * 4 * 3 + BLK * kw * 4 * 4) * 2 + n_heads * BLK * 2 * BLK * 4 * 2
    res = pl.pallas_call(
        functools.partial(_banded_kernel, n_seq=n_seq, n_heads=n_heads, group=group, with_lse=with_lse,
                          has_sink=has_sink),
        out_shape=out_shapes,
        grid=(n_seq + n_fill_seq, n_blk),
        in_specs=in_specs,
        out_specs=out_specs,
        compiler_params=_params(("arbitrary", "arbitrary"), nbytes),
        name=name,
    )(*args)
    return res if with_lse else res[0]


def _dil_sample_kernel(q_ref, kn_ref, vn_ref, cache_ref, bc_ref, bn_ref, o_in, lse_in, o_ref, lse_ref):
    del o_in, lse_in
    q = q_ref[...] * (HEAD_DIM ** -0.5)
    qt = jnp.concatenate([q] * B_HG, axis=0)
    row_h = lax.broadcasted_iota(jnp.int32, qt.shape, 0) // T_S
    col_h = lax.broadcasted_iota(jnp.int32, qt.shape, 1) // HEAD_DIM
    qbd = jnp.where(row_h == col_h, qt, 0.0).astype(BF16)
    pad = jnp.zeros((BLK - T_S, GROUP_W), F32)
    kn = jnp.concatenate([kn_ref[...], pad], axis=0).astype(BF16)
    vn = jnp.concatenate([vn_ref[...], pad], axis=0).astype(BF16)
    kc = cache_ref[:, 0:GROUP_W].astype(BF16)
    vc = cache_ref[:, GROUP_W:2 * GROUP_W].astype(BF16)
    sc = _nt_dot(qbd, kc) + bc_ref[...]
    sn = _nt_dot(qbd, kn) + bn_ref[...]
    m = jnp.maximum(jnp.max(sc, axis=-1, keepdims=True), jnp.max(sn, axis=-1, keepdims=True))
    pc = jnp.exp(sc - m)
    pn = jnp.exp(sn - m)
    l = jnp.sum(pc, axis=-1, keepdims=True) + jnp.sum(pn, axis=-1, keepdims=True)
    big = (jnp.dot(pc.astype(BF16), vc, preferred_element_type=F32)
           + jnp.dot(pn.astype(BF16), vn, preferred_element_type=F32)) / l
    lse = m + jnp.log(l)
    for h in range(B_HG):
        rs = slice(h * T_S, (h + 1) * T_S)
        cs = slice(h * HEAD_DIM, (h + 1) * HEAD_DIM)
        o_ref[:, cs] = big[rs, cs]
        lse_ref[:, cs] = jnp.broadcast_to(lse[rs], (T_S, HEAD_DIM))


def _dil_sample_attention(qkv, cache, bias_c, bias_n, o, lse, *, name):
    lb = cache.shape[1]
    row = lambda c: pl.BlockSpec((T_S, GROUP_W), lambda n, c=c: (SAMPLE_BLK8 + n, c))
    const = lambda n: (0, 0)
    anyspec = pl.BlockSpec(memory_space=pl.ANY)
    nbytes = lb * 2 * GROUP_W * 4 * 2 + 64 * lb * 4 * 6
    return pl.pallas_call(
        _dil_sample_kernel,
        out_shape=[jax.ShapeDtypeStruct((ROWS, GROUP_W), F32)] * 2,
        grid=(N_S,),
        in_specs=[row(0), row(1), row(2),
                  pl.BlockSpec((None, lb, 2 * GROUP_W), lambda n: (n, 0, 0)),
                  pl.BlockSpec((B_HG * T_S, lb), const), pl.BlockSpec((B_HG * T_S, BLK), const),
                  anyspec, anyspec],
        out_specs=[row(0)] * 2,
        input_output_aliases={6: 0, 7: 1},
        compiler_params=_params(("parallel",), nbytes),
        name=name,
    )(qkv, qkv, qkv, cache, bias_c, bias_n, o, lse)


def _merge_kernel(o0, o1, o2, l0, l1, l2, out_ref):
    ls = [l0[...], l1[...], l2[...]]
    m = jnp.maximum(jnp.maximum(ls[0], ls[1]), ls[2])
    es = [jnp.exp(l - m) for l in ls]
    tot = es[0] + es[1] + es[2]
    for g, o in enumerate((o0, o1, o2)):
        out_ref[:, g * GROUP_W:(g + 1) * GROUP_W] = (o[...] * (es[g] / tot)).astype(out_ref.dtype)


def _merge_dilations(outs, lses, *, name):
    spec = pl.BlockSpec((ROW_BLK, GROUP_W), lambda i: (i, 0))
    return pl.pallas_call(
        _merge_kernel,
        out_shape=jax.ShapeDtypeStruct((ROWS, ATTN_W), BF16),
        grid=(N_ROW_BLKS,),
        in_specs=[spec] * 6,
        out_specs=pl.BlockSpec((ROW_BLK, ATTN_W), lambda i: (i, 0)),
        compiler_params=_params(("parallel",), ROW_BLK * GROUP_W * 4 * 8 * 2),
        name=name,
    )(*outs, *lses)


def _swa_sample_kernel(q_ref, kvn_ref, cache_ref, bc_ref, bn_ref, sink_ref, o_in, o_ref):
    del o_in
    q = q_ref[...] * (HEAD_DIM ** -0.5)
    pad = jnp.zeros((BLK - T_S, 2 * C_KW), F32)
    kvn = jnp.concatenate([kvn_ref[...], pad], axis=0).astype(BF16)
    kvc = cache_ref[...].astype(BF16)
    for hk in range(C_KV_HEADS):
        heads = range(hk * C_GROUP, (hk + 1) * C_GROUP)
        qh = jnp.concatenate([q[:, h * HEAD_DIM:(h + 1) * HEAD_DIM] for h in heads], axis=0).astype(BF16)
        ks = slice(hk * HEAD_DIM, (hk + 1) * HEAD_DIM)
        vs = slice(C_KW + hk * HEAD_DIM, C_KW + (hk + 1) * HEAD_DIM)
        sc = _nt_dot(qh, kvc[:, ks]) + bc_ref[hk]
        sn = _nt_dot(qh, kvn[:, ks]) + bn_ref[hk]
        sink = sink_ref[hk]
        m = jnp.maximum(jnp.maximum(jnp.max(sc, axis=-1, keepdims=True),
                                    jnp.max(sn, axis=-1, keepdims=True)), sink)
        pc = jnp.exp(sc - m)
        pn = jnp.exp(sn - m)
        l = jnp.sum(pc, axis=-1, keepdims=True) + jnp.sum(pn, axis=-1, keepdims=True) + jnp.exp(sink - m)
        o = (jnp.dot(pc.astype(BF16), kvc[:, vs], preferred_element_type=F32)
             + jnp.dot(pn.astype(BF16), kvn[:, vs], preferred_element_type=F32)) / l
        for gi, h in enumerate(heads):
            o_ref[:, h * HEAD_DIM:(h + 1) * HEAD_DIM] = o[gi * T_S:(gi + 1) * T_S].astype(o_ref.dtype)


def _swa_sample_attention(qkv, cache, bias_c, bias_n, sink_tab, o, *, name):
    rows = C_GROUP * T_S
    const = lambda n: (0, 0, 0)
    return pl.pallas_call(
        _swa_sample_kernel,
        out_shape=jax.ShapeDtypeStruct((ROWS, ATTN_W), o.dtype),
        grid=(N_S,),
        in_specs=[pl.BlockSpec((T_S, ATTN_W), lambda n: (SAMPLE_BLK8 + n, 0)),
                  pl.BlockSpec((T_S, 2 * C_KW), lambda n: (SAMPLE_BLK8 + n, ATTN_W // (2 * C_KW))),
                  pl.BlockSpec((None, C_WINDOW, 2 * C_KW), lambda n: (n, 0, 0)),
                  pl.BlockSpec((C_KV_HEADS, rows, BLK), const), pl.BlockSpec((C_KV_HEADS, rows, BLK), const),
                  pl.BlockSpec((C_KV_HEADS, rows, 1), const),
                  pl.BlockSpec(memory_space=pl.ANY)],
        out_specs=pl.BlockSpec((T_S, ATTN_W), lambda n: (SAMPLE_BLK8 + n, 0)),
        input_output_aliases={6: 0},
        compiler_params=_params(("parallel",), 1 << 20),
        name=name,
    )(qkv, qkv, cache, bias_c, bias_n, sink_tab, o)


def _sgu_kernel(u_ref, v_ref, w_ref, b_ref, ws_ref, bs_ref, y_ref, vn_ref):
    i = pl.program_id(0)
    v = v_ref[...]
    mu = jnp.mean(v, axis=-1, keepdims=True)
    var = jnp.mean(jnp.square(v - mu), axis=-1, keepdims=True)
    vn = (v - mu) * lax.rsqrt(var + LN_EPS)
    vb = vn.astype(BF16)

    @pl.when(i < N_PROMPT_BLKS)
    def _():
        causal = (lax.broadcasted_iota(jnp.int32, (BLK, BLK), 0) >= lax.broadcasted_iota(jnp.int32, (BLK, BLK), 1))
        for g in range(SG_GROUPS):
            cs = slice(g * SG_GW, (g + 1) * SG_GW)
            wg = jnp.where(causal, w_ref[g], 0.0).astype(BF16)
            for c in range(ROW_BLK // BLK):
                rs = slice(c * BLK, (c + 1) * BLK)
                s = jnp.dot(wg, vb[rs, cs], preferred_element_type=F32) + b_ref[g]
                y_ref[rs, cs] = (u_ref[rs, cs] * s).astype(y_ref.dtype)

    @pl.when(i == N_PROMPT_BLKS)
    def _():
        vn_ref[...] = vn
        for g in range(SG_GROUPS):
            cs = slice(g * SG_GW, (g + 1) * SG_GW)
            s = jnp.dot(ws_ref[g].astype(BF16), vb[:, cs], preferred_element_type=F32) + bs_ref[g]
            y_ref[:, cs] = (u_ref[:, cs] * s).astype(y_ref.dtype)


def _spatial_gate(uv, w_s, b_s, *, name):
    r = np.arange(ROWS_S)
    same_seq = (r[:, None] // T_S) == (r[None, :] // T_S)
    causal = (r[:, None] % T_S) >= (r[None, :] % T_S)
    w8 = jnp.tile(w_s[:, :T_S, :T_S], (1, N_S, N_S))
    ws = jnp.where((same_seq & causal)[None], w8, 0.0)
    bs = jnp.tile(b_s[:, :T_S], (1, N_S)).reshape(SG_GROUPS, ROWS_S, 1)
    const3 = lambda i: (0, 0, 0)
    nbytes = ROW_BLK * D * 4 * 4 * 2 + SG_GROUPS * (BLK * BLK + ROW_BLK * ROW_BLK) * 4 * 2
    return pl.pallas_call(
        _sgu_kernel,
        out_shape=[jax.ShapeDtypeStruct((ROWS, D), BF16), jax.ShapeDtypeStruct((ROWS_S, D), F32)],
        grid=(N_ROW_BLKS,),
        in_specs=[pl.BlockSpec((ROW_BLK, D), lambda i: (i, 0)), pl.BlockSpec((ROW_BLK, D), lambda i: (i, 1)),
                  pl.BlockSpec((SG_GROUPS, BLK, BLK), const3), pl.BlockSpec((SG_GROUPS, BLK, 1), const3),
                  pl.BlockSpec((SG_GROUPS, ROWS_S, ROWS_S), const3), pl.BlockSpec((SG_GROUPS, ROWS_S, 1), const3)],
        out_specs=[pl.BlockSpec((ROW_BLK, D), lambda i: (i, 0)), pl.BlockSpec((ROWS_S, D), lambda i: (0, 0))],
        compiler_params=_params(("arbitrary",), nbytes),
        name=name,
    )(uv, uv, w_s, b_s.reshape(SG_GROUPS, BLK, 1), ws, bs)


def _gather_kernel(idx_ref, src_ref, o_ref, sem):
    base = pl.program_id(0) * GATHER_CHUNK

    def row_copy(r):
        return pltpu.make_async_copy(src_ref.at[idx_ref[base + r]], o_ref.at[r], sem)

    def start(g, carry):
        for u in range(GATHER_UNROLL):
            row_copy(g * GATHER_UNROLL + u).start()
        return carry

    def wait(g, carry):
        for u in range(GATHER_UNROLL):
            row_copy(g * GATHER_UNROLL + u).wait()
        return carry

    lax.fori_loop(0, GATHER_CHUNK // GATHER_UNROLL, start, 0)
    lax.fori_loop(0, GATHER_CHUNK // GATHER_UNROLL, wait, 0)


def _gather_rows(src, idx, *, name):
    n = src.shape[0]
    r = idx.shape[0]
    assert r % GATHER_CHUNK == 0
    slab = (D // LANE, LANE)
    out = pl.pallas_call(
        _gather_kernel,
        out_shape=jax.ShapeDtypeStruct((r,) + slab, src.dtype),
        grid_spec=pltpu.PrefetchScalarGridSpec(
            num_scalar_prefetch=1,
            grid=(r // GATHER_CHUNK,),
            in_specs=[pl.BlockSpec(memory_space=pl.ANY)],
            out_specs=pl.BlockSpec((GATHER_CHUNK,) + slab, lambda c, idx: (c, 0, 0)),
            scratch_shapes=[pltpu.SemaphoreType.DMA],
        ),
        compiler_params=_params(("arbitrary",), 2 * GATHER_CHUNK * D * src.dtype.itemsize),
        name=name,
    )(idx, src.reshape((n,) + slab))
    return out.reshape(r, D)


N_ASSIGN = 2 * ROWS
_MOE_ROWS_MIN = N_ASSIGN + N_EXPERTS * (MOE_TM - 1)
_MOE_UNIT = MOE_TM * GATHER_CHUNK // math.gcd(MOE_TM, GATHER_CHUNK)
MOE_ROWS = -(-_MOE_ROWS_MIN // _MOE_UNIT) * _MOE_UNIT
assert N_ASSIGN % GATHER_CHUNK == 0


def _moe_plan(route):
    e_flat = route[:, 0:2].astype(jnp.int32).reshape(-1)
    onehot = (e_flat[:, None] == jnp.arange(N_EXPERTS, dtype=jnp.int32)[None, :]).astype(jnp.int32)
    csum = jnp.cumsum(onehot, axis=0)
    rank = jnp.sum(csum * onehot, axis=1) - 1
    counts = csum[-1]
    padded = (counts + MOE_TM - 1) // MOE_TM * MOE_TM
    pad_end = jnp.cumsum(padded)
    pad_start = pad_end - padded
    dest = pad_start[e_flat] + rank
    token = jnp.arange(N_ASSIGN, dtype=jnp.int32) // 2
    src_row = (jnp.arange(MOE_ROWS, dtype=jnp.int32) % ROWS).at[dest].set(token)
    block_start = jnp.arange(MOE_ROWS // MOE_TM, dtype=jnp.int32) * MOE_TM
    block_expert = jnp.minimum(jnp.searchsorted(pad_end, block_start, side="right"), N_EXPERTS - 1).astype(jnp.int32)
    slots = jnp.concatenate([dest[0::2], dest[1::2]])
    n_used = (pad_end[-1:] // MOE_TM).astype(jnp.int32)
    return slots, src_row, block_expert, n_used


def _moe(h, route, w_gate, w_up, w_down, layer, *, name):
    slots, src_row, block_expert, n_used = _moe_plan(route)
    block_expert = block_expert + layer * N_EXPERTS
    xs = _gather_rows(h, src_row, name=name + "_dispatch")
    act = _gmm_gated(xs, w_gate, w_up, block_expert, n_used, tm=MOE_TM, tn=1024, name=name + "_up")
    split = MOE_TM // MOE_DOWN_TM
    ys = _gmm(act, w_down, jnp.repeat(block_expert, split), n_used * split, tm=MOE_DOWN_TM, tn=512, n_out=D,
              out_dtype=F32, name=name + "_down")
    return _gather_rows(ys, slots, name=name + "_combine")


def _mixer_conv(h, cache, w_in, conv_w, w_out):
    bcz = _dense(h, w_in, tn=DENSE_TN, out_dtype=F32, name="a_in")
    y, st_p, st_s = _short_conv(bcz, cache, conv_w, name="a_conv")
    out = _dense(y, w_out, tn=DENSE_TN, out_dtype=F32, name="a_out")
    return out, (st_p, st_s)


def _to_residues(a, dil):
    if dil == 1:
        return a
    w = a.shape[1]
    p = a[:ROWS_P].reshape(N_P, T_P // dil, dil, w).swapaxes(1, 2).reshape(ROWS_P, w)
    return jnp.concatenate([p, a[ROWS_P:]], axis=0)


def _from_residues(a, dil):
    if dil == 1:
        return a
    w = a.shape[1]
    p = a[:ROWS_P].reshape(N_P, dil, T_P // dil, w).swapaxes(1, 2).reshape(ROWS_P, w)
    return jnp.concatenate([p, a[ROWS_P:]], axis=0)


def _mixer_dilated(h, caches, w_qkv, w_o, rel_bias):
    outs, lses, rows_p, rows_s = [], [], [], []
    for gi, (win, dil) in enumerate(DIL_CFG):
        heads = slice(gi * B_HG, (gi + 1) * B_HG)
        qkv = _dense(_to_residues(h, dil), w_qkv, tn=GROUP_W, out_dtype=F32, n_out=3 * GROUP_W,
                     col_map=lambda j, gi=gi: N_DIL * j + gi, name=f"b_qkv{gi}")
        o, lse = _banded_attention(qkv, _band_bias(rel_bias, heads, dil, win // dil),
                                   n_seq=N_P * dil, q_col=0, k_col=1, v_col=2, qw=GROUP_W, kw=GROUP_W,
                                   out_dtype=F32, group=1, with_lse=True, name=f"b_attn_prompt{gi}")
        bias = _sample_bias(rel_bias, heads, win, dil, win).reshape(B_HG * T_S, win + BLK)
        bias_c, bias_n = bias[:, :win], bias[:, win:]
        cache = caches[gi].reshape(N_S, win, 2 * GROUP_W)
        o, lse = _dil_sample_attention(qkv, cache, bias_c, bias_n, o, lse, name=f"b_attn_sample{gi}")
        outs.append(_from_residues(o, dil))
        lses.append(_from_residues(lse, dil))
        keep = min(win, T_P)
        seq, tail = T_P // dil, keep // dil
        kv = jnp.stack([qkv[(s + 1) * seq - tail:(s + 1) * seq, GROUP_W:] for s in range(N_P * dil)])
        kv = kv.reshape(N_P, dil, tail, 2, GROUP_W)
        rows_p.append(kv.swapaxes(1, 2).reshape(1, N_P, keep, 2, B_HG, HEAD_DIM))
        rows_s.append(qkv[ROWS_P:].reshape(N_S, T_S, 3, GROUP_W)[:, :, 1:]
                      .reshape(1, N_S, T_S, 2, B_HG, HEAD_DIM))
    merged = _merge_dilations(outs, lses, name="b_merge")
    out = _dense(merged, w_o, tn=DENSE_TN, out_dtype=F32, name="b_out")
    return out, (rows_p, rows_s)


def _mixer_swa(h, cache, w_qkv, sink, w_o, rel_bias):
    qkv = _dense(h, w_qkv, tn=DENSE_TN, out_dtype=F32, name="c_qkv")
    heads = slice(0, N_HEADS)
    o = _banded_attention(qkv, _band_bias(rel_bias, heads, 1, C_WINDOW - 1), sink.astype(F32) * LOG2E,
                          n_seq=N_P, q_col=0, k_col=ATTN_W // C_KW, v_col=ATTN_W // C_KW + 1, qw=ATTN_W, kw=C_KW,
                          out_dtype=BF16, group=C_GROUP, with_lse=False, name="c_attn_prompt")
    bias = _sample_bias(rel_bias, heads, C_WINDOW, 1, C_WINDOW - 1)
    bias = bias.reshape(C_KV_HEADS, C_GROUP * T_S, 2 * BLK)
    sink_tab = jnp.repeat(sink.astype(F32).reshape(C_KV_HEADS, C_GROUP), T_S, axis=1)[:, :, None]
    o = _swa_sample_attention(qkv, cache.reshape(N_S, C_WINDOW, 2 * C_KW), bias[:, :, :BLK], bias[:, :, BLK:],
                              sink_tab, o, name="c_attn_sample")
    kv_p = jnp.stack([qkv[(n + 1) * T_P - C_WINDOW:(n + 1) * T_P, ATTN_W:] for n in range(N_P)])
    kv_p = kv_p.reshape(1, N_P, C_WINDOW, 2, C_KV_HEADS, HEAD_DIM)
    kv_s = qkv[ROWS_P:, ATTN_W:].reshape(1, N_S, T_S, 2, C_KV_HEADS, HEAD_DIM)
    out = _dense(o, w_o, tn=DENSE_TN, out_dtype=F32, name="c_out")
    return out, (kv_p, kv_s)


def _mixer_sgu(h, w_uv, w_s, b_s, w_out):
    uv = _dense(h, w_uv, tn=DENSE_TN, out_dtype=F32, act="gelu", name="d_uv")
    y, vn_s = _spatial_gate(uv, w_s, b_s, name="d_gate")
    out = _dense(y, w_out, tn=DENSE_TN, out_dtype=F32, name="d_out")
    return out, vn_s.reshape(1, N_S, T_S, D)


def _modulations(c_prompt, c_sample, ada_w, ada_b):
    n_sub = 2 * DEPTH
    pad_rows = 48
    cond = jax.nn.silu(jnp.concatenate([c_prompt, c_sample], axis=0))
    cond = jnp.concatenate([cond, jnp.zeros((pad_rows - N_P - N_S, D), F32)], axis=0).astype(BF16)
    x = jnp.tile(cond, (n_sub, 1))
    mod = _gmm(x, ada_w.reshape(n_sub, D, 3 * D), jnp.arange(n_sub, dtype=jnp.int32),
               jnp.full((1,), n_sub, jnp.int32), tm=pad_rows, tn=512, n_out=3 * D, out_dtype=F32, name="ada")
    mod = mod.reshape(n_sub, pad_rows, 3 * D) + ada_b.reshape(n_sub, 1, 3 * D)
    res = []
    for s in range(n_sub):
        modp = mod[s, :N_P].reshape(N_P, 1, 3 * D)
        mods = jnp.repeat(mod[s, N_P:N_P + N_S], T_S, axis=0)
        res.append((modp, mods))
    return res


def kernel(x_prompt, x_sample, c_prompt, c_sample, cache_a_conv, cache_b_kv0, cache_b_kv1, cache_b_kv2, cache_c_kv, norm_g, ada_w, ada_b, rel_bias, a_w_in, a_conv_w, a_w_out, b_w_qkv, b_w_o, c_w_qkv, c_sink, c_w_o, d_w_uv, d_w_s, d_b_s, d_w_out, f_w_gate, f_w_up, f_w_down, m_w_router, m_b_router, m_w_gate, m_w_up, m_w_down):
    mods = _modulations(c_prompt, c_sample, ada_w, ada_b)
    n_moe = m_w_gate.shape[0]
    mw_gate = m_w_gate.reshape((n_moe * N_EXPERTS,) + m_w_gate.shape[2:])
    mw_up = m_w_up.reshape((n_moe * N_EXPERTS,) + m_w_up.shape[2:])
    mw_down = m_w_down.reshape((n_moe * N_EXPERTS,) + m_w_down.shape[2:])

    def sub(i, j):
        return norm_g[i, 2 * j], norm_g[i, 2 * j + 1], mods[2 * i + j]

    def router(fi):
        wr = jnp.zeros((D, LANE), F32).at[:, :N_EXPERTS].set(m_w_router[fi])
        br = jnp.zeros((1, LANE), F32).at[0, :N_EXPERTS].set(m_b_router[fi])
        return wr, br

    x = (x_prompt.reshape(ROWS_P, D), x_sample.reshape(ROWS_S, D))
    g_in, _, (modp, modsr) = sub(0, 0)
    (h,) = _edge(x, nrm=(g_in, modp, modsr), name="norm_mix0")
    state = {}
    for i in range(DEPTH):
        kind = i % 4
        if kind == 0:
            out, state["a"] = _mixer_conv(h, cache_a_conv[0], a_w_in, a_conv_w[0], a_w_out)
        elif kind == 1:
            out, state["b"] = _mixer_dilated(h, (cache_b_kv0[0], cache_b_kv1[0], cache_b_kv2[0]),
                                             b_w_qkv, b_w_o, rel_bias)
        elif kind == 2:
            out, state["c"] = _mixer_swa(h, cache_c_kv[0], c_w_qkv, c_sink[0], c_w_o, rel_bias)
        else:
            out, state["d"] = _mixer_sgu(h, d_w_uv, d_w_s[0], d_b_s[0], d_w_out)

        _, g_out, (modp, modsr) = sub(i, 0)
        g_in, _, (modp2, modsr2) = sub(i, 1)
        fi = i // 2
        moe = i % 2 == 1
        res = _edge(x, y=out, res=(g_out, modp, modsr), nrm=(g_in, modp2, modsr2),
                    router=router(fi) if moe else None, name=f"edge_mix{i}")
        x, h = res[0], res[1]
        if moe:
            route = res[2]
            y = _moe(h, route, mw_gate, mw_up, mw_down, fi, name=f"moe{i}")
        else:
            route = None
            be, nu = _single(ROWS, DENSE_TM, fi)
            act = _gmm_gated(h, f_w_gate, f_w_up, be, nu, tm=DENSE_TM, tn=512, name=f"ffn_up{i}")
            y = _dense(act, f_w_down, tn=512, out_dtype=F32, name=f"ffn_down{i}", which=fi)

        _, g_out, (modp, modsr) = sub(i, 1)
        if i + 1 < DEPTH:
            g_in, _, (modp2, modsr2) = sub(i + 1, 0)
            x, h = _edge(x, y=y, route_in=route, res=(g_out, modp, modsr), nrm=(g_in, modp2, modsr2),
                         name=f"edge_ffn{i}")
        else:
            y_prompt, y_sample = _edge(x, y=y, route_in=route, res=(g_out, modp, modsr), split_out=True,
                                       name=f"edge_ffn{i}")

    (a_p, a_s) = state["a"]
    rows_p, rows_s = state["b"]
    c_p, c_s = state["c"]
    return (y_prompt.reshape(N_P, T_P, D), y_sample.reshape(N_S, T_S, D), a_p[None], a_s[None],
            rows_p[0], rows_s[0], rows_p[1], rows_s[1], rows_p[2], rows_s[2],
            c_p, c_s, state["d"])
```

```python
import functools
import math

import numpy as np
import jax
import jax.numpy as jnp
from jax import lax
from jax.experimental import pallas as pl
from jax.experimental.pallas import tpu as pltpu

D = 2048
N_P, T_P = 4, 2048
N_S, T_S = 32, 8
ROWS_P = N_P * T_P
ROWS_S = N_S * T_S
ROWS = ROWS_P + ROWS_S
DEPTH = 4
HEAD_DIM = 64
N_HEADS = 24
ATTN_W = N_HEADS * HEAD_DIM
DIL_CFG = ((128, 1), (512, 4), (2048, 16))
N_DIL = len(DIL_CFG)
B_HG = 8
GROUP_W = B_HG * HEAD_DIM
C_WINDOW = 128
C_KV_HEADS = 4
C_GROUP = 6
C_KW = C_KV_HEADS * HEAD_DIM
BLK = 128
SG_GROUPS = 8
SG_GW = D // SG_GROUPS
REL_BUCKETS = 32
REL_MAX_DIST = 2048
N_EXPERTS = 8
RMS_EPS = 1e-6
LN_EPS = 1e-5
NEG = -1e30

LANE = 128
VMEM_LIMIT_CAP = 58 * 1024 * 1024
VMEM_LIMIT_FLOOR = 32 * 1024 * 1024
VMEM_MARGIN = 8 * 1024 * 1024

ROW_BLK = 256
N_ROW_BLKS = ROWS // ROW_BLK
N_PROMPT_BLKS = ROWS_P // ROW_BLK
BLKS_PER_SEQ = T_P // ROW_BLK
SAMPLE_BLK8 = ROWS_P // T_S
DENSE_TM = 768
DENSE_TN = 1024
MOE_TM = 512
MOE_DOWN_TM = 512
GATHER_CHUNK = 512
GATHER_UNROLL = 8

BF16 = jnp.bfloat16
F32 = jnp.float32


def _params(sem, nbytes):
    limit = int(min(max(nbytes + VMEM_MARGIN, VMEM_LIMIT_FLOOR), VMEM_LIMIT_CAP))
    return pltpu.CompilerParams(dimension_semantics=sem, vmem_limit_bytes=limit)


def _weight_changed(be_ref):
    b = pl.program_id(1)
    prev = be_ref[jnp.maximum(b - 1, 0)]
    return jnp.logical_or(b == 0, be_ref[b] != prev)


def _gmm_kernel(be_ref, nu_ref, x_ref, w_ref, o_ref, wc_ref, *, act):
    used = pl.program_id(1) < nu_ref[0]

    @pl.when(used)
    def _():
        @pl.when(_weight_changed(be_ref))
        def _():
            wc_ref[...] = w_ref[...].astype(BF16)

        acc = jnp.dot(x_ref[...].astype(BF16), wc_ref[...], preferred_element_type=F32)
        if act == "gelu":
            acc = jax.nn.gelu(acc)
        o_ref[...] = acc.astype(o_ref.dtype)

    @pl.when(jnp.logical_not(used))
    def _():
        o_ref[...] = jnp.zeros_like(o_ref)


def _gmm_gated_kernel(be_ref, nu_ref, x_ref, wg_ref, wu_ref, o_ref, wgc_ref, wuc_ref):
    used = pl.program_id(1) < nu_ref[0]

    @pl.when(used)
    def _():
        @pl.when(_weight_changed(be_ref))
        def _():
            wgc_ref[...] = wg_ref[...].astype(BF16)
            wuc_ref[...] = wu_ref[...].astype(BF16)

        x = x_ref[...].astype(BF16)
        g = jnp.dot(x, wgc_ref[...], preferred_element_type=F32)
        u = jnp.dot(x, wuc_ref[...], preferred_element_type=F32)
        o_ref[...] = (jax.nn.silu(g) * u).astype(o_ref.dtype)

    @pl.when(jnp.logical_not(used))
    def _():
        o_ref[...] = jnp.zeros_like(o_ref)


def _gmm_specs(tm, k, tn, n_w, col_map):
    def x_map(j, b, be, nu):
        return (jnp.minimum(b, nu[0] - 1), 0)

    def w_map(j, b, be, nu):
        return (be[jnp.minimum(b, nu[0] - 1)], 0, col_map(j))

    in_specs = [pl.BlockSpec((tm, k), x_map)] + [pl.BlockSpec((None, k, tn), w_map)] * n_w
    out_spec = pl.BlockSpec((tm, tn), lambda j, b, be, nu: (b, j))
    return in_specs, out_spec


def _gmm(x, w, block_expert, n_used, *, tm, tn, n_out, out_dtype, act=None, col_map=lambda j: j, name):
    m, k = x.shape
    assert k == w.shape[1] and m % tm == 0 and n_out % tn == 0
    in_specs, out_spec = _gmm_specs(tm, k, tn, 1, col_map)
    nbytes = (2 * tm * k * x.dtype.itemsize + 2 * k * tn * 4 + k * tn * 2
              + 2 * tm * tn * jnp.dtype(out_dtype).itemsize)
    return pl.pallas_call(
        functools.partial(_gmm_kernel, act=act),
        out_shape=jax.ShapeDtypeStruct((m, n_out), out_dtype),
        grid_spec=pltpu.PrefetchScalarGridSpec(
            num_scalar_prefetch=2,
            grid=(n_out // tn, m // tm),
            in_specs=in_specs,
            out_specs=out_spec,
            scratch_shapes=[pltpu.VMEM((k, tn), BF16)],
        ),
        compiler_params=_params(("arbitrary", "arbitrary"), nbytes),
        name=name,
    )(block_expert, n_used, x, w)


def _gmm_gated(x, wg, wu, block_expert, n_used, *, tm, tn, name):
    m, k = x.shape
    n = wg.shape[2]
    assert k == wg.shape[1] and m % tm == 0 and n % tn == 0 and wu.shape == wg.shape
    in_specs, out_spec = _gmm_specs(tm, k, tn, 2, lambda j: j)
    nbytes = 2 * tm * k * x.dtype.itemsize + 4 * k * tn * 4 + 2 * k * tn * 2 + 2 * tm * tn * 2
    return pl.pallas_call(
        _gmm_gated_kernel,
        out_shape=jax.ShapeDtypeStruct((m, n), BF16),
        grid_spec=pltpu.PrefetchScalarGridSpec(
            num_scalar_prefetch=2,
            grid=(n // tn, m // tm),
            in_specs=in_specs,
            out_specs=out_spec,
            scratch_shapes=[pltpu.VMEM((k, tn), BF16), pltpu.VMEM((k, tn), BF16)],
        ),
        compiler_params=_params(("arbitrary", "arbitrary"), nbytes),
        name=name,
    )(block_expert, n_used, x, wg, wu)


def _single(m, tm, which=0):
    return jnp.full((m // tm,), which, jnp.int32), jnp.full((1,), m // tm, jnp.int32)


def _dense(x, w, *, tn, out_dtype, act=None, name, tm=DENSE_TM, which=0, n_out=None, col_map=lambda j: j):
    be, nu = _single(x.shape[0], tm, which)
    return _gmm(x, w, be, nu, tm=tm, tn=tn, n_out=n_out or w.shape[2], out_dtype=out_dtype, act=act,
                col_map=col_map, name=name)


def _rms(x, g):
    return x * lax.rsqrt(jnp.mean(x * x, axis=-1, keepdims=True) + RMS_EPS) * g


def _route(h, wr_ref, br_ref):
    logits = jnp.dot(h, wr_ref[...], preferred_element_type=F32, precision=lax.Precision.HIGHEST) + br_ref[...]
    lane = lax.broadcasted_iota(jnp.int32, logits.shape, 1)
    logits = jnp.where(lane < N_EXPERTS, logits, -jnp.inf)
    m1 = jnp.max(logits, axis=-1, keepdims=True)
    i1 = jnp.min(jnp.where(logits == m1, lane, LANE), axis=-1, keepdims=True)
    rest = jnp.where(lane == i1, -jnp.inf, logits)
    m2 = jnp.max(rest, axis=-1, keepdims=True)
    i2 = jnp.min(jnp.where(rest == m2, lane, LANE), axis=-1, keepdims=True)
    e2 = jnp.exp(m2 - m1)
    g1 = 1.0 / (1.0 + e2)
    g2 = e2 / (1.0 + e2)
    out = jnp.where(lane == 0, i1.astype(F32),
                    jnp.where(lane == 1, i2.astype(F32),
                              jnp.where(lane == 2, g1, jnp.where(lane == 3, g2, 0.0))))
    return out


def _edge_kernel(*refs, split_in, resid, moe_in, norm, route, split_out):
    refs = list(refs)

    def pop(n):
        return [refs.pop(0) for _ in range(n)]

    x_refs = pop(2) if split_in else pop(1) * 2
    if resid:
        y_refs = pop(3) if moe_in else pop(1)
        gr_ref, gate_p, gate_s = pop(3)
    if norm:
        gn_ref, shift_p, scale_p, shift_s, scale_s = pop(5)
        if route:
            wr_ref, br_ref = pop(2)
    if resid:
        xo_refs = pop(2) if split_out else pop(1) * 2
    if norm:
        (h_ref,) = pop(1)
        if route:
            (r_ref,) = pop(1)

    def run(sample):
        x = x_refs[sample][...]
        if resid:
            if moe_in:
                ya_ref, yb_ref, rp_ref = y_refs
                rp = rp_ref[...]
                out = rp[:, 2:3] * ya_ref[...] + rp[:, 3:4] * yb_ref[...]
            else:
                out = y_refs[0][...]
            x = x + (gate_s if sample else gate_p)[...] * _rms(out, gr_ref[...])
            xo_refs[sample][...] = x
        if norm:
            shift = (shift_s if sample else shift_p)[...]
            scale = (scale_s if sample else scale_p)[...]
            h = _rms(x, gn_ref[...]) * (1.0 + scale) + shift
            h_ref[...] = h.astype(h_ref.dtype)
            if route:
                r_ref[...] = _route(h, wr_ref, br_ref)

    i = pl.program_id(0)
    pl.when(i < N_PROMPT_BLKS)(functools.partial(run, 0))
    pl.when(i == N_PROMPT_BLKS)(functools.partial(run, 1))


def _mod_specs(cols, s):
    specs = []
    for c in cols:
        specs.append(pl.BlockSpec((None, None, 1, D),
                                  lambda i, c=c: (s, jnp.minimum(i // BLKS_PER_SEQ, N_P - 1), 0, c)))
    for c in cols:
        specs.append(pl.BlockSpec((None, ROW_BLK, D), lambda i, c=c: (s, 0, c)))
    return specs


def _edge(x, *, y=None, route_in=None, res=None, nrm=None, router=None, split_out=False, name):
    split_in = isinstance(x, tuple)
    resid, norm, moe_in, route = res is not None, nrm is not None, route_in is not None, router is not None
    row = pl.BlockSpec((ROW_BLK, D), lambda i: (i, 0))
    row_p = pl.BlockSpec((ROW_BLK, D), lambda i: (jnp.minimum(i, N_PROMPT_BLKS - 1), 0))
    row_s = pl.BlockSpec((ROW_BLK, D), lambda i: (0, 0))
    gain = pl.BlockSpec((1, D), lambda i: (0, 0))
    lane_row = pl.BlockSpec((ROW_BLK, LANE), lambda i: (i, 0))
    in_specs = [row_p, row_s] if split_in else [row]
    args = list(x) if split_in else [x]
    out_shape, out_specs = [], []
    if resid:
        g, (modp, mods, sub) = res
        if moe_in:
            in_specs += [row, pl.BlockSpec((ROW_BLK, D), lambda i: (i + N_ROW_BLKS, 0)), lane_row]
            args += [y, y, route_in]
        else:
            in_specs += [row]
            args += [y]
        in_specs += [gain] + _mod_specs((2,), sub)
        args += [g.reshape(1, D), modp, mods]
        if split_out:
            out_shape += [jax.ShapeDtypeStruct((ROWS_P, D), F32), jax.ShapeDtypeStruct((ROWS_S, D), F32)]
            out_specs += [row_p, row_s]
        else:
            out_shape += [jax.ShapeDtypeStruct((ROWS, D), F32)]
            out_specs += [row]
    if norm:
        g, (modp, mods, sub) = nrm
        in_specs += [gain] + _mod_specs((0, 1), sub)
        args += [g.reshape(1, D), modp, modp, mods, mods]
        if route:
            in_specs += [pl.BlockSpec((D, LANE), lambda i: (0, 0)), pl.BlockSpec((1, LANE), lambda i: (0, 0))]
            args += list(router)
        out_shape += [jax.ShapeDtypeStruct((ROWS, D), BF16)]
        out_specs += [row]
        if route:
            out_shape += [jax.ShapeDtypeStruct((ROWS, LANE), F32)]
            out_specs += [lane_row]
    n_blocks = len(in_specs) + len(out_specs)
    return pl.pallas_call(
        functools.partial(_edge_kernel, split_in=split_in, resid=resid, moe_in=moe_in, norm=norm, route=route,
                          split_out=split_out),
        out_shape=out_shape,
        grid=(N_ROW_BLKS,),
        in_specs=in_specs,
        out_specs=out_specs,
        compiler_params=_params(("arbitrary",), n_blocks * ROW_BLK * D * 4 * 2),
        name=name,
    )(*args)


def _conv_kernel(gb_ref, gc_ref, z_ref, cw_ref, b1_ref, b2_ref, y_ref, stp_ref, us_ref, tail_ref):
    i = pl.program_id(0)
    u = gc_ref[...] * z_ref[...]
    r = lax.broadcasted_iota(jnp.int32, u.shape, 0)
    roll1 = pltpu.roll(u, 1, axis=0)
    roll2 = pltpu.roll(u, 2, axis=0)

    def emit(um1, um2):
        conv = cw_ref[0:1, :] * um2 + cw_ref[1:2, :] * um1 + cw_ref[2:3, :] * u
        y_ref[...] = (gb_ref[...] * conv).astype(y_ref.dtype)

    @pl.when(i < N_PROMPT_BLKS)
    def _():
        @pl.when(i % BLKS_PER_SEQ == 0)
        def _():
            tail_ref[...] = jnp.zeros_like(tail_ref)

        prev2 = tail_ref[6:7, :]
        prev1 = tail_ref[7:8, :]
        emit(jnp.where(r == 0, prev1, roll1),
             jnp.where(r == 0, prev2, jnp.where(r == 1, prev1, roll2)))
        tail_ref[...] = u[ROW_BLK - 8:ROW_BLK, :]
        stp_ref[...] = u[ROW_BLK - 2:ROW_BLK, :]

    @pl.when(i == N_PROMPT_BLKS)
    def _():
        t = r % T_S
        emit(jnp.where(t == 0, b1_ref[...], roll1), jnp.where(t < 2, b2_ref[...], roll2))
        us_ref[...] = u


def _short_conv(bcz, cache, conv_w, *, name):
    zeros = jnp.zeros((N_S, T_S, D), F32)
    b1 = zeros.at[:, 0].set(cache[:, 1]).reshape(ROWS_S, D)
    b2 = zeros.at[:, 0].set(cache[:, 0]).at[:, 1].set(cache[:, 1]).reshape(ROWS_S, D)

    def col(c):
        return pl.BlockSpec((ROW_BLK, D), lambda i, c=c: (i, c))

    const = pl.BlockSpec((ROW_BLK, D), lambda i: (0, 0))
    y, st_p, u_s = pl.pallas_call(
        _conv_kernel,
        out_shape=[jax.ShapeDtypeStruct((ROWS, D), BF16), jax.ShapeDtypeStruct((N_P, 2, D), F32),
                   jax.ShapeDtypeStruct((ROWS_S, D), F32)],
        grid=(N_ROW_BLKS,),
        in_specs=[col(0), col(1), col(2), pl.BlockSpec((3, D), lambda i: (0, 0)), const, const],
        out_specs=[pl.BlockSpec((ROW_BLK, D), lambda i: (i, 0)),
                   pl.BlockSpec((None, 2, D), lambda i: (jnp.minimum(i // BLKS_PER_SEQ, N_P - 1), 0, 0)),
                   const],
        scratch_shapes=[pltpu.VMEM((8, D), F32)],
        compiler_params=_params(("arbitrary",), ROW_BLK * D * 4 * 7 * 2),
        name=name,
    )(bcz, bcz, bcz, conv_w, b1, b2)
    return y, st_p, u_s.reshape(N_S, T_S, D)[:, T_S - 2:]


def _rel_bucket(dist):
    n = np.maximum(np.asarray(dist), 0)
    exact = REL_BUCKETS // 2
    ratio = np.log(np.maximum(n, 1).astype(np.float32) / exact) / math.log(REL_MAX_DIST / exact)
    large = np.minimum(exact + (ratio * (REL_BUCKETS - exact)).astype(np.int32), REL_BUCKETS - 1)
    return np.where(n < exact, n, large).astype(np.int32)


def _toeplitz(a, n_rows, n_cols):
    h, p = a.shape
    assert n_cols < p
    t = jnp.tile(a, (1, n_rows))[:, :n_rows * (p - 1)].reshape(h, n_rows, p - 1)
    return t[:, :, :n_cols]


def _bias_line(rel_bias, heads, dist, valid):
    tab = rel_bias[_rel_bucket(dist)][:, heads]
    return jnp.where(valid[:, None], tab, NEG).T.astype(F32)


def _band_bias(rel_bias, heads, dil, max_dist):
    p = 3 * BLK
    m = np.arange(p)
    dist = np.where(m < 2 * BLK, BLK - m, BLK + p - m)
    valid = (dist >= 0) & (dist <= max_dist)
    return _toeplitz(_bias_line(rel_bias * LOG2E, heads, dist * dil, valid), BLK, 2 * BLK)


def _sample_bias(rel_bias, heads, win, dil, max_dist):
    lk = win + BLK
    p = lk + T_S
    m = np.arange(p)
    dist = np.where(m < lk, win - m, win + p - m)
    valid = (dist >= 0) & (dist % dil == 0) & (dist <= max_dist)
    return _toeplitz(_bias_line(rel_bias, heads, dist, valid), T_S, lk)


LOG2E = math.log2(math.e)
LN2 = math.log(2.0)


def _softmax_pv(s, v, sink=None):
    m = jnp.max(s, axis=-1, keepdims=True)
    if sink is not None:
        m = jnp.maximum(m, sink)
    p = jnp.exp2(s - m)
    l = jnp.sum(p, axis=-1, keepdims=True)
    if sink is not None:
        l = l + jnp.exp2(sink - m)
    o = jnp.dot(p.astype(BF16), v, preferred_element_type=F32) / l
    return o, (m + jnp.log2(l)) * LN2


def _nt_dot(a, b):
    return lax.dot_general(a, b, (((1,), (1,)), ((), ())), preferred_element_type=F32)


def _banded_kernel(*refs, n_seq, **kw):
    out_refs = refs[-2:] if kw["with_lse"] else refs[-1:]

    @pl.when(pl.program_id(0) < n_seq)
    def _():
        _banded_body(*refs, **kw)

    @pl.when(pl.program_id(0) >= n_seq)
    def _():
        for ref in out_refs:
            ref[...] = jnp.zeros_like(ref)


def _banded_body(*refs, n_heads, group, with_lse, has_sink):
    refs = list(refs)
    sink_ref = refs.pop(0) if has_sink else None
    q_ref, kp_ref, kc_ref, vp_ref, vc_ref, bias_ref = refs[:6]
    o_ref = refs[6]
    lse_ref = refs[7] if with_lse else None
    q = q_ref[...] * (HEAD_DIM ** -0.5 * LOG2E)
    k = jnp.concatenate([kp_ref[...], kc_ref[...]], axis=0).astype(BF16)
    v = jnp.concatenate([vp_ref[...], vc_ref[...]], axis=0).astype(BF16)
    col = lax.broadcasted_iota(jnp.int32, (BLK, 2 * BLK), 1)
    no_prev = jnp.logical_and(pl.program_id(1) == 0, col < BLK)
    low = lax.broadcasted_iota(jnp.int32, (BLK, LANE), 1) < HEAD_DIM
    for t in range(n_heads // 2):
        tile = slice(t * LANE, (t + 1) * LANE)
        q_tile = q[:, tile]
        outs, lses = [], []
        for half in range(2):
            h = 2 * t + half
            hk = h // group
            k_tile = slice(hk // 2 * LANE, (hk // 2 + 1) * LANE)
            k_half = hk % 2
            qh = q_tile if k_half == half else pltpu.roll(q_tile, HEAD_DIM, axis=1)
            qh = jnp.where(low if k_half == 0 else jnp.logical_not(low), qh, 0.0).astype(BF16)
            s = _nt_dot(qh, k[:, k_tile]) + bias_ref[h]
            s = jnp.where(no_prev, NEG, s)
            sink = sink_ref[h] if has_sink else None
            o, lse = _softmax_pv(s, v[:, k_tile], sink)
            outs.append(o if k_half == half else pltpu.roll(o, HEAD_DIM, axis=1))
            lses.append(lse)
        o_ref[:, tile] = jnp.where(low, outs[0], outs[1]).astype(o_ref.dtype)
        if with_lse:
            lse_ref[:, tile] = jnp.where(low, lses[0], lses[1])


def _banded_attention(src, bias, sink=None, *, n_seq, q_col, k_col, v_col, qw, kw, out_dtype, group, with_lse, name):
    n_heads = qw // HEAD_DIM
    has_sink = sink is not None
    n_blk = ROWS_P // BLK // n_seq
    fill_row0 = ROWS_P // BLK
    fill_blks = ROWS_S // BLK
    n_fill_seq = -(-fill_blks // n_blk)

    def cur(col):
        return lambda s, b: (jnp.minimum(s, n_seq - 1) * n_blk + b, col)

    def prev(col):
        return lambda s, b: (jnp.minimum(s, n_seq - 1) * n_blk + jnp.maximum(b - 1, 0), col)

    def out_map(s, b):
        fill = jnp.minimum((s - n_seq) * n_blk + b, fill_blks - 1)
        return (jnp.where(s >= n_seq, fill_row0 + fill, s * n_blk + b), 0)

    in_specs = [pl.BlockSpec((BLK, qw), cur(q_col)),
                pl.BlockSpec((BLK, kw), prev(k_col)), pl.BlockSpec((BLK, kw), cur(k_col)),
                pl.BlockSpec((BLK, kw), prev(v_col)), pl.BlockSpec((BLK, kw), cur(v_col)),
                pl.BlockSpec((n_heads, BLK, 2 * BLK), lambda s, b: (0, 0, 0))]
    args = [src, src, src, src, src, bias]
    if has_sink:
        in_specs = [pl.BlockSpec(memory_space=pltpu.SMEM)] + in_specs
        args = [sink] + args
    out_shapes = [jax.ShapeDtypeStruct((ROWS, qw), out_dtype)]
    out_specs = [pl.BlockSpec((BLK, qw), out_map)]
    if with_lse:
        out_shapes.append(jax.ShapeDtypeStruct((ROWS, qw), F32))
        out_specs.append(pl.BlockSpec((BLK, qw), out_map))
    nbytes = (BLK * qw * 4 * 3 + BLK * kw * 4 * 4) * 2 + n_heads * BLK * 2 * BLK * 4 * 2
    res = pl.pallas_call(
        functools.partial(_banded_kernel, n_seq=n_seq, n_heads=n_heads, group=group, with_lse=with_lse,
                          has_sink=has_sink),
        out_shape=out_shapes,
        grid=(n_seq + n_fill_seq, n_blk),
        in_specs=in_specs,
        out_specs=out_specs,
        compiler_params=_params(("arbitrary", "arbitrary"), nbytes),
        name=name,
    )(*args)
    return res if with_lse else res[0]


def _dil_sample_kernel(q_ref, kn_ref, vn_ref, cache_ref, bc_ref, bn_ref, o_in, lse_in, o_ref, lse_ref, *, n_res):
    del o_in, lse_in
    q = q_ref[...] * (HEAD_DIM ** -0.5)
    qt = jnp.concatenate([q] * B_HG, axis=0)
    row_h = lax.broadcasted_iota(jnp.int32, qt.shape, 0) // T_S
    col_h = lax.broadcasted_iota(jnp.int32, qt.shape, 1) // HEAD_DIM
    qbd = jnp.where(row_h == col_h, qt, 0.0).astype(BF16)
    pad = jnp.zeros((BLK - T_S, GROUP_W), F32)
    kn = jnp.concatenate([kn_ref[...], pad], axis=0).astype(BF16)
    vn = jnp.concatenate([vn_ref[...], pad], axis=0).astype(BF16)
    kv_w = 2 * GROUP_W
    kc = jnp.concatenate([cache_ref[:, r * kv_w:r * kv_w + GROUP_W] for r in range(n_res)], axis=0).astype(BF16)
    vc = jnp.concatenate([cache_ref[:, r * kv_w + GROUP_W:(r + 1) * kv_w] for r in range(n_res)],
                         axis=0).astype(BF16)
    sc = _nt_dot(qbd, kc) + bc_ref[...]
    sn = _nt_dot(qbd, kn) + bn_ref[...]
    m = jnp.maximum(jnp.max(sc, axis=-1, keepdims=True), jnp.max(sn, axis=-1, keepdims=True))
    pc = jnp.exp(sc - m)
    pn = jnp.exp(sn - m)
    l = jnp.sum(pc, axis=-1, keepdims=True) + jnp.sum(pn, axis=-1, keepdims=True)
    big = (jnp.dot(pc.astype(BF16), vc, preferred_element_type=F32)
           + jnp.dot(pn.astype(BF16), vn, preferred_element_type=F32)) / l
    lse = m + jnp.log(l)
    for h in range(B_HG):
        rs = slice(h * T_S, (h + 1) * T_S)
        cs = slice(h * HEAD_DIM, (h + 1) * HEAD_DIM)
        o_ref[:, cs] = big[rs, cs]
        lse_ref[:, cs] = jnp.broadcast_to(lse[rs], (T_S, HEAD_DIM))


def _dil_sample_attention(qkv, cache, bias_c, bias_n, o, lse, *, n_res, name):
    rows = cache.shape[1]
    lb = rows * n_res
    row = lambda c: pl.BlockSpec((T_S, GROUP_W), lambda n, c=c: (SAMPLE_BLK8 + n, c))
    const = lambda n: (0, 0)
    anyspec = pl.BlockSpec(memory_space=pl.ANY)
    nbytes = lb * 2 * GROUP_W * 4 * 2 + 64 * lb * 4 * 6
    return pl.pallas_call(
        functools.partial(_dil_sample_kernel, n_res=n_res),
        out_shape=[jax.ShapeDtypeStruct((ROWS, GROUP_W), F32)] * 2,
        grid=(N_S,),
        in_specs=[row(0), row(1), row(2),
                  pl.BlockSpec((None, rows, n_res * 2 * GROUP_W), lambda n: (n, 0, 0)),
                  pl.BlockSpec((B_HG * T_S, lb), const), pl.BlockSpec((B_HG * T_S, BLK), const),
                  anyspec, anyspec],
        out_specs=[row(0)] * 2,
        input_output_aliases={6: 0, 7: 1},
        compiler_params=_params(("parallel",), nbytes),
        name=name,
    )(qkv, qkv, qkv, cache, bias_c, bias_n, o, lse)


def _merge_kernel(o0, o1, o2, l0, l1, l2, out_ref):
    ls = [l0[...], l1[...], l2[...]]
    m = jnp.maximum(jnp.maximum(ls[0], ls[1]), ls[2])
    es = [jnp.exp(l - m) for l in ls]
    tot = es[0] + es[1] + es[2]
    for g, o in enumerate((o0, o1, o2)):
        out_ref[:, g * GROUP_W:(g + 1) * GROUP_W] = (o[...] * (es[g] / tot)).astype(out_ref.dtype)


def _merge_dilations(outs, lses, *, name):
    spec = pl.BlockSpec((ROW_BLK, GROUP_W), lambda i: (i, 0))
    return pl.pallas_call(
        _merge_kernel,
        out_shape=jax.ShapeDtypeStruct((ROWS, ATTN_W), BF16),
        grid=(N_ROW_BLKS,),
        in_specs=[spec] * 6,
        out_specs=pl.BlockSpec((ROW_BLK, ATTN_W), lambda i: (i, 0)),
        compiler_params=_params(("parallel",), ROW_BLK * GROUP_W * 4 * 8 * 2),
        name=name,
    )(*outs, *lses)


def _swa_sample_kernel(q_ref, kvn_ref, cache_ref, bc_ref, bn_ref, sink_ref, o_in, o_ref):
    del o_in
    q = q_ref[...] * (HEAD_DIM ** -0.5)
    pad = jnp.zeros((BLK - T_S, 2 * C_KW), F32)
    kvn = jnp.concatenate([kvn_ref[...], pad], axis=0).astype(BF16)
    kvc = cache_ref[...].astype(BF16)
    for hk in range(C_KV_HEADS):
        heads = range(hk * C_GROUP, (hk + 1) * C_GROUP)
        qh = jnp.concatenate([q[:, h * HEAD_DIM:(h + 1) * HEAD_DIM] for h in heads], axis=0).astype(BF16)
        ks = slice(hk * HEAD_DIM, (hk + 1) * HEAD_DIM)
        vs = slice(C_KW + hk * HEAD_DIM, C_KW + (hk + 1) * HEAD_DIM)
        sc = _nt_dot(qh, kvc[:, ks]) + bc_ref[hk]
        sn = _nt_dot(qh, kvn[:, ks]) + bn_ref[hk]
        sink = sink_ref[hk]
        m = jnp.maximum(jnp.maximum(jnp.max(sc, axis=-1, keepdims=True),
                                    jnp.max(sn, axis=-1, keepdims=True)), sink)
        pc = jnp.exp(sc - m)
        pn = jnp.exp(sn - m)
        l = jnp.sum(pc, axis=-1, keepdims=True) + jnp.sum(pn, axis=-1, keepdims=True) + jnp.exp(sink - m)
        o = (jnp.dot(pc.astype(BF16), kvc[:, vs], preferred_element_type=F32)
             + jnp.dot(pn.astype(BF16), kvn[:, vs], preferred_element_type=F32)) / l
        for gi, h in enumerate(heads):
            o_ref[:, h * HEAD_DIM:(h + 1) * HEAD_DIM] = o[gi * T_S:(gi + 1) * T_S].astype(o_ref.dtype)


def _swa_sample_attention(qkv, cache, bias_c, bias_n, sink_tab, o, *, name):
    rows = C_GROUP * T_S
    const = lambda n: (0, 0, 0)
    return pl.pallas_call(
        _swa_sample_kernel,
        out_shape=jax.ShapeDtypeStruct((ROWS, ATTN_W), o.dtype),
        grid=(N_S,),
        in_specs=[pl.BlockSpec((T_S, ATTN_W), lambda n: (SAMPLE_BLK8 + n, 0)),
                  pl.BlockSpec((T_S, 2 * C_KW), lambda n: (SAMPLE_BLK8 + n, ATTN_W // (2 * C_KW))),
                  pl.BlockSpec((None, C_WINDOW, 2 * C_KW), lambda n: (n, 0, 0)),
                  pl.BlockSpec((C_KV_HEADS, rows, BLK), const), pl.BlockSpec((C_KV_HEADS, rows, BLK), const),
                  pl.BlockSpec((C_KV_HEADS, rows, 1), const),
                  pl.BlockSpec(memory_space=pl.ANY)],
        out_specs=pl.BlockSpec((T_S, ATTN_W), lambda n: (SAMPLE_BLK8 + n, 0)),
        input_output_aliases={6: 0},
        compiler_params=_params(("parallel",), 1 << 20),
        name=name,
    )(qkv, qkv, cache, bias_c, bias_n, sink_tab, o)


def _sgu_kernel(u_ref, v_ref, w_ref, b_ref, ws_ref, bs_ref, y_ref, vn_ref):
    i = pl.program_id(0)
    v = v_ref[...]
    mu = jnp.mean(v, axis=-1, keepdims=True)
    var = jnp.mean(jnp.square(v - mu), axis=-1, keepdims=True)
    vn = (v - mu) * lax.rsqrt(var + LN_EPS)
    vb = vn.astype(BF16)

    @pl.when(i < N_PROMPT_BLKS)
    def _():
        causal = (lax.broadcasted_iota(jnp.int32, (BLK, BLK), 0) >= lax.broadcasted_iota(jnp.int32, (BLK, BLK), 1))
        for g in range(SG_GROUPS):
            cs = slice(g * SG_GW, (g + 1) * SG_GW)
            wg = jnp.where(causal, w_ref[g], 0.0).astype(BF16)
            for c in range(ROW_BLK // BLK):
                rs = slice(c * BLK, (c + 1) * BLK)
                s = jnp.dot(wg, vb[rs, cs], preferred_element_type=F32) + b_ref[g]
                y_ref[rs, cs] = (u_ref[rs, cs] * s).astype(y_ref.dtype)

    @pl.when(i == N_PROMPT_BLKS)
    def _():
        vn_ref[...] = vn
        for g in range(SG_GROUPS):
            cs = slice(g * SG_GW, (g + 1) * SG_GW)
            s = jnp.dot(ws_ref[g].astype(BF16), vb[:, cs], preferred_element_type=F32) + bs_ref[g]
            y_ref[:, cs] = (u_ref[:, cs] * s).astype(y_ref.dtype)


def _spatial_gate(uv, w_s, b_s, *, name):
    r = np.arange(ROWS_S)
    same_seq = (r[:, None] // T_S) == (r[None, :] // T_S)
    causal = (r[:, None] % T_S) >= (r[None, :] % T_S)
    w8 = jnp.tile(w_s[:, :T_S, :T_S], (1, N_S, N_S))
    ws = jnp.where((same_seq & causal)[None], w8, 0.0)
    bs = jnp.tile(b_s[:, :T_S], (1, N_S)).reshape(SG_GROUPS, ROWS_S, 1)
    const3 = lambda i: (0, 0, 0)
    nbytes = ROW_BLK * D * 4 * 4 * 2 + SG_GROUPS * (BLK * BLK + ROW_BLK * ROW_BLK) * 4 * 2
    return pl.pallas_call(
        _sgu_kernel,
        out_shape=[jax.ShapeDtypeStruct((ROWS, D), BF16), jax.ShapeDtypeStruct((ROWS_S, D), F32)],
        grid=(N_ROW_BLKS,),
        in_specs=[pl.BlockSpec((ROW_BLK, D), lambda i: (i, 0)), pl.BlockSpec((ROW_BLK, D), lambda i: (i, 1)),
                  pl.BlockSpec((SG_GROUPS, BLK, BLK), const3), pl.BlockSpec((SG_GROUPS, BLK, 1), const3),
                  pl.BlockSpec((SG_GROUPS, ROWS_S, ROWS_S), const3), pl.BlockSpec((SG_GROUPS, ROWS_S, 1), const3)],
        out_specs=[pl.BlockSpec((ROW_BLK, D), lambda i: (i, 0)), pl.BlockSpec((ROWS_S, D), lambda i: (0, 0))],
        compiler_params=_params(("arbitrary",), nbytes),
        name=name,
    )(uv, uv, w_s, b_s.reshape(SG_GROUPS, BLK, 1), ws, bs)


def _gather_kernel(idx_ref, src_ref, o_ref, sem):
    base = pl.program_id(0) * GATHER_CHUNK

    def row_copy(r):
        return pltpu.make_async_copy(src_ref.at[idx_ref[base + r]], o_ref.at[r], sem)

    def start(g, carry):
        for u in range(GATHER_UNROLL):
            row_copy(g * GATHER_UNROLL + u).start()
        return carry

    def wait(g, carry):
        for u in range(GATHER_UNROLL):
            row_copy(g * GATHER_UNROLL + u).wait()
        return carry

    lax.fori_loop(0, GATHER_CHUNK // GATHER_UNROLL, start, 0)
    lax.fori_loop(0, GATHER_CHUNK // GATHER_UNROLL, wait, 0)


def _gather_rows(src, idx, *, name):
    n = src.shape[0]
    r = idx.shape[0]
    assert r % GATHER_CHUNK == 0
    slab = (D // LANE, LANE)
    out = pl.pallas_call(
        _gather_kernel,
        out_shape=jax.ShapeDtypeStruct((r,) + slab, src.dtype),
        grid_spec=pltpu.PrefetchScalarGridSpec(
            num_scalar_prefetch=1,
            grid=(r // GATHER_CHUNK,),
            in_specs=[pl.BlockSpec(memory_space=pl.ANY)],
            out_specs=pl.BlockSpec((GATHER_CHUNK,) + slab, lambda c, idx: (c, 0, 0)),
            scratch_shapes=[pltpu.SemaphoreType.DMA],
        ),
        compiler_params=_params(("arbitrary",), 2 * GATHER_CHUNK * D * src.dtype.itemsize),
        name=name,
    )(idx, src.reshape((n,) + slab))
    return out.reshape(r, D)


N_ASSIGN = 2 * ROWS
_MOE_ROWS_MIN = N_ASSIGN + N_EXPERTS * (MOE_TM - 1)
_MOE_UNIT = MOE_TM * GATHER_CHUNK // math.gcd(MOE_TM, GATHER_CHUNK)
MOE_ROWS = -(-_MOE_ROWS_MIN // _MOE_UNIT) * _MOE_UNIT
assert N_ASSIGN % GATHER_CHUNK == 0


def _moe_plan(route):
    e_flat = route[:, 0:2].astype(jnp.int32).reshape(-1)
    onehot = (e_flat[:, None] == jnp.arange(N_EXPERTS, dtype=jnp.int32)[None, :]).astype(jnp.int32)
    csum = jnp.cumsum(onehot, axis=0)
    rank = jnp.sum(csum * onehot, axis=1) - 1
    counts = csum[-1]
    padded = (counts + MOE_TM - 1) // MOE_TM * MOE_TM
    pad_end = jnp.cumsum(padded)
    pad_start = pad_end - padded
    dest = pad_start[e_flat] + rank
    token = jnp.arange(N_ASSIGN, dtype=jnp.int32) // 2
    src_row = (jnp.arange(MOE_ROWS, dtype=jnp.int32) % ROWS).at[dest].set(token)
    block_start = jnp.arange(MOE_ROWS // MOE_TM, dtype=jnp.int32) * MOE_TM
    block_expert = jnp.minimum(jnp.searchsorted(pad_end, block_start, side="right"), N_EXPERTS - 1).astype(jnp.int32)
    slots = jnp.concatenate([dest[0::2], dest[1::2]])
    n_used = (pad_end[-1:] // MOE_TM).astype(jnp.int32)
    return slots, src_row, block_expert, n_used


def _moe(h, route, w_gate, w_up, w_down, layer, *, name):
    slots, src_row, block_expert, n_used = _moe_plan(route)
    block_expert = block_expert + layer * N_EXPERTS
    xs = _gather_rows(h, src_row, name=name + "_dispatch")
    act = _gmm_gated(xs, w_gate, w_up, block_expert, n_used, tm=MOE_TM, tn=1024, name=name + "_up")
    split = MOE_TM // MOE_DOWN_TM
    ys = _gmm(act, w_down, jnp.repeat(block_expert, split), n_used * split, tm=MOE_DOWN_TM, tn=512, n_out=D,
              out_dtype=F32, name=name + "_down")
    return _gather_rows(ys, slots, name=name + "_combine")


def _mixer_conv(h, cache, w_in, conv_w, w_out):
    bcz = _dense(h, w_in, tn=DENSE_TN, out_dtype=F32, name="a_in")
    y, st_p, st_s = _short_conv(bcz, cache, conv_w, name="a_conv")
    out = _dense(y, w_out, tn=DENSE_TN, out_dtype=F32, name="a_out")
    return out, (st_p, st_s)


def _to_residues(a, dil):
    if dil == 1:
        return a
    w = a.shape[1]
    p = a[:ROWS_P].reshape(N_P, T_P // dil, dil, w).swapaxes(1, 2).reshape(ROWS_P, w)
    return jnp.concatenate([p, a[ROWS_P:]], axis=0)


def _from_residues(a, dil):
    if dil == 1:
        return a
    w = a.shape[1]
    p = a[:ROWS_P].reshape(N_P, dil, T_P // dil, w).swapaxes(1, 2).reshape(ROWS_P, w)
    return jnp.concatenate([p, a[ROWS_P:]], axis=0)


def _mixer_dilated(h, caches, w_qkv, w_o, rel_bias):
    outs, lses, rows_p, rows_s = [], [], [], []
    for gi, (win, dil) in enumerate(DIL_CFG):
        heads = slice(gi * B_HG, (gi + 1) * B_HG)
        qkv = _dense(_to_residues(h, dil), w_qkv, tn=GROUP_W, out_dtype=F32, n_out=3 * GROUP_W,
                     col_map=lambda j, gi=gi: N_DIL * j + gi, name=f"b_qkv{gi}")
        o, lse = _banded_attention(qkv, _band_bias(rel_bias, heads, dil, win // dil),
                                   n_seq=N_P * dil, q_col=0, k_col=1, v_col=2, qw=GROUP_W, kw=GROUP_W,
                                   out_dtype=F32, group=1, with_lse=True, name=f"b_attn_prompt{gi}")
        bias = _sample_bias(rel_bias, heads, win, dil, win).reshape(B_HG * T_S, win + BLK)
        bias_c, bias_n = bias[:, :win], bias[:, win:]
        if dil >= 2 * T_S:
            n_res = T_S
            cache = caches[gi].reshape(N_S, win // dil, dil * 2 * GROUP_W)
            bias_c = bias_c.reshape(B_HG * T_S, win // dil, dil)[:, :, :T_S].swapaxes(1, 2).reshape(B_HG * T_S, -1)
        else:
            n_res = 1
            cache = caches[gi].reshape(N_S, win, 2 * GROUP_W)
        o, lse = _dil_sample_attention(qkv, cache, bias_c, bias_n, o, lse, n_res=n_res, name=f"b_attn_sample{gi}")
        outs.append(_from_residues(o, dil))
        lses.append(_from_residues(lse, dil))
        keep = min(win, T_P)
        seq, tail = T_P // dil, keep // dil
        kv = jnp.stack([qkv[(s + 1) * seq - tail:(s + 1) * seq, GROUP_W:] for s in range(N_P * dil)])
        kv = kv.reshape(N_P, dil, tail, 2, GROUP_W)
        rows_p.append(kv.swapaxes(1, 2).reshape(1, N_P, keep, 2, B_HG, HEAD_DIM))
        rows_s.append(qkv[ROWS_P:].reshape(N_S, T_S, 3, GROUP_W)[:, :, 1:]
                      .reshape(1, N_S, T_S, 2, B_HG, HEAD_DIM))
    merged = _merge_dilations(outs, lses, name="b_merge")
    out = _dense(merged, w_o, tn=DENSE_TN, out_dtype=F32, name="b_out")
    return out, (rows_p, rows_s)


def _mixer_swa(h, cache, w_qkv, sink, w_o, rel_bias):
    qkv = _dense(h, w_qkv, tn=DENSE_TN, out_dtype=F32, name="c_qkv")
    heads = slice(0, N_HEADS)
    o = _banded_attention(qkv, _band_bias(rel_bias, heads, 1, C_WINDOW - 1), sink.astype(F32) * LOG2E,
                          n_seq=N_P, q_col=0, k_col=ATTN_W // C_KW, v_col=ATTN_W // C_KW + 1, qw=ATTN_W, kw=C_KW,
                          out_dtype=BF16, group=C_GROUP, with_lse=False, name="c_attn_prompt")
    bias = _sample_bias(rel_bias, heads, C_WINDOW, 1, C_WINDOW - 1)
    bias = bias.reshape(C_KV_HEADS, C_GROUP * T_S, 2 * BLK)
    sink_tab = jnp.repeat(sink.astype(F32).reshape(C_KV_HEADS, C_GROUP), T_S, axis=1)[:, :, None]
    o = _swa_sample_attention(qkv, cache.reshape(N_S, C_WINDOW, 2 * C_KW), bias[:, :, :BLK], bias[:, :, BLK:],
                              sink_tab, o, name="c_attn_sample")
    kv_p = jnp.stack([qkv[(n + 1) * T_P - C_WINDOW:(n + 1) * T_P, ATTN_W:] for n in range(N_P)])
    kv_p = kv_p.reshape(1, N_P, C_WINDOW, 2, C_KV_HEADS, HEAD_DIM)
    kv_s = qkv[ROWS_P:, ATTN_W:].reshape(1, N_S, T_S, 2, C_KV_HEADS, HEAD_DIM)
    out = _dense(o, w_o, tn=DENSE_TN, out_dtype=F32, name="c_out")
    return out, (kv_p, kv_s)


def _mixer_sgu(h, w_uv, w_s, b_s, w_out):
    uv = _dense(h, w_uv, tn=DENSE_TN, out_dtype=F32, act="gelu", name="d_uv")
    y, vn_s = _spatial_gate(uv, w_s, b_s, name="d_gate")
    out = _dense(y, w_out, tn=DENSE_TN, out_dtype=F32, name="d_out")
    return out, vn_s.reshape(1, N_S, T_S, D)


def _modulations(c_prompt, c_sample, ada_w, ada_b):
    n_sub = 2 * DEPTH
    pad_rows = 48
    cond = jax.nn.silu(jnp.concatenate([c_prompt, c_sample], axis=0))
    cond = jnp.concatenate([cond, jnp.zeros((pad_rows - N_P - N_S, D), F32)], axis=0).astype(BF16)
    x = jnp.tile(cond, (n_sub, 1))
    mod = _gmm(x, ada_w.reshape(n_sub, D, 3 * D), jnp.arange(n_sub, dtype=jnp.int32),
               jnp.full((1,), n_sub, jnp.int32), tm=pad_rows, tn=512, n_out=3 * D, out_dtype=F32, name="ada")
    mod = mod.reshape(n_sub, pad_rows, 3 * D) + ada_b.reshape(n_sub, 1, 3 * D)
    modp = mod[:, :N_P].reshape(n_sub, N_P, 1, 3 * D)
    mods = jnp.repeat(mod[:, N_P:N_P + N_S], T_S, axis=1)
    return [(modp, mods, s) for s in range(n_sub)]


def kernel(x_prompt, x_sample, c_prompt, c_sample, cache_a_conv, cache_b_kv0, cache_b_kv1, cache_b_kv2, cache_c_kv, norm_g, ada_w, ada_b, rel_bias, a_w_in, a_conv_w, a_w_out, b_w_qkv, b_w_o, c_w_qkv, c_sink, c_w_o, d_w_uv, d_w_s, d_b_s, d_w_out, f_w_gate, f_w_up, f_w_down, m_w_router, m_b_router, m_w_gate, m_w_up, m_w_down):
    mods = _modulations(c_prompt, c_sample, ada_w, ada_b)
    n_moe = m_w_gate.shape[0]
    mw_gate = m_w_gate.reshape((n_moe * N_EXPERTS,) + m_w_gate.shape[2:])
    mw_up = m_w_up.reshape((n_moe * N_EXPERTS,) + m_w_up.shape[2:])
    mw_down = m_w_down.reshape((n_moe * N_EXPERTS,) + m_w_down.shape[2:])

    def sub(i, j):
        return norm_g[i, 2 * j], norm_g[i, 2 * j + 1], mods[2 * i + j]

    def router(fi):
        wr = jnp.zeros((D, LANE), F32).at[:, :N_EXPERTS].set(m_w_router[fi])
        br = jnp.zeros((1, LANE), F32).at[0, :N_EXPERTS].set(m_b_router[fi])
        return wr, br

    x = (x_prompt.reshape(ROWS_P, D), x_sample.reshape(ROWS_S, D))
    g_in, _, mod_in = sub(0, 0)
    (h,) = _edge(x, nrm=(g_in, mod_in), name="norm_mix0")
    state = {}
    for i in range(DEPTH):
        kind = i % 4
        if kind == 0:
            out, state["a"] = _mixer_conv(h, cache_a_conv[0], a_w_in, a_conv_w[0], a_w_out)
        elif kind == 1:
            out, state["b"] = _mixer_dilated(h, (cache_b_kv0[0], cache_b_kv1[0], cache_b_kv2[0]),
                                             b_w_qkv, b_w_o, rel_bias)
        elif kind == 2:
            out, state["c"] = _mixer_swa(h, cache_c_kv[0], c_w_qkv, c_sink[0], c_w_o, rel_bias)
        else:
            out, state["d"] = _mixer_sgu(h, d_w_uv, d_w_s[0], d_b_s[0], d_w_out)

        _, g_out, mod_out = sub(i, 0)
        g_in, _, mod_in = sub(i, 1)
        fi = i // 2
        moe = i % 2 == 1
        res = _edge(x, y=out, res=(g_out, mod_out), nrm=(g_in, mod_in),
                    router=router(fi) if moe else None, name=f"edge_mix{i}")
        x, h = res[0], res[1]
        if moe:
            route = res[2]
            y = _moe(h, route, mw_gate, mw_up, mw_down, fi, name=f"moe{i}")
        else:
            route = None
            be, nu = _single(ROWS, DENSE_TM, fi)
            act = _gmm_gated(h, f_w_gate, f_w_up, be, nu, tm=DENSE_TM, tn=512, name=f"ffn_up{i}")
            y = _dense(act, f_w_down, tn=512, out_dtype=F32, name=f"ffn_down{i}", which=fi)

        _, g_out, mod_out = sub(i, 1)
        if i + 1 < DEPTH:
            g_in, _, mod_in = sub(i + 1, 0)
            x, h = _edge(x, y=y, route_in=route, res=(g_out, mod_out), nrm=(g_in, mod_in),
                         name=f"edge_ffn{i}")
        else:
            y_prompt, y_sample = _edge(x, y=y, route_in=route, res=(g_out, mod_out), split_out=True,
                                       name=f"edge_ffn{i}")

    (a_p, a_s) = state["a"]
    rows_p, rows_s = state["b"]
    c_p, c_s = state["c"]
    return (y_prompt.reshape(N_P, T_P, D), y_sample.reshape(N_S, T_S, D), a_p[None], a_s[None],
            rows_p[0], rows_s[0], rows_p[1], rows_s[1], rows_p[2], rows_s[2],
            c_p, c_s, state["d"])
```
